```python
import jax, jax.numpy as jnp
from jax import lax
import numpy as np

D_MODEL = 2048
BATCH = 8
SEQ = 2048
DEPTH = 4

N_META = 16
SB_HEADS = 8
SB_HEAD_DIM = D_MODEL // 16
SB_WIDTH = SB_HEADS * SB_HEAD_DIM
POOL_WINDOWS = (2, 4, 8, 16)
POOL_GROUPS = len(POOL_WINDOWS)
POOL_WIDTH = D_MODEL // 2
POOL_GROUP_DIM = POOL_WIDTH // POOL_GROUPS
Q_BLOCK = 128
RMS_EPS = 1e-6
IN_SIZES = (SB_WIDTH, SB_WIDTH, SB_WIDTH, SB_WIDTH, POOL_WIDTH, POOL_WIDTH, D_MODEL, D_MODEL)
IN_COLS = sum(IN_SIZES)

kernel_name = "hybrid_stickbreak_pool_gated_trunk"


def rmsnorm(x, gain):
    xf = x.astype(jnp.float32)
    y = xf * lax.rsqrt(jnp.mean(xf * xf, axis=-1, keepdims=True) + RMS_EPS)
    return (y * gain.astype(jnp.float32)).astype(x.dtype)


def stick_breaking_block(q_blk, k_pre, v_pre, q0):
    nq = q_blk.shape[1]
    nk = k_pre.shape[1]
    z = jnp.einsum('bqhd,bkhd->bhqk', q_blk.astype(jnp.float32), k_pre.astype(jnp.float32)) * (SB_HEAD_DIM ** -0.5)
    qpos = q0 + jnp.arange(nq)
    kpos = jnp.arange(nk)
    causal = kpos[None, :] < qpos[:, None]
    log_1m_beta = jnp.where(causal, jax.nn.log_sigmoid(-z), 0.0)
    between = lax.cumsum(log_1m_beta, axis=3, reverse=True) - log_1m_beta
    a = jnp.where(causal, jnp.exp(jax.nn.log_sigmoid(z) + between), 0.0)
    o = jnp.einsum('bhqk,bkhd->bqhd', a, v_pre.astype(jnp.float32))
    return o


def stick_breaking_attention(q, k, v):
    L = q.shape[1]
    bounds = [(0, N_META)] + [(s, min(s + Q_BLOCK, L)) for s in range(N_META, L, Q_BLOCK)]
    outs = [stick_breaking_block(q[:, s:e], k[:, :e], v[:, :e], s) for (s, e) in bounds]
    return jnp.concatenate(outs, axis=1)


def multiscale_pool(u):
    L = u.shape[1]
    wmax = max(POOL_WINDOWS)
    c = jnp.cumsum(jnp.pad(u, ((0, 0), (wmax, 0), (0, 0))), axis=1)
    steps = jnp.arange(1, L + 1, dtype=jnp.float32)[None, :, None]
    outs = []
    for g, w in enumerate(POOL_WINDOWS):
        sl = slice(g * POOL_GROUP_DIM, (g + 1) * POOL_GROUP_DIM)
        win_sum = c[:, wmax:, sl] - c[:, wmax - w:wmax - w + L, sl]
        cnt = jnp.minimum(steps, float(w))
        outs.append(win_sum / cnt - u[:, :, sl])
    return jnp.stack(outs, axis=2)


def hybrid_layer(x, gain, w_in, pool_w, pool_scale, w_attn_up, w_pool_up, w_out):
    B, L, _ = x.shape
    h = rmsnorm(x, gain)
    zin = h @ w_in.astype(h.dtype)
    cuts = np.cumsum(IN_SIZES)[:-1].tolist()
    q, k, v, g_attn, u, g_pool, m_attn, m_pool = jnp.split(zin, cuts, axis=-1)
    q = q.reshape(B, L, SB_HEADS, SB_HEAD_DIM)
    k = k.reshape(B, L, SB_HEADS, SB_HEAD_DIM)
    v = v.reshape(B, L, SB_HEADS, SB_HEAD_DIM)
    o_attn = stick_breaking_attention(q, k, v).reshape(B, L, SB_WIDTH)
    o_attn = (o_attn * jax.nn.silu(g_attn.astype(jnp.float32))).astype(x.dtype)
    y_attn = o_attn @ w_attn_up.astype(x.dtype)
    pooled = multiscale_pool(u.astype(jnp.float32))
    mixed = jnp.einsum('blgc,gcd->blgd', pooled, pool_w.astype(jnp.float32)).reshape(B, L, POOL_WIDTH)
    o_pool = (mixed * pool_scale.astype(jnp.float32) * jax.nn.silu(g_pool.astype(jnp.float32))).astype(x.dtype)
    y_pool = o_pool @ w_pool_up.astype(x.dtype)
    merged = jax.nn.sigmoid(m_attn) * y_attn + jax.nn.sigmoid(m_pool) * y_pool
    return x + merged @ w_out.astype(x.dtype)


def _fwd_setup_inputs(seed: int = 0) -> dict:
    key = jax.random.key(seed)
    ks = jax.random.split(key, 10)
    f32 = jnp.float32
    x = jax.random.normal(ks[0], (BATCH, SEQ, D_MODEL), f32)
    meta_tokens = jax.random.normal(ks[1], (N_META, D_MODEL), f32)
    norm_gain = 1.0 + 0.02 * jax.random.normal(ks[2], (DEPTH, D_MODEL), f32)
    w_in = jax.random.normal(ks[3], (DEPTH, D_MODEL, IN_COLS), f32) * D_MODEL ** -0.5
    pool_w = jax.random.normal(ks[4], (DEPTH, POOL_GROUPS, POOL_GROUP_DIM, POOL_GROUP_DIM), f32) * POOL_GROUP_DIM ** -0.5
    pool_scale = 1.0 + 0.1 * jax.random.normal(ks[5], (DEPTH, POOL_WIDTH), f32)
    w_attn_up = jax.random.normal(ks[6], (DEPTH, SB_WIDTH, D_MODEL), f32) * SB_WIDTH ** -0.5
    w_pool_up = jax.random.normal(ks[7], (DEPTH, POOL_WIDTH, D_MODEL), f32) * POOL_WIDTH ** -0.5
    w_out = jax.random.normal(ks[8], (DEPTH, D_MODEL, D_MODEL), f32) * D_MODEL ** -0.5
    final_gain = 1.0 + 0.02 * jax.random.normal(ks[9], (D_MODEL,), f32)
    return {"x": x, "meta_tokens": meta_tokens, "norm_gain": norm_gain, "w_in": w_in,
            "pool_w": pool_w, "pool_scale": pool_scale, "w_attn_up": w_attn_up,
            "w_pool_up": w_pool_up, "w_out": w_out, "final_gain": final_gain}


def _fwd_reference(x, meta_tokens, norm_gain, w_in, pool_w, pool_scale, w_attn_up, w_pool_up, w_out, final_gain):
    B = x.shape[0]
    meta = jnp.broadcast_to(meta_tokens.astype(x.dtype)[None], (B, N_META, x.shape[2]))
    hs = jnp.concatenate([meta, x], axis=1)
    for layer in range(DEPTH):
        hs = hybrid_layer(hs, norm_gain[layer], w_in[layer], pool_w[layer], pool_scale[layer],
                          w_attn_up[layer], w_pool_up[layer], w_out[layer])
    return rmsnorm(hs, final_gain)[:, N_META:]


import jax as _jax
import jax.numpy as _jnp

TWIN_FORMAT = 'train_step'
FWD_PARAMS = ['x', 'meta_tokens', 'norm_gain', 'w_in', 'pool_w', 'pool_scale', 'w_attn_up', 'w_pool_up', 'w_out', 'final_gain']
TWIN_WEIGHTS = ['meta_tokens', 'norm_gain', 'w_in', 'pool_w', 'pool_scale', 'w_attn_up', 'w_pool_up', 'w_out', 'final_gain']
TWIN_DIFF_INPUT = 'x'
TWIN_INPUTS = ['x', 'meta_tokens', 'norm_gain', 'w_in', 'pool_w', 'pool_scale', 'w_attn_up', 'w_pool_up', 'w_out', 'final_gain', 'loss_target', 'm_meta_tokens', 'm_norm_gain', 'm_w_in', 'm_pool_w', 'm_pool_scale', 'm_w_attn_up', 'm_w_pool_up', 'm_w_out', 'm_final_gain', 'v_meta_tokens', 'v_norm_gain', 'v_w_in', 'v_pool_w', 'v_pool_scale', 'v_w_attn_up', 'v_w_pool_up', 'v_w_out', 'v_final_gain']
TWIN_OUTPUTS = ['loss', 'grad_x', 'grad_meta_tokens', 'grad_norm_gain', 'grad_w_in', 'grad_pool_w', 'grad_pool_scale', 'grad_w_attn_up', 'grad_w_pool_up', 'grad_w_out', 'grad_final_gain', 'delta_meta_tokens', 'delta_norm_gain', 'delta_w_in', 'delta_pool_w', 'delta_pool_scale', 'delta_w_attn_up', 'delta_w_pool_up', 'delta_w_out', 'delta_final_gain', 'new_m_meta_tokens', 'new_m_norm_gain', 'new_m_w_in', 'new_m_pool_w', 'new_m_pool_scale', 'new_m_w_attn_up', 'new_m_w_pool_up', 'new_m_w_out', 'new_m_final_gain', 'new_v_meta_tokens', 'new_v_norm_gain', 'new_v_w_in', 'new_v_pool_w', 'new_v_pool_scale', 'new_v_w_attn_up', 'new_v_w_pool_up', 'new_v_w_out', 'new_v_final_gain']
TWIN_LEAF_KINDS = {'loss': 'loss', 'grad_x': 'grad_x', 'grad_meta_tokens': 'grad_w', 'grad_norm_gain': 'grad_w', 'grad_w_in': 'grad_w', 'grad_pool_w': 'grad_w', 'grad_pool_scale': 'grad_w', 'grad_w_attn_up': 'grad_w', 'grad_w_pool_up': 'grad_w', 'grad_w_out': 'grad_w', 'grad_final_gain': 'grad_w', 'delta_meta_tokens': 'delta_w', 'delta_norm_gain': 'delta_w', 'delta_w_in': 'delta_w', 'delta_pool_w': 'delta_w', 'delta_pool_scale': 'delta_w', 'delta_w_attn_up': 'delta_w', 'delta_w_pool_up': 'delta_w', 'delta_w_out': 'delta_w', 'delta_final_gain': 'delta_w', 'new_m_meta_tokens': 'new_m', 'new_m_norm_gain': 'new_m', 'new_m_w_in': 'new_m', 'new_m_pool_w': 'new_m', 'new_m_pool_scale': 'new_m', 'new_m_w_attn_up': 'new_m', 'new_m_w_pool_up': 'new_m', 'new_m_w_out': 'new_m', 'new_m_final_gain': 'new_m', 'new_v_meta_tokens': 'new_v', 'new_v_norm_gain': 'new_v', 'new_v_w_in': 'new_v', 'new_v_pool_w': 'new_v', 'new_v_pool_scale': 'new_v', 'new_v_w_attn_up': 'new_v', 'new_v_w_pool_up': 'new_v', 'new_v_w_out': 'new_v', 'new_v_final_gain': 'new_v'}


def _forward(args):
    return _fwd_reference(*[args[k] for k in FWD_PARAMS])


def _output_shape():
    out = _jax.eval_shape(lambda: _forward(_fwd_setup_inputs(0)))
    return out.shape, out.dtype

N_MICROBATCH = 1
ADAM_LR = 0.001
ADAM_B1 = 0.9
ADAM_B2 = 0.999
ADAM_EPS = 1e-08
ADAM_WD = 0.01
ADAM_STEP = 10
PER_EXAMPLE_BATCH_AXIS = {'x': 0, 'loss_target': 0}
SHARED_INPUTS = []
_WEIGHT_DTYPES = {'meta_tokens': _jnp.float32, 'norm_gain': _jnp.float32, 'w_in': _jnp.float32, 'pool_w': _jnp.float32, 'pool_scale': _jnp.float32, 'w_attn_up': _jnp.float32, 'w_pool_up': _jnp.float32, 'w_out': _jnp.float32, 'final_gain': _jnp.float32}
MOMENT_SCALE = {'meta_tokens': 6.866683e-04, 'norm_gain': 3.275142e-02, 'w_in': 1.465906e-02, 'pool_w': 2.427557e-02, 'pool_scale': 2.395063e-02, 'w_attn_up': 1.262738e-02, 'w_pool_up': 1.713971e-02, 'w_out': 2.128036e-02, 'final_gain': 7.995058e+00}


def _to_microbatches(a, axis):
    t = _jnp.moveaxis(a, axis, 0)
    t = t.reshape((N_MICROBATCH, t.shape[0] // N_MICROBATCH) + t.shape[1:])
    return _jnp.moveaxis(t, 1, axis + 1)


def setup_inputs(seed: int = 0) -> dict:
    inp = _fwd_setup_inputs(seed)
    key = _jax.random.fold_in(_jax.random.key(seed), 7919)
    shape, _ = _output_shape()
    out = dict(inp)
    out["loss_target"] = _jax.random.normal(_jax.random.fold_in(key, 0), shape, _jnp.float32)
    for i, name in enumerate(TWIN_WEIGHTS):
        w = inp[name].astype(_jnp.float32)
        if MOMENT_SCALE is None:
            s = _jnp.sqrt(_jnp.mean(_jnp.square(w)) + 1e-30)
        else:
            s = MOMENT_SCALE[name]
        km, kv = _jax.random.split(_jax.random.fold_in(key, i + 1))
        out[name] = w
        out["m_" + name] = s * _jax.random.normal(km, w.shape, _jnp.float32)
        out["v_" + name] = (s * s) * _jax.random.uniform(kv, w.shape, _jnp.float32, 0.5, 1.5)
    if N_MICROBATCH > 1:
        for name, axis in PER_EXAMPLE_BATCH_AXIS.items():
            out[name] = _to_microbatches(out[name], axis)
    return {'x': out['x'], 'meta_tokens': out['meta_tokens'], 'norm_gain': out['norm_gain'], 'w_in': out['w_in'], 'pool_w': out['pool_w'], 'pool_scale': out['pool_scale'], 'w_attn_up': out['w_attn_up'], 'w_pool_up': out['w_pool_up'], 'w_out': out['w_out'], 'final_gain': out['final_gain'], 'loss_target': out['loss_target'], 'm_meta_tokens': out['m_meta_tokens'], 'm_norm_gain': out['m_norm_gain'], 'm_w_in': out['m_w_in'], 'm_pool_w': out['m_pool_w'], 'm_pool_scale': out['m_pool_scale'], 'm_w_attn_up': out['m_w_attn_up'], 'm_w_pool_up': out['m_w_pool_up'], 'm_w_out': out['m_w_out'], 'm_final_gain': out['m_final_gain'], 'v_meta_tokens': out['v_meta_tokens'], 'v_norm_gain': out['v_norm_gain'], 'v_w_in': out['v_w_in'], 'v_pool_w': out['v_pool_w'], 'v_pool_scale': out['v_pool_scale'], 'v_w_attn_up': out['v_w_attn_up'], 'v_w_pool_up': out['v_w_pool_up'], 'v_w_out': out['v_w_out'], 'v_final_gain': out['v_final_gain']}


def _loss(weights, diff, rest, loss_target):
    with _jax.named_scope("forward"):
        args = {**rest, TWIN_DIFF_INPUT: diff, **{k: w.astype(_WEIGHT_DTYPES[k]) for k, w in weights.items()}}
        y = _forward(args)
    with _jax.named_scope("loss_head"):
        err = _jnp.square(y.astype(_jnp.float32) - loss_target)
        return 0.5 * _jnp.sum(_jnp.mean(err, axis=-1)) if err.ndim else 0.5 * err


def _adamw(w, g, m, v):
    m = ADAM_B1 * m + (1.0 - ADAM_B1) * g
    v = ADAM_B2 * v + (1.0 - ADAM_B2) * _jnp.square(g)
    m_hat = m / (1.0 - ADAM_B1 ** ADAM_STEP)
    v_hat = v / (1.0 - ADAM_B2 ** ADAM_STEP)
    delta = -ADAM_LR * (m_hat / (_jnp.sqrt(v_hat) + ADAM_EPS) + ADAM_WD * w)
    return delta, m, v


def reference(x, meta_tokens, norm_gain, w_in, pool_w, pool_scale, w_attn_up, w_pool_up, w_out, final_gain, loss_target, m_meta_tokens, m_norm_gain, m_w_in, m_pool_w, m_pool_scale, m_w_attn_up, m_w_pool_up, m_w_out, m_final_gain, v_meta_tokens, v_norm_gain, v_w_in, v_pool_w, v_pool_scale, v_w_attn_up, v_w_pool_up, v_w_out, v_final_gain):
    given = dict(x=x, meta_tokens=meta_tokens, norm_gain=norm_gain, w_in=w_in, pool_w=pool_w, pool_scale=pool_scale, w_attn_up=w_attn_up, w_pool_up=w_pool_up, w_out=w_out, final_gain=final_gain, loss_target=loss_target, m_meta_tokens=m_meta_tokens, m_norm_gain=m_norm_gain, m_w_in=m_w_in, m_pool_w=m_pool_w, m_pool_scale=m_pool_scale, m_w_attn_up=m_w_attn_up, m_w_pool_up=m_w_pool_up, m_w_out=m_w_out, m_final_gain=m_final_gain, v_meta_tokens=v_meta_tokens, v_norm_gain=v_norm_gain, v_w_in=v_w_in, v_pool_w=v_pool_w, v_pool_scale=v_pool_scale, v_w_attn_up=v_w_attn_up, v_w_pool_up=v_w_pool_up, v_w_out=v_w_out, v_final_gain=v_final_gain)
    weights = {n: given[n] for n in TWIN_WEIGHTS}
    shared = {n: given[n] for n in SHARED_INPUTS}
    per_example = {n: given[n] for n in ['x']}
    grad_fn = _jax.value_and_grad(_loss, argnums=(0, 1))

    def one_microbatch(ex, loss_target):
        ex = dict(ex)
        diff = ex.pop(TWIN_DIFF_INPUT)
        return grad_fn(weights, diff, {**shared, **ex}, loss_target)

    if N_MICROBATCH == 1:
        loss, (grad_w, grad_x) = one_microbatch(per_example, given["loss_target"])
    else:
        def body(carry, xs):
            loss_sum, grad_sum = carry
            l_k, (gw_k, gx_k) = one_microbatch(xs[0], xs[1])
            with _jax.named_scope("update"):
                return (loss_sum + l_k, _jax.tree.map(_jnp.add, grad_sum, gw_k)), gx_k

        init = (_jnp.zeros((), _jnp.float32), _jax.tree.map(_jnp.zeros_like, weights))
        (loss, grad_w), grad_x = _jax.lax.scan(body, init, (per_example, given["loss_target"]))
    with _jax.named_scope("update"):
        delta_w, new_m, new_v = {}, {}, {}
        for n in TWIN_WEIGHTS:
            delta_w[n], new_m[n], new_v[n] = _adamw(weights[n], grad_w[n], given["m_" + n], given["v_" + n])
    return (loss, grad_x, *[grad_w[n] for n in TWIN_WEIGHTS], *[delta_w[n] for n in TWIN_WEIGHTS],
            *[new_m[n] for n in TWIN_WEIGHTS], *[new_v[n] for n in TWIN_WEIGHTS])
```

```python
import functools

import jax
import jax.numpy as jnp
from jax import lax
from jax.experimental import pallas as pl
from jax.experimental.pallas import tpu as pltpu

F32 = jnp.float32
BF16 = jnp.bfloat16
MESH = pl.DeviceIdType.MESH

N_DEV = 8
HEAD_DIM = 128
ATT_BLOCK = 128
POOL_WINDOWS = (2, 4, 8, 16)
RMS_EPS = 1e-6
ADAM_LR, ADAM_B1, ADAM_B2, ADAM_EPS, ADAM_WD, ADAM_STEP = 0.001, 0.9, 0.999, 1e-08, 0.01, 10
VMEM_LIMIT_BYTES = 56 * 1024 * 1024


def _tile(n, target, mult):
    if n <= target:
        return n
    best = 0
    for t in range(mult, target + 1, mult):
        if n % t == 0:
            best = t
    assert best > 0, (n, target, mult)
    return best


def _params(*sem):
    return pltpu.CompilerParams(dimension_semantics=sem, vmem_limit_bytes=VMEM_LIMIT_BYTES)


def _sigmoid(x):
    return 1.0 / (1.0 + jnp.exp(-x))


def _dot(a, b, ta=False, tb=False):
    dims = (((0 if ta else 1,), (1 if tb else 0,)), ((), ()))
    return lax.dot_general(a.astype(BF16), b.astype(BF16), dims, preferred_element_type=F32)


def _dot_split(a, b):
    hi = a.astype(BF16)
    lo = (a - hi.astype(F32)).astype(BF16)
    return _dot(hi, b) + _dot(lo, b)


def _cast_bf16(x, name):
    a, r, c = x.shape
    tr = _tile(r, max(16, (512 * 1024) // c // 16 * 16), 16)

    def body(x_ref, o_ref):
        o_ref[...] = x_ref[...].astype(BF16)

    return pl.pallas_call(
        body, name=name, grid=(a, r // tr),
        in_specs=[pl.BlockSpec((None, tr, c), lambda i, j: (i, j, 0))],
        out_specs=pl.BlockSpec((None, tr, c), lambda i, j: (i, j, 0)),
        out_shape=jax.ShapeDtypeStruct(x.shape, BF16),
        compiler_params=_params("parallel", "parallel"),
    )(x)


def _matmul(a, b, *, ta=False, tb=False, out_dtype=F32, add=None, name, tm=1088, tn=1024, tk=2048):
    m, k = (a.shape[1], a.shape[0]) if ta else a.shape
    n = b.shape[0] if tb else b.shape[1]
    tm = _tile(m, tm, 128 if ta else 16)
    tn = _tile(n, tn, 128)
    tk = _tile(k, tk, 128 if (not ta or tb) else 16)
    nk = k // tk
    a_spec = pl.BlockSpec((tk, tm), lambda i, j, kk: (kk, i)) if ta else pl.BlockSpec((tm, tk), lambda i, j, kk: (i, kk))
    b_spec = pl.BlockSpec((tn, tk), lambda i, j, kk: (j, kk)) if tb else pl.BlockSpec((tk, tn), lambda i, j, kk: (kk, j))
    o_spec = pl.BlockSpec((tm, tn), lambda i, j, kk: (i, j))
    has_add = add is not None

    def body(*refs):
        a_ref, b_ref = refs[0], refs[1]
        add_ref = refs[2] if has_add else None
        o_ref = refs[3] if has_add else refs[2]
        acc_ref = refs[-1]
        kk = pl.program_id(2)
        p = _dot(a_ref[...], b_ref[...], ta, tb)

        def finish(total):
            if has_add:
                total = total + add_ref[...]
            o_ref[...] = total.astype(out_dtype)

        if nk == 1:
            finish(p)
        else:
            @pl.when(kk == 0)
            def _():
                acc_ref[...] = p

            @pl.when(jnp.logical_and(kk > 0, kk < nk - 1))
            def _():
                acc_ref[...] += p

            @pl.when(kk == nk - 1)
            def _():
                finish(acc_ref[...] + p)

    operands = [a, b] + ([add] if has_add else [])
    in_specs = [a_spec, b_spec] + ([o_spec] if has_add else [])
    return pl.pallas_call(
        body, name=name, grid=(m // tm, n // tn, nk),
        in_specs=in_specs, out_specs=o_spec,
        out_shape=jax.ShapeDtypeStruct((m, n), out_dtype),
        scratch_shapes=[pltpu.VMEM((tm, tn) if nk > 1 else (8, 128), F32)],
        compiler_params=_params("parallel", "parallel", "arbitrary"),
    )(*operands)


def _rms_fwd(x, gain, name):
    l, d = x.shape
    tr = _tile(l, 272, 16)

    def body(x_ref, g_ref, o_ref):
        xv = x_ref[...]
        r = lax.rsqrt(jnp.mean(xv * xv, axis=-1, keepdims=True) + RMS_EPS)
        o_ref[...] = (xv * r * g_ref[...]).astype(BF16)

    return pl.pallas_call(
        body, name=name, grid=(l // tr,),
        in_specs=[pl.BlockSpec((tr, d), lambda i: (i, 0)), pl.BlockSpec((1, d), lambda i: (0, 0))],
        out_specs=pl.BlockSpec((tr, d), lambda i: (i, 0)),
        out_shape=jax.ShapeDtypeStruct((l, d), BF16),
        compiler_params=_params("parallel"),
    )(x, gain)


def _rms_bwd(dh, x, gain, gout, name):
    l, d = x.shape
    tr = _tile(l, 272, 16)

    def body(dh_ref, x_ref, g_ref, go_ref, gx_ref, gxb_ref, dg_ref):
        xv = x_ref[...]
        r = lax.rsqrt(jnp.mean(xv * xv, axis=-1, keepdims=True) + RMS_EPS)
        nv = xv * r
        dhv = dh_ref[...]
        dn = dhv * g_ref[...]
        dx = r * (dn - nv * jnp.mean(dn * nv, axis=-1, keepdims=True))
        gx = go_ref[...] + dx
        gx_ref[...] = gx
        gxb_ref[...] = gx.astype(BF16)
        part = jnp.sum(dhv * nv, axis=0, keepdims=True)

        @pl.when(pl.program_id(0) == 0)
        def _():
            dg_ref[...] = part

        @pl.when(pl.program_id(0) > 0)
        def _():
            dg_ref[...] += part

    row = pl.BlockSpec((tr, d), lambda i: (i, 0))
    vec = pl.BlockSpec((1, d), lambda i: (0, 0))
    return pl.pallas_call(
        body, name=name, grid=(l // tr,),
        in_specs=[row, row, vec, row], out_specs=[row, row, vec],
        out_shape=[jax.ShapeDtypeStruct((l, d), F32), jax.ShapeDtypeStruct((l, d), BF16),
                   jax.ShapeDtypeStruct((1, d), F32)],
        compiler_params=_params("arbitrary"),
    )(dh, x, gain, gout)


def _final_loss(x, gain, target, row_lo, row_hi, name):
    l, d = x.shape
    tr = _tile(l, 272, 16)

    def body(x_ref, g_ref, t_ref, gx_ref, gxb_ref, dg_ref, loss_ref):
        i = pl.program_id(0)
        xv = x_ref[...]
        r = lax.rsqrt(jnp.mean(xv * xv, axis=-1, keepdims=True) + RMS_EPS)
        nv = xv * r
        gv = g_ref[...]
        rows = i * tr + lax.broadcasted_iota(jnp.int32, (tr, 1), 0)
        valid = jnp.logical_and(rows >= row_lo, rows < row_hi)
        err = jnp.where(valid, nv * gv - t_ref[...], 0.0)
        dy = err * (1.0 / d)
        dn = dy * gv
        gx = r * (dn - nv * jnp.mean(dn * nv, axis=-1, keepdims=True))
        gx_ref[...] = gx
        gxb_ref[...] = gx.astype(BF16)
        part = jnp.sum(dy * nv, axis=0, keepdims=True)
        lpart = jnp.full((1, 128), 0.5 * jnp.sum(jnp.mean(err * err, axis=-1, keepdims=True)), F32)

        @pl.when(i == 0)
        def _():
            dg_ref[...] = part
            loss_ref[...] = lpart

        @pl.when(i > 0)
        def _():
            dg_ref[...] += part
            loss_ref[...] += lpart

    row = pl.BlockSpec((tr, d), lambda i: (i, 0))
    vec = pl.BlockSpec((1, d), lambda i: (0, 0))
    return pl.pallas_call(
        body, name=name, grid=(l // tr,),
        in_specs=[row, vec, row], out_specs=[row, row, vec, pl.BlockSpec((1, 128), lambda i: (0, 0))],
        out_shape=[jax.ShapeDtypeStruct((l, d), F32), jax.ShapeDtypeStruct((l, d), BF16),
                   jax.ShapeDtypeStruct((1, d), F32), jax.ShapeDtypeStruct((1, 128), F32)],
        compiler_params=_params("arbitrary"),
    )(x, gain, target)


def _attn_tile(q, kb, q0, k0, scale):
    z = _dot(q, kb, tb=True) * scale
    lb = -(jnp.maximum(z, 0.0) + jnp.log1p(jnp.exp(-jnp.abs(z))))
    rows = q0 + lax.broadcasted_iota(jnp.int32, z.shape, 0)
    cols = k0 + lax.broadcasted_iota(jnp.int32, z.shape, 1)
    causal = cols < rows
    return z, lb, jnp.where(causal, lb, 0.0), causal


def _tri(n, relation):
    r = lax.broadcasted_iota(jnp.int32, (n, n), 0)
    c = lax.broadcasted_iota(jnp.int32, (n, n), 1)
    return jnp.where(relation(r, c), 1.0, 0.0).astype(BF16)


def _attn_fwd(zin, n_heads, name):
    l = zin.shape[0]
    t = ATT_BLOCK
    nq = l // t
    scale = HEAD_DIM ** -0.5

    def body(q_ref, k_ref, v_ref, g_ref, o_ref, oa_ref, tot_ref):
        i = pl.program_id(1)
        q = q_ref[...]
        after = _tri(t, lambda r, c: r > c)

        def step(it, carry):
            run, acc = carry
            j = i - it
            k0 = pl.multiple_of(j * t, t)
            z, lb_all, lb, causal = _attn_tile(q, k_ref[pl.ds(k0, t), :], i * t, k0, scale)
            between = _dot_split(lb, after) + run
            a = jnp.where(causal, jnp.exp(z + lb_all + between), 0.0)
            acc = acc + _dot(a, v_ref[pl.ds(k0, t), :])
            return run + jnp.sum(lb, axis=1, keepdims=True), acc

        total, o = lax.fori_loop(0, i + 1, step, (jnp.zeros((t, 1), F32), jnp.zeros((t, HEAD_DIM), F32)))
        gate = g_ref[...]
        o_ref[...] = o
        tot_ref[...] = jnp.broadcast_to(total, (t, HEAD_DIM))
        oa_ref[...] = (o * (gate * _sigmoid(gate))).astype(BF16)

    def col(off):
        return lambda h, i: (i, off + h)

    def slab(off):
        return lambda h, i: (0, off + h)

    blk = (t, HEAD_DIM)
    return pl.pallas_call(
        body, name=name, grid=(n_heads, nq),
        in_specs=[pl.BlockSpec(blk, col(0)), pl.BlockSpec((l, HEAD_DIM), slab(n_heads)),
                  pl.BlockSpec((l, HEAD_DIM), slab(2 * n_heads)), pl.BlockSpec(blk, col(3 * n_heads))],
        out_specs=[pl.BlockSpec(blk, col(0)), pl.BlockSpec(blk, col(0)), pl.BlockSpec(blk, col(0))],
        out_shape=[jax.ShapeDtypeStruct((l, n_heads * HEAD_DIM), F32),
                   jax.ShapeDtypeStruct((l, n_heads * HEAD_DIM), BF16),
                   jax.ShapeDtypeStruct((l, n_heads * HEAD_DIM), F32)],
        compiler_params=_params("parallel", "arbitrary"),
    )(zin, zin, zin, zin)


def _attn_bwd(zin, o, tot, doa, n_heads, name):
    l = zin.shape[0]
    t = ATT_BLOCK
    nq = l // t
    scale = HEAD_DIM ** -0.5

    def body(q_ref, k_ref, v_ref, g_ref, o_ref, tot_ref, doa_ref, dq_ref, dk_ref, dv_ref, dg_ref, dk_acc, dv_acc):
        i = pl.program_id(1)

        @pl.when(i == 0)
        def _():
            dk_acc[...] = jnp.zeros_like(dk_acc)
            dv_acc[...] = jnp.zeros_like(dv_acc)

        q = q_ref[...]
        gate = g_ref[...]
        sg = _sigmoid(gate)
        ov = o_ref[...]
        doav = doa_ref[...]
        dg_ref[...] = (doav * ov * (sg * (1.0 + gate * (1.0 - sg)))).astype(BF16)
        do = doav * (gate * sg)
        total = tot_ref[:, 0:1]
        upto = _tri(t, lambda r, c: r <= c)
        before = _tri(t, lambda r, c: r < c)

        def step(j, carry):
            run, pre, dq = carry
            k0 = pl.multiple_of(j * t, t)
            kb = k_ref[pl.ds(k0, t), :]
            vb = v_ref[pl.ds(k0, t), :]
            z, lb_all, lb, causal = _attn_tile(q, kb, i * t, k0, scale)
            between = total - run - _dot_split(lb, upto)
            beta = jnp.exp(z + lb_all)
            a = jnp.where(causal, beta * jnp.exp(between), 0.0)
            e = a * _dot(do, vb, tb=True)
            prefix = pre + _dot_split(e, before)
            dz = jnp.where(causal, e * (1.0 - beta) - beta * prefix, 0.0) * scale
            dv_acc[pl.ds(k0, t), :] += _dot(a, do, ta=True)
            dk_acc[pl.ds(k0, t), :] += _dot(dz, q, ta=True)
            dq = dq + _dot(dz, kb)
            return run + jnp.sum(lb, axis=1, keepdims=True), pre + jnp.sum(e, axis=1, keepdims=True), dq

        zero = jnp.zeros((t, 1), F32)
        _, _, dq = lax.fori_loop(0, i + 1, step, (zero, zero, jnp.zeros((t, HEAD_DIM), F32)))
        dq_ref[...] = dq.astype(BF16)

        @pl.when(i == nq - 1)
        def _():
            dk_ref[...] = dk_acc[...].astype(BF16)
            dv_ref[...] = dv_acc[...].astype(BF16)

    def col(off):
        return lambda h, i: (i, off + h)

    def slab(off):
        return lambda h, i: (0, off + h)

    blk = (t, HEAD_DIM)
    full = (l, HEAD_DIM)
    out = jax.ShapeDtypeStruct((l, n_heads * HEAD_DIM), BF16)
    return pl.pallas_call(
        body, name=name, grid=(n_heads, nq),
        in_specs=[pl.BlockSpec(blk, col(0)), pl.BlockSpec(full, slab(n_heads)),
                  pl.BlockSpec(full, slab(2 * n_heads)), pl.BlockSpec(blk, col(3 * n_heads)),
                  pl.BlockSpec(blk, col(0)), pl.BlockSpec(blk, col(0)), pl.BlockSpec(blk, col(0))],
        out_specs=[pl.BlockSpec(blk, col(0)), pl.BlockSpec(full, slab(0)), pl.BlockSpec(full, slab(0)),
                   pl.BlockSpec(blk, col(0))],
        out_shape=[out, out, out, out],
        scratch_shapes=[pltpu.VMEM(full, F32), pltpu.VMEM(full, F32)],
        compiler_params=_params("parallel", "arbitrary"),
    )(zin, zin, zin, zin, o, tot, doa)


def _shift_rows(x, k, down):
    n = x.shape[0]
    rows = lax.broadcasted_iota(jnp.int32, x.shape, 0)
    if down:
        return jnp.where(rows >= k, pltpu.roll(x, k, 0), 0.0)
    return jnp.where(rows < n - k, pltpu.roll(x, n - k, 0), 0.0)


def _window_sum(x, g, down):
    result = x
    total = x
    for step, k in enumerate((1, 2, 4, 8)):
        total = total + _shift_rows(total, k, down)
        result = jnp.where(g >= step, total, result)
    return result


def _pooled(u, g):
    rows = lax.broadcasted_iota(jnp.int32, (u.shape[0], 1), 0)
    window = jnp.left_shift(2, g)
    cnt = jnp.minimum(rows + 1, window).astype(F32)
    return _window_sum(u, g, True) / cnt - u, cnt


def _pool_fwd(zin, pool_w, pool_scale, u_off, name):
    l = zin.shape[0]
    n_groups, gd, _ = pool_w.shape

    def body(u_ref, g_ref, w_ref, s_ref, o_ref):
        g = pl.program_id(0)
        pooled, _ = _pooled(u_ref[...], g)
        mixed = _dot(pooled, w_ref[...])
        gate = g_ref[...]
        o_ref[...] = (mixed * s_ref[...] * (gate * _sigmoid(gate))).astype(BF16)

    return pl.pallas_call(
        body, name=name, grid=(n_groups,),
        in_specs=[pl.BlockSpec((l, gd), lambda g: (0, u_off + g)),
                  pl.BlockSpec((l, gd), lambda g: (0, u_off + n_groups + g)),
                  pl.BlockSpec((None, gd, gd), lambda g: (g, 0, 0)),
                  pl.BlockSpec((1, gd), lambda g: (0, g))],
        out_specs=pl.BlockSpec((l, gd), lambda g: (0, g)),
        out_shape=jax.ShapeDtypeStruct((l, n_groups * gd), BF16),
        compiler_params=_params("parallel"),
    )(zin, zin, pool_w, pool_scale)


def _pool_bwd(zin, dop, pool_w, pool_scale, u_off, name):
    l = zin.shape[0]
    n_groups, gd, _ = pool_w.shape

    def body(u_ref, g_ref, w_ref, s_ref, d_ref, du_ref, dg_ref, dw_ref, ds_ref):
        g = pl.program_id(0)
        pooled, cnt = _pooled(u_ref[...], g)
        w = w_ref[...]
        mixed = _dot(pooled, w)
        gate = g_ref[...]
        sg = _sigmoid(gate)
        silu = gate * sg
        dop_v = d_ref[...]
        sc = s_ref[...]
        ds_ref[...] = jnp.sum(dop_v * mixed * silu, axis=0, keepdims=True)
        dg_ref[...] = (dop_v * mixed * sc * (sg * (1.0 + gate * (1.0 - sg)))).astype(BF16)
        dmixed = dop_v * sc * silu
        dw_ref[...] = _dot(pooled, dmixed, ta=True).astype(BF16)
        dpooled = _dot(dmixed, w, tb=True)
        du_ref[...] = (_window_sum(dpooled / cnt, g, False) - dpooled).astype(BF16)

    slab = pl.BlockSpec((l, gd), lambda g: (0, g))
    return pl.pallas_call(
        body, name=name, grid=(n_groups,),
        in_specs=[pl.BlockSpec((l, gd), lambda g: (0, u_off + g)),
                  pl.BlockSpec((l, gd), lambda g: (0, u_off + n_groups + g)),
                  pl.BlockSpec((None, gd, gd), lambda g: (g, 0, 0)),
                  pl.BlockSpec((1, gd), lambda g: (0, g)), slab],
        out_specs=[slab, slab, pl.BlockSpec((None, gd, gd), lambda g: (g, 0, 0)),
                   pl.BlockSpec((1, gd), lambda g: (0, g))],
        out_shape=[jax.ShapeDtypeStruct((l, n_groups * gd), BF16), jax.ShapeDtypeStruct((l, n_groups * gd), BF16),
                   jax.ShapeDtypeStruct(pool_w.shape, BF16), jax.ShapeDtypeStruct((1, n_groups * gd), F32)],
        compiler_params=_params("parallel"),
    )(zin, zin, pool_w, pool_scale, dop)


def _merge_fwd(oa, op, w_au, w_pu, zin, name):
    l, wa = oa.shape
    wp = op.shape[1]
    d = w_au.shape[1]
    tm = _tile(l, 544, 16)
    tn = _tile(d, 512, 128)
    ma_off = (zin.shape[1] - 2 * d) // tn

    def body(oa_ref, op_ref, wa_ref, wp_ref, ma_ref, mp_ref, ya_ref, yp_ref, mg_ref):
        ya = _dot(oa_ref[...], wa_ref[...])
        yp = _dot(op_ref[...], wp_ref[...])
        ya_ref[...] = ya
        yp_ref[...] = yp
        mg_ref[...] = (_sigmoid(ma_ref[...]) * ya + _sigmoid(mp_ref[...]) * yp).astype(BF16)

    tile = pl.BlockSpec((tm, tn), lambda i, j: (i, j))
    return pl.pallas_call(
        body, name=name, grid=(l // tm, d // tn),
        in_specs=[pl.BlockSpec((tm, wa), lambda i, j: (i, 0)), pl.BlockSpec((tm, wp), lambda i, j: (i, 0)),
                  pl.BlockSpec((wa, tn), lambda i, j: (0, j)), pl.BlockSpec((wp, tn), lambda i, j: (0, j)),
                  pl.BlockSpec((tm, tn), lambda i, j: (i, ma_off + j)),
                  pl.BlockSpec((tm, tn), lambda i, j: (i, ma_off + d // tn + j))],
        out_specs=[tile, tile, tile],
        out_shape=[jax.ShapeDtypeStruct((l, d), F32), jax.ShapeDtypeStruct((l, d), F32),
                   jax.ShapeDtypeStruct((l, d), BF16)],
        compiler_params=_params("parallel", "parallel"),
    )(oa, op, w_au, w_pu, zin, zin)


def _merge_bwd(gout, w_out, zin, ya, yp, name):
    l, d = gout.shape
    tm = _tile(l, 544, 16)
    tn = _tile(d, 512, 128)
    ma_off = (zin.shape[1] - 2 * d) // tn

    def body(g_ref, w_ref, ma_ref, mp_ref, ya_ref, yp_ref, dya_ref, dyp_ref, dma_ref, dmp_ref):
        dm = _dot(g_ref[...], w_ref[...], tb=True)
        sa = _sigmoid(ma_ref[...])
        sp = _sigmoid(mp_ref[...])
        dya_ref[...] = (dm * sa).astype(BF16)
        dyp_ref[...] = (dm * sp).astype(BF16)
        dma_ref[...] = (dm * ya_ref[...] * (sa * (1.0 - sa))).astype(BF16)
        dmp_ref[...] = (dm * yp_ref[...] * (sp * (1.0 - sp))).astype(BF16)

    tile = pl.BlockSpec((tm, tn), lambda i, j: (i, j))
    out = jax.ShapeDtypeStruct((l, d), BF16)
    return pl.pallas_call(
        body, name=name, grid=(l // tm, d // tn),
        in_specs=[pl.BlockSpec((tm, d), lambda i, j: (i, 0)), pl.BlockSpec((tn, d), lambda i, j: (j, 0)),
                  pl.BlockSpec((tm, tn), lambda i, j: (i, ma_off + j)),
                  pl.BlockSpec((tm, tn), lambda i, j: (i, ma_off + d // tn + j)), tile, tile],
        out_specs=[tile, tile, tile, tile],
        out_shape=[out, out, out, out],
        compiler_params=_params("parallel", "parallel"),
    )(gout, w_out, zin, zin, ya, yp)


def _adamw(parts, w, m, v, name):
    n_parts, r, c = parts.shape
    tr = _tile(r, max(8, (128 * 1024) // c // 8 * 8), 8)
    bias1 = 1.0 - ADAM_B1 ** ADAM_STEP
    bias2 = 1.0 - ADAM_B2 ** ADAM_STEP

    def body(p_ref, w_ref, m_ref, v_ref, g_ref, d_ref, nm_ref, nv_ref):
        g = p_ref[0].astype(F32)
        for j in range(1, n_parts):
            g = g + p_ref[j].astype(F32)
        nm = ADAM_B1 * m_ref[...] + (1.0 - ADAM_B1) * g
        nv = ADAM_B2 * v_ref[...] + (1.0 - ADAM_B2) * (g * g)
        g_ref[...] = g
        nm_ref[...] = nm
        nv_ref[...] = nv
        d_ref[...] = -ADAM_LR * ((nm / bias1) / (jnp.sqrt(nv / bias2) + ADAM_EPS) + ADAM_WD * w_ref[...])

    tile = pl.BlockSpec((tr, c), lambda i: (i, 0))
    out = jax.ShapeDtypeStruct((r, c), F32)
    return pl.pallas_call(
        body, name=name, grid=(r // tr,),
        in_specs=[pl.BlockSpec((n_parts, tr, c), lambda i: (0, i, 0)), tile, tile, tile],
        out_specs=[tile, tile, tile, tile], out_shape=[out, out, out, out],
        compiler_params=_params("parallel"),
    )(parts, w, m, v)


def _position():
    return lax.axis_index("x"), lax.axis_index("y"), lax.axis_index("c")


def _block_of(ref, axis, size, index):
    idx = [slice(None)] * len(ref.shape)
    idx[axis] = pl.ds(index * size, size)
    return ref.at[tuple(idx)]


def _allgather_blocks(shards, axes, name):
    n = len(shards)
    full_shapes = []
    for s, ax in zip(shards, axes):
        shape = list(s.shape)
        shape[ax] *= N_DEV
        full_shapes.append(jax.ShapeDtypeStruct(tuple(shape), s.dtype))

    def body(*refs):
        ins, outs = refs[:n], refs[n:2 * n]
        send_sems, recv_sems, local_sems = refs[2 * n:]
        x, y, c = _position()
        me, sibling = (x, y, c), (x, y, 1 - c)
        chips = [(1 - x, y), (x, 1 - y), (1 - x, 1 - y)]

        def slot(a, dev):
            return _block_of(outs[a], axes[a], shards[a].shape[axes[a]], 4 * dev[0] + 2 * dev[1] + dev[2])

        def copy(a, k, block, to, src=None):
            return pltpu.make_async_remote_copy(
                src_ref=slot(a, block) if src is None else src, dst_ref=slot(a, block),
                send_sem=send_sems.at[a * 7 + k], recv_sem=recv_sems.at[a * 7 + k],
                device_id=to, device_id_type=MESH)

        mine = [pltpu.make_async_copy(ins[a], slot(a, me), local_sems.at[a]) for a in range(n)]
        for cp in mine:
            cp.start()
        first = []
        for a in range(n):
            first.append(copy(a, 0, me, sibling, src=ins[a]))
            first += [copy(a, 1 + j, me, (*chip, c), src=ins[a]) for j, chip in enumerate(chips)]
        for cp in first:
            cp.start()
        passed = []
        for j, chip in enumerate(chips):
            for a in range(n):
                copy(a, 1 + j, (*chip, c), me).wait_recv()
                fwd = copy(a, 4 + j, (*chip, c), sibling)
                fwd.start()
                passed.append(fwd)
        for a in range(n):
            copy(a, 0, sibling, me).wait_recv()
            for j, chip in enumerate(chips):
                copy(a, 4 + j, (*chip, 1 - c), me).wait_recv()
        for cp in first + passed:
            cp.wait_send()
        for cp in mine:
            cp.wait()

    hbm = pl.BlockSpec(memory_space=pl.ANY)
    return pl.pallas_call(
        body, name=name, in_specs=[hbm] * n, out_specs=[hbm] * n, out_shape=full_shapes,
        scratch_shapes=[pltpu.SemaphoreType.DMA((7 * n,)), pltpu.SemaphoreType.DMA((7 * n,)),
                        pltpu.SemaphoreType.DMA((n,))],
    )(*shards)


def _exchange_grads(fulls, axes, name):
    n = len(fulls)
    out_shapes, sizes = [], []
    for f, ax in zip(fulls, axes):
        shape = list(f.shape)
        shape[ax] //= N_DEV
        sizes.append(shape[ax])
        out_shapes.append(jax.ShapeDtypeStruct((N_DEV, *shape), f.dtype))
    flips = [(a, b, d) for a in (0, 1) for b in (0, 1) for d in (0, 1)][1:]

    def body(*refs):
        ins, outs = refs[:n], refs[n:2 * n]
        send_sems, recv_sems, local_sems = refs[2 * n:]
        x, y, c = _position()
        my_index = 4 * x + 2 * y + c

        def peer_of(flip):
            return (x ^ flip[0], y ^ flip[1], c ^ flip[2])

        def copy(a, k, flip):
            px, py, pc = peer_of(flip)
            return pltpu.make_async_remote_copy(
                src_ref=_block_of(ins[a], axes[a], sizes[a], 4 * px + 2 * py + pc),
                dst_ref=outs[a].at[my_index],
                send_sem=send_sems.at[a * 7 + k], recv_sem=recv_sems.at[a * 7 + k],
                device_id=(px, py, pc), device_id_type=MESH)

        mine = [pltpu.make_async_copy(_block_of(ins[a], axes[a], sizes[a], my_index), outs[a].at[my_index],
                                      local_sems.at[a]) for a in range(n)]
        for cp in mine:
            cp.start()
        sends = [copy(a, k, flip) for k, flip in enumerate(flips) for a in range(n)]
        for cp in sends:
            cp.start()
        for k, flip in enumerate(flips):
            px, py, pc = peer_of(flip)
            for a in range(n):
                pltpu.make_async_remote_copy(
                    src_ref=outs[a].at[my_index], dst_ref=outs[a].at[4 * px + 2 * py + pc],
                    send_sem=send_sems.at[a * 7 + k], recv_sem=recv_sems.at[a * 7 + k],
                    device_id=(px, py, pc), device_id_type=MESH).wait_recv()
        for cp in sends:
            cp.wait_send()
        for cp in mine:
            cp.wait()

    hbm = pl.BlockSpec(memory_space=pl.ANY)
    return pl.pallas_call(
        body, name=name, in_specs=[hbm] * n, out_specs=[hbm] * n, out_shape=out_shapes,
        scratch_shapes=[pltpu.SemaphoreType.DMA((7 * n,)), pltpu.SemaphoreType.DMA((7 * n,)),
                        pltpu.SemaphoreType.DMA((n,))],
    )(*fulls)


def _allgather_small(v, name):
    r, c = v.shape
    flips = [(a, b, d) for a in (0, 1) for b in (0, 1) for d in (0, 1)][1:]

    def body(v_ref, out_ref, send_sems, recv_sems):
        x, y, c_ = _position()
        my_index = 4 * x + 2 * y + c_
        out_ref[my_index] = v_ref[...]
        sends = []
        for k, flip in enumerate(flips):
            peer = (x ^ flip[0], y ^ flip[1], c_ ^ flip[2])
            cp = pltpu.make_async_remote_copy(
                src_ref=v_ref, dst_ref=out_ref.at[my_index],
                send_sem=send_sems.at[k], recv_sem=recv_sems.at[k], device_id=peer, device_id_type=MESH)
            cp.start()
            sends.append(cp)
        for k, flip in enumerate(flips):
            px, py, pc = x ^ flip[0], y ^ flip[1], c_ ^ flip[2]
            pltpu.make_async_remote_copy(
                src_ref=v_ref, dst_ref=out_ref.at[4 * px + 2 * py + pc],
                send_sem=send_sems.at[k], recv_sem=recv_sems.at[k],
                device_id=(px, py, pc), device_id_type=MESH).wait_recv()
        for cp in sends:
            cp.wait_send()

    return pl.pallas_call(
        body, name=name,
        in_specs=[pl.BlockSpec(memory_space=pltpu.VMEM)], out_specs=pl.BlockSpec(memory_space=pltpu.VMEM),
        out_shape=jax.ShapeDtypeStruct((N_DEV, r, c), v.dtype),
        scratch_shapes=[pltpu.SemaphoreType.DMA((7,)), pltpu.SemaphoreType.DMA((7,))],
    )(v)


def kernel(x, meta_tokens, norm_gain, w_in, pool_w, pool_scale, w_attn_up, w_pool_up, w_out, final_gain, loss_target, m_meta_tokens, m_norm_gain, m_w_in, m_pool_w, m_pool_scale, m_w_attn_up, m_w_pool_up, m_w_out, m_final_gain, v_meta_tokens, v_norm_gain, v_w_in, v_pool_w, v_pool_scale, v_w_attn_up, v_w_pool_up, v_w_out, v_final_gain):
    _, seq, d = x.shape
    n_meta = meta_tokens.shape[0]
    depth = w_in.shape[0]
    sb_width = w_attn_up.shape[1]
    pool_width = w_pool_up.shape[1]
    n_heads = sb_width // HEAD_DIM
    n_groups = pool_w.shape[1]
    gd = pool_w.shape[3]
    assert n_groups == len(POOL_WINDOWS) and gd * n_groups == pool_width
    assert w_in.shape[2] * N_DEV == 4 * sb_width + 2 * pool_width + 2 * d
    l_real = n_meta + seq
    l_pad = -(-l_real // ATT_BLOCK) * ATT_BLOCK
    my_index = 4 * lax.axis_index("x") + 2 * lax.axis_index("y") + lax.axis_index("c")

    w_in_b = _cast_bf16(w_in, "cast_w_in")
    pool_w_b = _cast_bf16(pool_w.reshape(depth, n_groups * pool_w.shape[2], gd), "cast_pool_w").reshape(pool_w.shape)
    w_au_b = _cast_bf16(w_attn_up, "cast_w_attn_up")
    w_pu_b = _cast_bf16(w_pool_up, "cast_w_pool_up")
    w_out_b = _cast_bf16(w_out, "cast_w_out")
    weights = [_allgather_blocks([w_in_b[i], pool_w_b[i], w_au_b[i], w_pu_b[i], w_out_b[i]], [1, 1, 1, 1, 0],
                                 "allgather_weights") for i in range(depth)]
    meta_all = _allgather_small(meta_tokens, "allgather_meta")
    meta_full = jnp.transpose(meta_all, (1, 0, 2)).reshape(n_meta, d)

    pad_rows = l_pad - l_real
    hs = jnp.concatenate([meta_full, x[0], jnp.zeros((pad_rows, d), F32)], axis=0)
    target = jnp.concatenate([jnp.zeros((n_meta, d), F32), loss_target[0], jnp.zeros((pad_rows, d), F32)], axis=0)
    u_off = 4 * sb_width // gd
    saved = []
    for i in range(depth):
        wi, pw, wau, wpu, wo = weights[i]
        h = _rms_fwd(hs, norm_gain[i][None], "rms_fwd")
        zin = _matmul(h, wi, name="mm_zin")
        o, oa, tot = _attn_fwd(zin, n_heads, "attn_fwd")
        op = _pool_fwd(zin, pw, pool_scale[i][None], u_off, "pool_fwd")
        ya, yp, merged = _merge_fwd(oa, op, wau, wpu, zin, "merge_fwd")
        saved.append((hs, h, zin, o, tot, oa, op, ya, yp, merged))
        hs = _matmul(merged, wo, add=hs, name="mm_out")
    g, gb, d_final_gain, loss_part = _final_loss(hs, final_gain[None], target, n_meta, l_real, "final_loss")
    loss = lax.psum(loss_part[0, 0], ("x", "y", "c"))

    d_norm_gain = [None] * depth
    d_pool_scale = [None] * depth
    recv = [None] * depth
    for i in reversed(range(depth)):
        wi, pw, wau, wpu, wo = weights[i]
        hs_in, h, zin, o, tot, oa, op, ya, yp, merged = saved[i]
        dya, dyp, dma, dmp = _merge_bwd(gb, wo, zin, ya, yp, "merge_bwd")
        dw_out = _matmul(merged, gb, ta=True, out_dtype=BF16, name="mm_dw_out", tm=1024, tk=1088)
        doa = _matmul(dya, wau, tb=True, name="mm_doa")
        dw_au = _matmul(oa, dya, ta=True, out_dtype=BF16, name="mm_dw_au", tm=1024, tk=1088)
        dop = _matmul(dyp, wpu, tb=True, name="mm_dop")
        dw_pu = _matmul(op, dyp, ta=True, out_dtype=BF16, name="mm_dw_pu", tm=1024, tk=1088)
        dq, dk, dv, dga = _attn_bwd(zin, o, tot, doa, n_heads, "attn_bwd")
        du, dgp, dpw, dps = _pool_bwd(zin, dop, pw, pool_scale[i][None], u_off, "pool_bwd")
        dzin = jnp.concatenate([dq, dk, dv, dga, du, dgp, dma, dmp], axis=1)
        dw_in = _matmul(h, dzin, ta=True, out_dtype=BF16, name="mm_dw_in", tm=1024, tk=1088)
        dh = _matmul(dzin, wi, tb=True, name="mm_dh")
        g, gb, dng = _rms_bwd(dh, hs_in, norm_gain[i][None], g, "rms_bwd")
        d_norm_gain[i] = dng
        d_pool_scale[i] = dps
        recv[i] = _exchange_grads([dw_in, dpw, dw_au, dw_pu, dw_out], [1, 1, 1, 1, 0], "exchange_grads")
    grad_x = g[n_meta:l_real][None]

    zeros_ps = jnp.zeros((depth, d - pool_width), F32)
    small_rows = [jnp.concatenate(d_norm_gain, axis=0),
                  jnp.concatenate([jnp.concatenate(d_pool_scale, axis=0), zeros_ps], axis=1), d_final_gain]
    n_small = 2 * depth + 1
    small_pad = -(-n_small // 8) * 8
    small = jnp.concatenate(small_rows + [jnp.zeros((small_pad - n_small, d), F32), g[:n_meta]], axis=0)
    small_all = _allgather_small(small, "allgather_small")

    def replicated(rows, width, w, m, v, name):
        parts = lax.slice(small_all, (0, rows[0], 0), (N_DEV, rows[1], width))
        return _adamw(parts, w, m, v, name)

    out_ng = replicated((0, depth), d, norm_gain, m_norm_gain, v_norm_gain, "adamw_norm_gain")
    out_ps = replicated((depth, 2 * depth), pool_width, pool_scale, m_pool_scale, v_pool_scale, "adamw_pool_scale")
    out_fg = [t[0] for t in replicated((2 * depth, 2 * depth + 1), d, final_gain[None], m_final_gain[None],
                                       v_final_gain[None], "adamw_final_gain")]
    cols = d // N_DEV
    meta_parts = lax.dynamic_slice(small_all, (0, small_pad, my_index * cols), (N_DEV, n_meta, cols))
    out_meta = _adamw(meta_parts, meta_tokens, m_meta_tokens, v_meta_tokens, "adamw_meta")

    def sharded(k, w, m, v, name):
        per_layer = []
        for i in range(depth):
            parts = recv[i][k]
            shape2 = (-1, parts.shape[-1])
            res = _adamw(parts.reshape(N_DEV, *w[i].reshape(shape2).shape), w[i].reshape(shape2),
                         m[i].reshape(shape2), v[i].reshape(shape2), name)
            per_layer.append([t.reshape(w[i].shape) for t in res])
        return [jnp.stack([per_layer[i][t] for i in range(depth)]) for t in range(4)]

    out_wi = sharded(0, w_in, m_w_in, v_w_in, "adamw_w_in")
    out_pw = sharded(1, pool_w, m_pool_w, v_pool_w, "adamw_pool_w")
    out_au = sharded(2, w_attn_up, m_w_attn_up, v_w_attn_up, "adamw_w_attn_up")
    out_pu = sharded(3, w_pool_up, m_w_pool_up, v_w_pool_up, "adamw_w_pool_up")
    out_wo = sharded(4, w_out, m_w_out, v_w_out, "adamw_w_out")

    by_weight = [out_meta, out_ng, out_wi, out_pw, out_ps, out_au, out_pu, out_wo, out_fg]
    return (loss, grad_x, *[o[0] for o in by_weight], *[o[1] for o in by_weight],
            *[o[2] for o in by_weight], *[o[3] for o in by_weight])
```

```python
import jax
import jax.numpy as jnp
from jax import lax
from jax.experimental import pallas as pl
from jax.experimental.pallas import tpu as pltpu

F32 = jnp.float32
BF16 = jnp.bfloat16
MESH = pl.DeviceIdType.MESH

N_DEV = 8
HEAD_DIM = 128
ATT_BLOCK = 128
ATT_HEADS = 4
POOL_WINDOWS = (2, 4, 8, 16)
RMS_EPS = 1e-6
ADAM_LR, ADAM_B1, ADAM_B2, ADAM_EPS, ADAM_WD, ADAM_STEP = 0.001, 0.9, 0.999, 1e-08, 0.01, 10
VMEM_LIMIT_BYTES = 56 * 1024 * 1024


def _tile(n, target, mult):
    if n <= target:
        return n
    best = 0
    for t in range(mult, target + 1, mult):
        if n % t == 0:
            best = t
    assert best > 0, (n, target, mult)
    return best


def _params(*sem):
    return pltpu.CompilerParams(dimension_semantics=sem, vmem_limit_bytes=VMEM_LIMIT_BYTES)


def _sigmoid(x):
    return 1.0 / (1.0 + jnp.exp(-x))


def _dot(a, b, ta=False, tb=False):
    dims = (((0 if ta else 1,), (1 if tb else 0,)), ((), ()))
    return lax.dot_general(a.astype(BF16), b.astype(BF16), dims, preferred_element_type=F32)


def _dot_split(a, b):
    hi = a.astype(BF16)
    lo = (a - hi.astype(F32)).astype(BF16)
    return _dot(hi, b) + _dot(lo, b)


def _cast_bf16(x, name):
    a, r, c = x.shape
    tr = _tile(r, max(16, (512 * 1024) // c // 16 * 16), 16)

    def body(x_ref, o_ref):
        o_ref[...] = x_ref[...].astype(BF16)

    return pl.pallas_call(
        body, name=name, grid=(a, r // tr),
        in_specs=[pl.BlockSpec((None, tr, c), lambda i, j: (i, j, 0))],
        out_specs=pl.BlockSpec((None, tr, c), lambda i, j: (i, j, 0)),
        out_shape=jax.ShapeDtypeStruct(x.shape, BF16),
        compiler_params=_params("parallel", "parallel"),
    )(x)


def _matmul(a, b, *, ta=False, tb=False, out_dtype=F32, add=None, name, tm=1088, tn=1024, tk=2048):
    a_list = list(a) if isinstance(a, (list, tuple)) else None
    b_list = list(b) if isinstance(b, (list, tuple)) else None
    assert not (a_list and ta) and not (b_list and tb) and not (a_list and b_list)
    if a_list:
        m, k = a_list[0].shape[0], sum(p.shape[1] for p in a_list)
    else:
        m, k = (a.shape[1], a.shape[0]) if ta else a.shape
    if b_list:
        n = sum(p.shape[1] for p in b_list)
    else:
        n = b.shape[0] if tb else b.shape[1]
    tm = _tile(m, tm, 128 if ta else 16)
    tn = _tile(n, tn, 128)
    tk = _tile(k, tk, 128 if (not ta or tb) else 16)
    if a_list:
        while any(p.shape[1] % tk for p in a_list):
            tk //= 2
    if b_list:
        while any(p.shape[1] % tn for p in b_list):
            tn //= 2
    nk = k // tk

    def ranges(pieces, tile):
        out, start = [], 0
        for p in pieces:
            out.append((start, start + p.shape[1] // tile))
            start = out[-1][1]
        return out

    if a_list:
        a_ranges = ranges(a_list, tk)
        a_specs = [pl.BlockSpec((tm, tk), lambda i, j, kk, s=s, e=e: (i, jnp.clip(kk - s, 0, e - s - 1)))
                   for s, e in a_ranges]
    else:
        a_specs = [pl.BlockSpec((tk, tm), lambda i, j, kk: (kk, i)) if ta
                   else pl.BlockSpec((tm, tk), lambda i, j, kk: (i, kk))]
    if b_list:
        b_ranges = ranges(b_list, tn)
        b_specs = [pl.BlockSpec((tk, tn), lambda i, j, kk, s=s, e=e: (
            jnp.where(jnp.logical_and(j >= s, j < e), kk, 0), jnp.clip(j - s, 0, e - s - 1))) for s, e in b_ranges]
    else:
        b_specs = [pl.BlockSpec((tn, tk), lambda i, j, kk: (j, kk)) if tb
                   else pl.BlockSpec((tk, tn), lambda i, j, kk: (kk, j))]
    o_spec = pl.BlockSpec((tm, tn), lambda i, j, kk: (i, j))
    has_add = add is not None
    na, nb = len(a_specs), len(b_specs)
    simple = nk == 1 and not a_list and not b_list

    def body(*refs):
        a_refs, b_refs = refs[:na], refs[na:na + nb]
        add_ref = refs[na + nb] if has_add else None
        o_ref, acc_ref = refs[-2], refs[-1]
        j, kk = pl.program_id(1), pl.program_id(2)

        def finish(total):
            if has_add:
                total = total + add_ref[...]
            o_ref[...] = total.astype(out_dtype)

        if simple:
            finish(_dot(a_refs[0][...], b_refs[0][...], ta, tb))
            return

        @pl.when(kk == 0)
        def _():
            acc_ref[...] = jnp.zeros_like(acc_ref)

        if a_list:
            for (s, e), a_ref in zip(a_ranges, a_refs):
                @pl.when(jnp.logical_and(kk >= s, kk < e))
                def _(a_ref=a_ref):
                    acc_ref[...] += _dot(a_ref[...], b_refs[0][...], ta, tb)
        elif b_list:
            for (s, e), b_ref in zip(b_ranges, b_refs):
                @pl.when(jnp.logical_and(j >= s, j < e))
                def _(b_ref=b_ref):
                    acc_ref[...] += _dot(a_refs[0][...], b_ref[...], ta, tb)
        else:
            acc_ref[...] += _dot(a_refs[0][...], b_refs[0][...], ta, tb)

        @pl.when(kk == nk - 1)
        def _():
            finish(acc_ref[...])

    operands = (a_list or [a]) + (b_list or [b]) + ([add] if has_add else [])
    in_specs = a_specs + b_specs + ([o_spec] if has_add else [])
    return pl.pallas_call(
        body, name=name, grid=(m // tm, n // tn, nk),
        in_specs=in_specs, out_specs=o_spec,
        out_shape=jax.ShapeDtypeStruct((m, n), out_dtype),
        scratch_shapes=[pltpu.VMEM((8, 128) if simple else (tm, tn), F32)],
        compiler_params=_params("parallel", "parallel", "arbitrary"),
    )(*operands)


def _rms_fwd(x, gain, name):
    l, d = x.shape
    tr = _tile(l, 272, 16)

    def body(x_ref, g_ref, o_ref):
        xv = x_ref[...]
        r = lax.rsqrt(jnp.mean(xv * xv, axis=-1, keepdims=True) + RMS_EPS)
        o_ref[...] = (xv * r * g_ref[...]).astype(BF16)

    return pl.pallas_call(
        body, name=name, grid=(l // tr,),
        in_specs=[pl.BlockSpec((tr, d), lambda i: (i, 0)), pl.BlockSpec((1, d), lambda i: (0, 0))],
        out_specs=pl.BlockSpec((tr, d), lambda i: (i, 0)),
        out_shape=jax.ShapeDtypeStruct((l, d), BF16),
        compiler_params=_params("parallel"),
    )(x, gain)


def _rms_bwd(dh, x, gain, gout, name):
    l, d = x.shape
    tr = _tile(l, 272, 16)

    def body(dh_ref, x_ref, g_ref, go_ref, gx_ref, gxb_ref, dg_ref):
        xv = x_ref[...]
        r = lax.rsqrt(jnp.mean(xv * xv, axis=-1, keepdims=True) + RMS_EPS)
        nv = xv * r
        dhv = dh_ref[...]
        dn = dhv * g_ref[...]
        dx = r * (dn - nv * jnp.mean(dn * nv, axis=-1, keepdims=True))
        gx = go_ref[...] + dx
        gx_ref[...] = gx
        gxb_ref[...] = gx.astype(BF16)
        part = jnp.sum(dhv * nv, axis=0, keepdims=True)

        @pl.when(pl.program_id(0) == 0)
        def _():
            dg_ref[...] = part

        @pl.when(pl.program_id(0) > 0)
        def _():
            dg_ref[...] += part

    row = pl.BlockSpec((tr, d), lambda i: (i, 0))
    vec = pl.BlockSpec((1, d), lambda i: (0, 0))
    return pl.pallas_call(
        body, name=name, grid=(l // tr,),
        in_specs=[row, row, vec, row], out_specs=[row, row, vec],
        out_shape=[jax.ShapeDtypeStruct((l, d), F32), jax.ShapeDtypeStruct((l, d), BF16),
                   jax.ShapeDtypeStruct((1, d), F32)],
        compiler_params=_params("arbitrary"),
    )(dh, x, gain, gout)


def _final_loss(x, gain, target, row_lo, row_hi, name):
    l, d = x.shape
    tr = _tile(l, 272, 16)

    def body(x_ref, g_ref, t_ref, gx_ref, gxb_ref, dg_ref, loss_ref):
        i = pl.program_id(0)
        xv = x_ref[...]
        r = lax.rsqrt(jnp.mean(xv * xv, axis=-1, keepdims=True) + RMS_EPS)
        nv = xv * r
        gv = g_ref[...]
        rows = i * tr + lax.broadcasted_iota(jnp.int32, (tr, 1), 0)
        valid = jnp.logical_and(rows >= row_lo, rows < row_hi)
        err = jnp.where(valid, nv * gv - t_ref[...], 0.0)
        dy = err * (1.0 / d)
        dn = dy * gv
        gx = r * (dn - nv * jnp.mean(dn * nv, axis=-1, keepdims=True))
        gx_ref[...] = gx
        gxb_ref[...] = gx.astype(BF16)
        part = jnp.sum(dy * nv, axis=0, keepdims=True)
        lpart = jnp.full((1, 128), 0.5 * jnp.sum(jnp.mean(err * err, axis=-1, keepdims=True)), F32)

        @pl.when(i == 0)
        def _():
            dg_ref[...] = part
            loss_ref[...] = lpart

        @pl.when(i > 0)
        def _():
            dg_ref[...] += part
            loss_ref[...] += lpart

    row = pl.BlockSpec((tr, d), lambda i: (i, 0))
    vec = pl.BlockSpec((1, d), lambda i: (0, 0))
    return pl.pallas_call(
        body, name=name, grid=(l // tr,),
        in_specs=[row, vec, row], out_specs=[row, row, vec, pl.BlockSpec((1, 128), lambda i: (0, 0))],
        out_shape=[jax.ShapeDtypeStruct((l, d), F32), jax.ShapeDtypeStruct((l, d), BF16),
                   jax.ShapeDtypeStruct((1, d), F32), jax.ShapeDtypeStruct((1, 128), F32)],
        compiler_params=_params("arbitrary"),
    )(x, gain, target)


def _log1m_beta(z):
    return -(jnp.maximum(z, 0.0) + jnp.log(1.0 + jnp.exp(-jnp.abs(z))))


def _tri(n, relation):
    r = lax.broadcasted_iota(jnp.int32, (n, n), 0)
    c = lax.broadcasted_iota(jnp.int32, (n, n), 1)
    return jnp.where(relation(r, c), 1.0, 0.0).astype(BF16)


def _head(hh):
    return slice(hh * HEAD_DIM, (hh + 1) * HEAD_DIM)


def _attn_specs(l, hb, n_heads):
    width = hb * HEAD_DIM
    groups = n_heads // hb

    def tile(section):
        return pl.BlockSpec((ATT_BLOCK, width), lambda h, i: (i, section * groups + h))

    def slab(section):
        return pl.BlockSpec((l, width), lambda h, i: (0, section * groups + h))

    return tile, slab


def _attn_fwd(zin, n_heads, name):
    l = zin.shape[0]
    t = ATT_BLOCK
    hb = ATT_HEADS if n_heads % ATT_HEADS == 0 else 1
    scale = HEAD_DIM ** -0.5

    def body(q_ref, k_ref, v_ref, g_ref, o_ref, oa_ref, tot_ref):
        i = pl.program_id(1)
        after = _tri(t, lambda r, c: r > c)
        causal = lax.broadcasted_iota(jnp.int32, (t, t), 1) < lax.broadcasted_iota(jnp.int32, (t, t), 0)
        qs = [q_ref[:, _head(hh)].astype(BF16) for hh in range(hb)]

        def tile(hh, k0, run, acc, diagonal):
            z = _dot(qs[hh], k_ref[pl.ds(k0, t), _head(hh)], tb=True) * scale
            lb_all = _log1m_beta(z)
            lb = jnp.where(causal, lb_all, 0.0) if diagonal else lb_all
            a = jnp.exp(z + lb_all + _dot_split(lb, after) + run)
            if diagonal:
                a = jnp.where(causal, a, 0.0)
            return run + jnp.sum(lb, axis=1, keepdims=True), acc + _dot(a, v_ref[pl.ds(k0, t), _head(hh)])

        carry = []
        for hh in range(hb):
            carry += tile(hh, pl.multiple_of(i * t, t), jnp.zeros((t, 1), F32), jnp.zeros((t, HEAD_DIM), F32), True)

        def step(it, carry):
            k0 = pl.multiple_of((i - 1 - it) * t, t)
            out = []
            for hh in range(hb):
                out += tile(hh, k0, carry[2 * hh], carry[2 * hh + 1], False)
            return tuple(out)

        carry = lax.fori_loop(0, i, step, tuple(carry))
        for hh in range(hb):
            gate = g_ref[:, _head(hh)]
            o = carry[2 * hh + 1]
            o_ref[:, _head(hh)] = o
            oa_ref[:, _head(hh)] = (o * (gate * _sigmoid(gate))).astype(BF16)
            tot_ref[:, _head(hh)] = jnp.broadcast_to(carry[2 * hh], (t, HEAD_DIM))

    tile_spec, slab_spec = _attn_specs(l, hb, n_heads)
    width = n_heads * HEAD_DIM
    return pl.pallas_call(
        body, name=name, grid=(n_heads // hb, l // t),
        in_specs=[tile_spec(0), slab_spec(1), slab_spec(2), tile_spec(3)],
        out_specs=[tile_spec(0), tile_spec(0), tile_spec(0)],
        out_shape=[jax.ShapeDtypeStruct((l, width), F32), jax.ShapeDtypeStruct((l, width), BF16),
                   jax.ShapeDtypeStruct((l, width), F32)],
        compiler_params=_params("parallel", "arbitrary"),
    )(zin, zin, zin, zin)


def _attn_bwd(zin, o, tot, doa, n_heads, name):
    l = zin.shape[0]
    t = ATT_BLOCK
    nq = l // t
    hb = ATT_HEADS if n_heads % ATT_HEADS == 0 else 1
    scale = HEAD_DIM ** -0.5

    def body(q_ref, k_ref, v_ref, g_ref, o_ref, tot_ref, doa_ref, dq_ref, dk_ref, dv_ref, dg_ref, dk_acc, dv_acc):
        i = pl.program_id(1)

        @pl.when(i == 0)
        def _():
            dk_acc[...] = jnp.zeros_like(dk_acc)
            dv_acc[...] = jnp.zeros_like(dv_acc)

        upto = _tri(t, lambda r, c: r <= c)
        before = _tri(t, lambda r, c: r < c)
        causal = lax.broadcasted_iota(jnp.int32, (t, t), 1) < lax.broadcasted_iota(jnp.int32, (t, t), 0)
        qs, dos, totals = [], [], []
        for hh in range(hb):
            gate = g_ref[:, _head(hh)]
            sg = _sigmoid(gate)
            doav = doa_ref[:, _head(hh)]
            dg_ref[:, _head(hh)] = (doav * o_ref[:, _head(hh)] * (sg * (1.0 + gate * (1.0 - sg)))).astype(BF16)
            dos.append((doav * (gate * sg)).astype(BF16))
            qs.append(q_ref[:, _head(hh)].astype(BF16))
            totals.append(tot_ref[:, hh * HEAD_DIM:hh * HEAD_DIM + 1])

        def tile(hh, k0, run, pre, dq, diagonal):
            kb = k_ref[pl.ds(k0, t), _head(hh)].astype(BF16)
            z = _dot(qs[hh], kb, tb=True) * scale
            lb_all = _log1m_beta(z)
            lb = jnp.where(causal, lb_all, 0.0) if diagonal else lb_all
            beta = jnp.exp(z + lb_all)
            a = beta * jnp.exp(totals[hh] - run - _dot_split(lb, upto))
            if diagonal:
                a = jnp.where(causal, a, 0.0)
            e = a * _dot(dos[hh], v_ref[pl.ds(k0, t), _head(hh)], tb=True)
            dz = (e * (1.0 - beta) - beta * (pre + _dot(e, before))) * scale
            if diagonal:
                dz = jnp.where(causal, dz, 0.0)
            dv_acc[pl.ds(k0, t), _head(hh)] += _dot(a, dos[hh], ta=True)
            dk_acc[pl.ds(k0, t), _head(hh)] += _dot(dz, qs[hh], ta=True)
            return (run + jnp.sum(lb, axis=1, keepdims=True), pre + jnp.sum(e, axis=1, keepdims=True),
                    dq + _dot(dz, kb))

        def step(j, carry):
            k0 = pl.multiple_of(j * t, t)
            out = []
            for hh in range(hb):
                out += tile(hh, k0, *carry[3 * hh:3 * hh + 3], False)
            return tuple(out)

        zero = jnp.zeros((t, 1), F32)
        carry = lax.fori_loop(0, i, step, (zero, zero, jnp.zeros((t, HEAD_DIM), F32)) * hb)
        for hh in range(hb):
            _, _, dq = tile(hh, pl.multiple_of(i * t, t), *carry[3 * hh:3 * hh + 3], True)
            dq_ref[:, _head(hh)] = dq.astype(BF16)

        @pl.when(i == nq - 1)
        def _():
            dk_ref[...] = dk_acc[...].astype(BF16)
            dv_ref[...] = dv_acc[...].astype(BF16)

    tile_spec, slab_spec = _attn_specs(l, hb, n_heads)
    out = jax.ShapeDtypeStruct((l, n_heads * HEAD_DIM), BF16)
    return pl.pallas_call(
        body, name=name, grid=(n_heads // hb, nq),
        in_specs=[tile_spec(0), slab_spec(1), slab_spec(2), tile_spec(3), tile_spec(0), tile_spec(0), tile_spec(0)],
        out_specs=[tile_spec(0), slab_spec(0), slab_spec(0), tile_spec(0)],
        out_shape=[out, out, out, out],
        scratch_shapes=[pltpu.VMEM((l, hb * HEAD_DIM), F32), pltpu.VMEM((l, hb * HEAD_DIM), F32)],
        compiler_params=_params("parallel", "arbitrary"),
    )(zin, zin, zin, zin, o, tot, doa)


def _shift_rows(x, k, down):
    n = x.shape[0]
    rows = lax.broadcasted_iota(jnp.int32, x.shape, 0)
    if down:
        return jnp.where(rows >= k, pltpu.roll(x, k, 0), 0.0)
    return jnp.where(rows < n - k, pltpu.roll(x, n - k, 0), 0.0)


def _window_sum(x, g, down):
    result = x
    total = x
    for step, k in enumerate((1, 2, 4, 8)):
        total = total + _shift_rows(total, k, down)
        result = jnp.where(g >= step, total, result)
    return result


def _pooled(u, g):
    rows = lax.broadcasted_iota(jnp.int32, (u.shape[0], 1), 0)
    window = jnp.left_shift(2, g)
    cnt = jnp.minimum(rows + 1, window).astype(F32)
    return _window_sum(u, g, True) / cnt - u, cnt


def _pool_fwd(zin, pool_w, pool_scale, u_off, name):
    l = zin.shape[0]
    n_groups, gd, _ = pool_w.shape

    def body(u_ref, g_ref, w_ref, s_ref, o_ref):
        g = pl.program_id(0)
        pooled, _ = _pooled(u_ref[...], g)
        mixed = _dot(pooled, w_ref[...])
        gate = g_ref[...]
        o_ref[...] = (mixed * s_ref[...] * (gate * _sigmoid(gate))).astype(BF16)

    return pl.pallas_call(
        body, name=name, grid=(n_groups,),
        in_specs=[pl.BlockSpec((l, gd), lambda g: (0, u_off + g)),
                  pl.BlockSpec((l, gd), lambda g: (0, u_off + n_groups + g)),
                  pl.BlockSpec((None, gd, gd), lambda g: (g, 0, 0)),
                  pl.BlockSpec((1, gd), lambda g: (0, g))],
        out_specs=pl.BlockSpec((l, gd), lambda g: (0, g)),
        out_shape=jax.ShapeDtypeStruct((l, n_groups * gd), BF16),
        compiler_params=_params("parallel"),
    )(zin, zin, pool_w, pool_scale)


def _pool_bwd(zin, dop, pool_w, pool_scale, u_off, name):
    l = zin.shape[0]
    n_groups, gd, _ = pool_w.shape

    def body(u_ref, g_ref, w_ref, s_ref, d_ref, du_ref, dg_ref, dw_ref, ds_ref):
        g = pl.program_id(0)
        pooled, cnt = _pooled(u_ref[...], g)
        w = w_ref[...]
        mixed = _dot(pooled, w)
        gate = g_ref[...]
        sg = _sigmoid(gate)
        silu = gate * sg
        dop_v = d_ref[...]
        sc = s_ref[...]
        ds_ref[...] = jnp.sum(dop_v * mixed * silu, axis=0, keepdims=True)
        dg_ref[...] = (dop_v * mixed * sc * (sg * (1.0 + gate * (1.0 - sg)))).astype(BF16)
        dmixed = dop_v * sc * silu
        dw_ref[...] = _dot(pooled, dmixed, ta=True).astype(BF16)
        dpooled = _dot(dmixed, w, tb=True)
        du_ref[...] = (_window_sum(dpooled / cnt, g, False) - dpooled).astype(BF16)

    slab = pl.BlockSpec((l, gd), lambda g: (0, g))
    return pl.pallas_call(
        body, name=name, grid=(n_groups,),
        in_specs=[pl.BlockSpec((l, gd), lambda g: (0, u_off + g)),
                  pl.BlockSpec((l, gd), lambda g: (0, u_off + n_groups + g)),
                  pl.BlockSpec((None, gd, gd), lambda g: (g, 0, 0)),
                  pl.BlockSpec((1, gd), lambda g: (0, g)), slab],
        out_specs=[slab, slab, pl.BlockSpec((None, gd, gd), lambda g: (g, 0, 0)),
                   pl.BlockSpec((1, gd), lambda g: (0, g))],
        out_shape=[jax.ShapeDtypeStruct((l, n_groups * gd), BF16), jax.ShapeDtypeStruct((l, n_groups * gd), BF16),
                   jax.ShapeDtypeStruct(pool_w.shape, BF16), jax.ShapeDtypeStruct((1, n_groups * gd), F32)],
        compiler_params=_params("parallel"),
    )(zin, zin, pool_w, pool_scale, dop)


def _merge_fwd(oa, op, w_au, w_pu, zin, name):
    l, wa = oa.shape
    wp = op.shape[1]
    d = w_au.shape[1]
    tm = _tile(l, 544, 16)
    tn = _tile(d, 512, 128)
    ma_off = (zin.shape[1] - 2 * d) // tn

    def body(oa_ref, op_ref, wa_ref, wp_ref, ma_ref, mp_ref, ya_ref, yp_ref, mg_ref):
        ya = _dot(oa_ref[...], wa_ref[...])
        yp = _dot(op_ref[...], wp_ref[...])
        ya_ref[...] = ya
        yp_ref[...] = yp
        mg_ref[...] = (_sigmoid(ma_ref[...]) * ya + _sigmoid(mp_ref[...]) * yp).astype(BF16)

    tile = pl.BlockSpec((tm, tn), lambda i, j: (i, j))
    return pl.pallas_call(
        body, name=name, grid=(l // tm, d // tn),
        in_specs=[pl.BlockSpec((tm, wa), lambda i, j: (i, 0)), pl.BlockSpec((tm, wp), lambda i, j: (i, 0)),
                  pl.BlockSpec((wa, tn), lambda i, j: (0, j)), pl.BlockSpec((wp, tn), lambda i, j: (0, j)),
                  pl.BlockSpec((tm, tn), lambda i, j: (i, ma_off + j)),
                  pl.BlockSpec((tm, tn), lambda i, j: (i, ma_off + d // tn + j))],
        out_specs=[tile, tile, tile],
        out_shape=[jax.ShapeDtypeStruct((l, d), F32), jax.ShapeDtypeStruct((l, d), F32),
                   jax.ShapeDtypeStruct((l, d), BF16)],
        compiler_params=_params("parallel", "parallel"),
    )(oa, op, w_au, w_pu, zin, zin)


def _merge_bwd(gout, w_out, zin, ya, yp, name):
    l, d = gout.shape
    tm = _tile(l, 544, 16)
    tn = _tile(d, 512, 128)
    ma_off = (zin.shape[1] - 2 * d) // tn

    def body(g_ref, w_ref, ma_ref, mp_ref, ya_ref, yp_ref, dya_ref, dyp_ref, dma_ref, dmp_ref):
        dm = _dot(g_ref[...], w_ref[...], tb=True)
        sa = _sigmoid(ma_ref[...])
        sp = _sigmoid(mp_ref[...])
        dya_ref[...] = (dm * sa).astype(BF16)
        dyp_ref[...] = (dm * sp).astype(BF16)
        dma_ref[...] = (dm * ya_ref[...] * (sa * (1.0 - sa))).astype(BF16)
        dmp_ref[...] = (dm * yp_ref[...] * (sp * (1.0 - sp))).astype(BF16)

    tile = pl.BlockSpec((tm, tn), lambda i, j: (i, j))
    out = jax.ShapeDtypeStruct((l, d), BF16)
    return pl.pallas_call(
        body, name=name, grid=(l // tm, d // tn),
        in_specs=[pl.BlockSpec((tm, d), lambda i, j: (i, 0)), pl.BlockSpec((tn, d), lambda i, j: (j, 0)),
                  pl.BlockSpec((tm, tn), lambda i, j: (i, ma_off + j)),
                  pl.BlockSpec((tm, tn), lambda i, j: (i, ma_off + d // tn + j)), tile, tile],
        out_specs=[tile, tile, tile, tile],
        out_shape=[out, out, out, out],
        compiler_params=_params("parallel", "parallel"),
    )(gout, w_out, zin, zin, ya, yp)


def _adamw(parts, w, m, v, name):
    n_arrays, n_parts, r, c = parts.shape
    tr = _tile(r, max(8, (128 * 1024) // c // 8 * 8), 8)
    bias1 = 1.0 - ADAM_B1 ** ADAM_STEP
    bias2 = 1.0 - ADAM_B2 ** ADAM_STEP

    def body(p_ref, w_ref, m_ref, v_ref, g_ref, d_ref, nm_ref, nv_ref):
        g = p_ref[0].astype(F32)
        for j in range(1, n_parts):
            g = g + p_ref[j].astype(F32)
        nm = ADAM_B1 * m_ref[...] + (1.0 - ADAM_B1) * g
        nv = ADAM_B2 * v_ref[...] + (1.0 - ADAM_B2) * (g * g)
        g_ref[...] = g
        nm_ref[...] = nm
        nv_ref[...] = nv
        d_ref[...] = -ADAM_LR * ((nm / bias1) / (jnp.sqrt(nv / bias2) + ADAM_EPS) + ADAM_WD * w_ref[...])

    tile = pl.BlockSpec((None, tr, c), lambda a, i: (a, i, 0))
    out = jax.ShapeDtypeStruct((n_arrays, r, c), F32)
    return pl.pallas_call(
        body, name=name, grid=(n_arrays, r // tr),
        in_specs=[pl.BlockSpec((None, n_parts, tr, c), lambda a, i: (a, 0, i, 0)), tile, tile, tile],
        out_specs=[tile, tile, tile, tile], out_shape=[out, out, out, out],
        compiler_params=_params("parallel", "parallel"),
    )(parts, w, m, v)


def _position():
    return lax.axis_index("x"), lax.axis_index("y"), lax.axis_index("c")


def _block_of(ref, axis, size, index):
    idx = [slice(None)] * len(ref.shape)
    idx[axis] = pl.ds(index * size, size)
    return ref.at[tuple(idx)]


def _allgather_blocks(shards, axes, name):
    n = len(shards)
    full_shapes = []
    for s, ax in zip(shards, axes):
        shape = list(s.shape)
        shape[ax] *= N_DEV
        full_shapes.append(jax.ShapeDtypeStruct(tuple(shape), s.dtype))

    def body(*refs):
        ins, outs = refs[:n], refs[n:2 * n]
        send_sems, recv_sems, local_sems = refs[2 * n:]
        x, y, c = _position()
        me, sibling = (x, y, c), (x, y, 1 - c)
        chips = [(1 - x, y), (x, 1 - y), (1 - x, 1 - y)]

        def slot(a, dev):
            return _block_of(outs[a], axes[a], shards[a].shape[axes[a]], 4 * dev[0] + 2 * dev[1] + dev[2])

        def copy(a, k, block, to, src=None):
            return pltpu.make_async_remote_copy(
                src_ref=slot(a, block) if src is None else src, dst_ref=slot(a, block),
                send_sem=send_sems.at[a * 7 + k], recv_sem=recv_sems.at[a * 7 + k],
                device_id=to, device_id_type=MESH)

        mine = [pltpu.make_async_copy(ins[a], slot(a, me), local_sems.at[a]) for a in range(n)]
        for cp in mine:
            cp.start()
        first = []
        for a in range(n):
            first.append(copy(a, 0, me, sibling, src=ins[a]))
            first += [copy(a, 1 + j, me, (*chip, c), src=ins[a]) for j, chip in enumerate(chips)]
        for cp in first:
            cp.start()
        passed = []
        for j, chip in enumerate(chips):
            for a in range(n):
                copy(a, 1 + j, (*chip, c), me).wait_recv()
                fwd = copy(a, 4 + j, (*chip, c), sibling)
                fwd.start()
                passed.append(fwd)
        for a in range(n):
            copy(a, 0, sibling, me).wait_recv()
            for j, chip in enumerate(chips):
                copy(a, 4 + j, (*chip, 1 - c), me).wait_recv()
        for cp in first + passed:
            cp.wait_send()
        for cp in mine:
            cp.wait()

    hbm = pl.BlockSpec(memory_space=pl.ANY)
    return pl.pallas_call(
        body, name=name, in_specs=[hbm] * n, out_specs=[hbm] * n, out_shape=full_shapes,
        scratch_shapes=[pltpu.SemaphoreType.DMA((7 * n,)), pltpu.SemaphoreType.DMA((7 * n,)),
                        pltpu.SemaphoreType.DMA((n,))],
    )(*shards)


def _exchange_grads(fulls, landing, axes, layer, name):
    n = len(fulls)
    sizes = [f.shape[ax] // N_DEV for f, ax in zip(fulls, axes)]
    flips = [(a, b, d) for a in (0, 1) for b in (0, 1) for d in (0, 1)][1:]

    def body(*refs):
        ins, outs = refs[:n], refs[2 * n:3 * n]
        send_sems, recv_sems, local_sems = refs[3 * n:]
        x, y, c = _position()
        my_index = 4 * x + 2 * y + c

        def peer_of(flip):
            return (x ^ flip[0], y ^ flip[1], c ^ flip[2])

        def copy(a, k, flip):
            px, py, pc = peer_of(flip)
            return pltpu.make_async_remote_copy(
                src_ref=_block_of(ins[a], axes[a], sizes[a], 4 * px + 2 * py + pc),
                dst_ref=outs[a].at[layer, my_index],
                send_sem=send_sems.at[a * 7 + k], recv_sem=recv_sems.at[a * 7 + k],
                device_id=(px, py, pc), device_id_type=MESH)

        mine = [pltpu.make_async_copy(_block_of(ins[a], axes[a], sizes[a], my_index), outs[a].at[layer, my_index],
                                      local_sems.at[a]) for a in range(n)]
        for cp in mine:
            cp.start()
        sends = [copy(a, k, flip) for k, flip in enumerate(flips) for a in range(n)]
        for cp in sends:
            cp.start()
        for k, flip in enumerate(flips):
            px, py, pc = peer_of(flip)
            for a in range(n):
                pltpu.make_async_remote_copy(
                    src_ref=outs[a].at[layer, my_index], dst_ref=outs[a].at[layer, 4 * px + 2 * py + pc],
                    send_sem=send_sems.at[a * 7 + k], recv_sem=recv_sems.at[a * 7 + k],
                    device_id=(px, py, pc), device_id_type=MESH).wait_recv()
        for cp in sends:
            cp.wait_send()
        for cp in mine:
            cp.wait()

    hbm = pl.BlockSpec(memory_space=pl.ANY)
    return pl.pallas_call(
        body, name=name, in_specs=[hbm] * (2 * n), out_specs=[hbm] * n,
        out_shape=[jax.ShapeDtypeStruct(b.shape, b.dtype) for b in landing],
        input_output_aliases={n + a: a for a in range(n)},
        scratch_shapes=[pltpu.SemaphoreType.DMA((7 * n,)), pltpu.SemaphoreType.DMA((7 * n,)),
                        pltpu.SemaphoreType.DMA((n,))],
    )(*fulls, *landing)


def _allgather_small(v, name):
    r, c = v.shape
    flips = [(a, b, d) for a in (0, 1) for b in (0, 1) for d in (0, 1)][1:]

    def body(v_ref, out_ref, send_sems, recv_sems):
        x, y, c_ = _position()
        my_index = 4 * x + 2 * y + c_
        out_ref[my_index] = v_ref[...]
        sends = []
        for k, flip in enumerate(flips):
            peer = (x ^ flip[0], y ^ flip[1], c_ ^ flip[2])
            cp = pltpu.make_async_remote_copy(
                src_ref=v_ref, dst_ref=out_ref.at[my_index],
                send_sem=send_sems.at[k], recv_sem=recv_sems.at[k], device_id=peer, device_id_type=MESH)
            cp.start()
            sends.append(cp)
        for k, flip in enumerate(flips):
            px, py, pc = x ^ flip[0], y ^ flip[1], c_ ^ flip[2]
            pltpu.make_async_remote_copy(
                src_ref=v_ref, dst_ref=out_ref.at[4 * px + 2 * py + pc],
                send_sem=send_sems.at[k], recv_sem=recv_sems.at[k],
                device_id=(px, py, pc), device_id_type=MESH).wait_recv()
        for cp in sends:
            cp.wait_send()

    return pl.pallas_call(
        body, name=name,
        in_specs=[pl.BlockSpec(memory_space=pltpu.VMEM)], out_specs=pl.BlockSpec(memory_space=pltpu.VMEM),
        out_shape=jax.ShapeDtypeStruct((N_DEV, r, c), v.dtype),
        scratch_shapes=[pltpu.SemaphoreType.DMA((7,)), pltpu.SemaphoreType.DMA((7,))],
    )(v)


def kernel(x, meta_tokens, norm_gain, w_in, pool_w, pool_scale, w_attn_up, w_pool_up, w_out, final_gain, loss_target, m_meta_tokens, m_norm_gain, m_w_in, m_pool_w, m_pool_scale, m_w_attn_up, m_w_pool_up, m_w_out, m_final_gain, v_meta_tokens, v_norm_gain, v_w_in, v_pool_w, v_pool_scale, v_w_attn_up, v_w_pool_up, v_w_out, v_final_gain):
    _, seq, d = x.shape
    n_meta = meta_tokens.shape[0]
    depth = w_in.shape[0]
    sb_width = w_attn_up.shape[1]
    pool_width = w_pool_up.shape[1]
    n_heads = sb_width // HEAD_DIM
    n_groups = pool_w.shape[1]
    gd = pool_w.shape[3]
    assert n_groups == len(POOL_WINDOWS) and gd * n_groups == pool_width
    assert w_in.shape[2] * N_DEV == 4 * sb_width + 2 * pool_width + 2 * d
    l_real = n_meta + seq
    l_pad = -(-l_real // ATT_BLOCK) * ATT_BLOCK
    my_index = 4 * lax.axis_index("x") + 2 * lax.axis_index("y") + lax.axis_index("c")

    pool_w3 = pool_w.reshape(depth, n_groups * pool_w.shape[2], gd)
    w_in_b = _cast_bf16(w_in, "cast_w_in")
    pool_w_b = _cast_bf16(pool_w3, "cast_pool_w").reshape(pool_w.shape)
    w_au_b = _cast_bf16(w_attn_up, "cast_w_attn_up")
    w_pu_b = _cast_bf16(w_pool_up, "cast_w_pool_up")
    w_out_b = _cast_bf16(w_out, "cast_w_out")
    weights = [_allgather_blocks([w_in_b[i], pool_w_b[i], w_au_b[i], w_pu_b[i], w_out_b[i]], [1, 1, 1, 1, 0],
                                 "allgather_weights") for i in range(depth)]
    meta_all = _allgather_small(meta_tokens, "allgather_meta")
    meta_full = jnp.transpose(meta_all, (1, 0, 2)).reshape(n_meta, d)

    pad_rows = l_pad - l_real
    hs = jnp.concatenate([meta_full, x[0], jnp.zeros((pad_rows, d), F32)], axis=0)
    target = jnp.concatenate([jnp.zeros((n_meta, d), F32), loss_target[0], jnp.zeros((pad_rows, d), F32)], axis=0)
    u_off = 4 * sb_width // gd
    saved = []
    for i in range(depth):
        wi, pw, wau, wpu, wo = weights[i]
        h = _rms_fwd(hs, norm_gain[i][None], "rms_fwd")
        zin = _matmul(h, wi, name="mm_zin")
        o, oa, tot = _attn_fwd(zin, n_heads, "attn_fwd")
        op = _pool_fwd(zin, pw, pool_scale[i][None], u_off, "pool_fwd")
        ya, yp, merged = _merge_fwd(oa, op, wau, wpu, zin, "merge_fwd")
        saved.append((hs, h, zin, o, tot, oa, op, ya, yp, merged))
        hs = _matmul(merged, wo, add=hs, name="mm_out")
    g, gb, d_final_gain, loss_part = _final_loss(hs, final_gain[None], target, n_meta, l_real, "final_loss")
    loss = lax.psum(loss_part[0, 0], ("x", "y", "c"))

    d_norm_gain = [None] * depth
    d_pool_scale = [None] * depth
    blocks = [w_in[0], pool_w[0], w_attn_up[0], w_pool_up[0], w_out[0]]
    landing = [lax.empty((depth, N_DEV, *b.shape), BF16) for b in blocks]
    for i in reversed(range(depth)):
        wi, pw, wau, wpu, wo = weights[i]
        hs_in, h, zin, o, tot, oa, op, ya, yp, merged = saved[i]
        dya, dyp, dma, dmp = _merge_bwd(gb, wo, zin, ya, yp, "merge_bwd")
        dw_out = _matmul(merged, gb, ta=True, out_dtype=BF16, name="mm_dw_out", tm=1024, tk=1088)
        doa = _matmul(dya, wau, tb=True, name="mm_doa")
        dw_au = _matmul(oa, dya, ta=True, out_dtype=BF16, name="mm_dw_au", tm=1024, tk=1088)
        dop = _matmul(dyp, wpu, tb=True, name="mm_dop")
        dw_pu = _matmul(op, dyp, ta=True, out_dtype=BF16, name="mm_dw_pu", tm=1024, tk=1088)
        dq, dk, dv, dga = _attn_bwd(zin, o, tot, doa, n_heads, "attn_bwd")
        du, dgp, dpw, dps = _pool_bwd(zin, dop, pw, pool_scale[i][None], u_off, "pool_bwd")
        dzin = [dq, dk, dv, dga, du, dgp, dma, dmp]
        dw_in = _matmul(h, dzin, ta=True, out_dtype=BF16, name="mm_dw_in", tm=1024, tk=1088)
        dh = _matmul(dzin, wi, tb=True, name="mm_dh")
        g, gb, dng = _rms_bwd(dh, hs_in, norm_gain[i][None], g, "rms_bwd")
        d_norm_gain[i] = dng
        d_pool_scale[i] = dps
        landing = _exchange_grads([dw_in, dpw, dw_au, dw_pu, dw_out], landing, [1, 1, 1, 1, 0], i,
                                  "exchange_grads_%d" % i)
    grad_x = g[n_meta:l_real][None]

    zeros_ps = jnp.zeros((depth, d - pool_width), F32)
    small_rows = [jnp.concatenate(d_norm_gain, axis=0),
                  jnp.concatenate([jnp.concatenate(d_pool_scale, axis=0), zeros_ps], axis=1), d_final_gain]
    n_small = 2 * depth + 1
    small_pad = -(-n_small // 8) * 8
    small = jnp.concatenate(small_rows + [jnp.zeros((small_pad - n_small, d), F32), g[:n_meta]], axis=0)
    small_all = _allgather_small(small, "allgather_small")

    def replicated(rows, width, w, m, v, name):
        parts = lax.slice(small_all, (0, rows[0], 0), (N_DEV, rows[1], width))
        return [t[0] for t in _adamw(parts[None], w[None], m[None], v[None], name)]

    out_ng = replicated((0, depth), d, norm_gain, m_norm_gain, v_norm_gain, "adamw_norm_gain")
    out_ps = replicated((depth, 2 * depth), pool_width, pool_scale, m_pool_scale, v_pool_scale, "adamw_pool_scale")
    out_fg = [t[0] for t in replicated((2 * depth, 2 * depth + 1), d, final_gain[None], m_final_gain[None],
                                       v_final_gain[None], "adamw_final_gain")]
    cols = d // N_DEV
    meta_parts = lax.dynamic_slice(small_all, (0, small_pad, my_index * cols), (N_DEV, n_meta, cols))
    out_meta = [t[0] for t in _adamw(meta_parts[None], meta_tokens[None], m_meta_tokens[None], v_meta_tokens[None],
                                     "adamw_meta")]

    def sharded(parts, w, m, v, name):
        flat = (depth, -1, w.shape[-1])
        res = _adamw(parts.reshape(depth, N_DEV, -1, w.shape[-1]), w.reshape(flat), m.reshape(flat), v.reshape(flat),
                     name)
        return [t.reshape(w.shape) for t in res]

    out_wi = sharded(landing[0], w_in, m_w_in, v_w_in, "adamw_w_in")
    out_pw = sharded(landing[1], pool_w, m_pool_w, v_pool_w, "adamw_pool_w")
    out_au = sharded(landing[2], w_attn_up, m_w_attn_up, v_w_attn_up, "adamw_w_attn_up")
    out_pu = sharded(landing[3], w_pool_up, m_w_pool_up, v_w_pool_up, "adamw_w_pool_up")
    out_wo = sharded(landing[4], w_out, m_w_out, v_w_out, "adamw_w_out")

    by_weight = [out_meta, out_ng, out_wi, out_pw, out_ps, out_au, out_pu, out_wo, out_fg]
    return (loss, grad_x, *[o[0] for o in by_weight], *[o[1] for o in by_weight],
            *[o[2] for o in by_weight], *[o[3] for o in by_weight])
```

```python
import jax
import jax.numpy as jnp
from jax import lax
from jax.experimental import pallas as pl
from jax.experimental.pallas import tpu as pltpu

F32 = jnp.float32
BF16 = jnp.bfloat16
MESH = pl.DeviceIdType.MESH

N_DEV = 8
HEAD_DIM = 128
ATT_BLOCK = 128
ATT_HEADS = 4
POOL_WINDOWS = (2, 4, 8, 16)
RMS_EPS = 1e-6
ADAM_LR, ADAM_B1, ADAM_B2, ADAM_EPS, ADAM_WD, ADAM_STEP = 0.001, 0.9, 0.999, 1e-08, 0.01, 10
VMEM_LIMIT_BYTES = 56 * 1024 * 1024


def _tile(n, target, mult):
    if n <= target:
        return n
    best = 0
    for t in range(mult, target + 1, mult):
        if n % t == 0:
            best = t
    assert best > 0, (n, target, mult)
    return best


def _params(*sem):
    return pltpu.CompilerParams(dimension_semantics=sem, vmem_limit_bytes=VMEM_LIMIT_BYTES)


def _sigmoid(x):
    return 1.0 / (1.0 + jnp.exp(-x))


def _dot(a, b, ta=False, tb=False):
    dims = (((0 if ta else 1,), (1 if tb else 0,)), ((), ()))
    return lax.dot_general(a.astype(BF16), b.astype(BF16), dims, preferred_element_type=F32)


def _dot_split(a, b):
    hi = a.astype(BF16)
    lo = (a - hi.astype(F32)).astype(BF16)
    return _dot(hi, b) + _dot(lo, b)


def _cast_bf16(x, name):
    a, r, c = x.shape
    tr = _tile(r, max(16, (512 * 1024) // c // 16 * 16), 16)

    def body(x_ref, o_ref):
        o_ref[...] = x_ref[...].astype(BF16)

    return pl.pallas_call(
        body, name=name, grid=(a, r // tr),
        in_specs=[pl.BlockSpec((None, tr, c), lambda i, j: (i, j, 0))],
        out_specs=pl.BlockSpec((None, tr, c), lambda i, j: (i, j, 0)),
        out_shape=jax.ShapeDtypeStruct(x.shape, BF16),
        compiler_params=_params("parallel", "parallel"),
    )(x)


def _after_operands(after):
    return [] if after is None else [after]


def _after_specs(after):
    return [] if after is None else [pl.BlockSpec(memory_space=pl.ANY)]


def _matmul(a, b, *, ta=False, tb=False, out_dtype=F32, add=None, after=None, name, tm=1088, tn=1024, tk=2048):
    a_list = list(a) if isinstance(a, (list, tuple)) else None
    b_list = list(b) if isinstance(b, (list, tuple)) else None
    assert not (a_list and ta) and not (b_list and tb) and not (a_list and b_list)
    if a_list:
        m, k = a_list[0].shape[0], sum(p.shape[1] for p in a_list)
    else:
        m, k = (a.shape[1], a.shape[0]) if ta else a.shape
    if b_list:
        n = sum(p.shape[1] for p in b_list)
    else:
        n = b.shape[0] if tb else b.shape[1]
    tm = _tile(m, tm, 128 if ta else 16)
    tn = _tile(n, tn, 128)
    tk = _tile(k, tk, 128 if (not ta or tb) else 16)
    if a_list:
        while any(p.shape[1] % tk for p in a_list):
            tk //= 2
    if b_list:
        while any(p.shape[1] % tn for p in b_list):
            tn //= 2
    nk = k // tk

    def ranges(pieces, tile):
        out, start = [], 0
        for p in pieces:
            out.append((start, start + p.shape[1] // tile))
            start = out[-1][1]
        return out

    if a_list:
        a_ranges = ranges(a_list, tk)
        a_specs = [pl.BlockSpec((tm, tk), lambda i, j, kk, s=s, e=e: (i, jnp.clip(kk - s, 0, e - s - 1)))
                   for s, e in a_ranges]
    else:
        a_specs = [pl.BlockSpec((tk, tm), lambda i, j, kk: (kk, i)) if ta
                   else pl.BlockSpec((tm, tk), lambda i, j, kk: (i, kk))]
    if b_list:
        b_ranges = ranges(b_list, tn)
        b_specs = [pl.BlockSpec((tk, tn), lambda i, j, kk, s=s, e=e: (
            jnp.where(jnp.logical_and(j >= s, j < e), kk, 0), jnp.clip(j - s, 0, e - s - 1))) for s, e in b_ranges]
    else:
        b_specs = [pl.BlockSpec((tn, tk), lambda i, j, kk: (j, kk)) if tb
                   else pl.BlockSpec((tk, tn), lambda i, j, kk: (kk, j))]
    o_spec = pl.BlockSpec((tm, tn), lambda i, j, kk: (i, j))
    has_add = add is not None
    na, nb = len(a_specs), len(b_specs)
    simple = nk == 1 and not a_list and not b_list

    def body(*refs):
        a_refs, b_refs = refs[:na], refs[na:na + nb]
        add_ref = refs[na + nb] if has_add else None
        o_ref, acc_ref = refs[-2], refs[-1]
        j, kk = pl.program_id(1), pl.program_id(2)

        def finish(total):
            if has_add:
                total = total + add_ref[...]
            o_ref[...] = total.astype(out_dtype)

        if simple:
            finish(_dot(a_refs[0][...], b_refs[0][...], ta, tb))
            return

        @pl.when(kk == 0)
        def _():
            acc_ref[...] = jnp.zeros_like(acc_ref)

        if a_list:
            for (s, e), a_ref in zip(a_ranges, a_refs):
                @pl.when(jnp.logical_and(kk >= s, kk < e))
                def _(a_ref=a_ref):
                    acc_ref[...] += _dot(a_ref[...], b_refs[0][...], ta, tb)
        elif b_list:
            for (s, e), b_ref in zip(b_ranges, b_refs):
                @pl.when(jnp.logical_and(j >= s, j < e))
                def _(b_ref=b_ref):
                    acc_ref[...] += _dot(a_refs[0][...], b_ref[...], ta, tb)
        else:
            acc_ref[...] += _dot(a_refs[0][...], b_refs[0][...], ta, tb)

        @pl.when(kk == nk - 1)
        def _():
            finish(acc_ref[...])

    operands = (a_list or [a]) + (b_list or [b]) + ([add] if has_add else []) + _after_operands(after)
    in_specs = a_specs + b_specs + ([o_spec] if has_add else []) + _after_specs(after)
    return pl.pallas_call(
        body, name=name, grid=(m // tm, n // tn, nk),
        in_specs=in_specs, out_specs=o_spec,
        out_shape=jax.ShapeDtypeStruct((m, n), out_dtype),
        scratch_shapes=[pltpu.VMEM((8, 128) if simple else (tm, tn), F32)],
        compiler_params=_params("parallel", "parallel", "arbitrary"),
    )(*operands)


def _rms_fwd(x, gain, name, after=None):
    l, d = x.shape
    tr = _tile(l, 272, 16)

    def body(x_ref, g_ref, *rest):
        o_ref = rest[-1]
        xv = x_ref[...]
        r = lax.rsqrt(jnp.mean(xv * xv, axis=-1, keepdims=True) + RMS_EPS)
        o_ref[...] = (xv * r * g_ref[...]).astype(BF16)

    return pl.pallas_call(
        body, name=name, grid=(l // tr,),
        in_specs=[pl.BlockSpec((tr, d), lambda i: (i, 0)), pl.BlockSpec((1, d), lambda i: (0, 0))] + _after_specs(after),
        out_specs=pl.BlockSpec((tr, d), lambda i: (i, 0)),
        out_shape=jax.ShapeDtypeStruct((l, d), BF16),
        compiler_params=_params("parallel"),
    )(x, gain, *_after_operands(after))


def _rms_bwd(dh, x, gain, gout, name):
    l, d = x.shape
    tr = _tile(l, 272, 16)

    def body(dh_ref, x_ref, g_ref, go_ref, gx_ref, gxb_ref, dg_ref):
        xv = x_ref[...]
        r = lax.rsqrt(jnp.mean(xv * xv, axis=-1, keepdims=True) + RMS_EPS)
        nv = xv * r
        dhv = dh_ref[...]
        dn = dhv * g_ref[...]
        dx = r * (dn - nv * jnp.mean(dn * nv, axis=-1, keepdims=True))
        gx = go_ref[...] + dx
        gx_ref[...] = gx
        gxb_ref[...] = gx.astype(BF16)
        part = jnp.sum(dhv * nv, axis=0, keepdims=True)

        @pl.when(pl.program_id(0) == 0)
        def _():
            dg_ref[...] = part

        @pl.when(pl.program_id(0) > 0)
        def _():
            dg_ref[...] += part

    row = pl.BlockSpec((tr, d), lambda i: (i, 0))
    vec = pl.BlockSpec((1, d), lambda i: (0, 0))
    return pl.pallas_call(
        body, name=name, grid=(l // tr,),
        in_specs=[row, row, vec, row], out_specs=[row, row, vec],
        out_shape=[jax.ShapeDtypeStruct((l, d), F32), jax.ShapeDtypeStruct((l, d), BF16),
                   jax.ShapeDtypeStruct((1, d), F32)],
        compiler_params=_params("arbitrary"),
    )(dh, x, gain, gout)


def _final_loss(x, gain, target, row_lo, row_hi, name):
    l, d = x.shape
    tr = _tile(l, 272, 16)

    def body(x_ref, g_ref, t_ref, gx_ref, gxb_ref, dg_ref, loss_ref):
        i = pl.program_id(0)
        xv = x_ref[...]
        r = lax.rsqrt(jnp.mean(xv * xv, axis=-1, keepdims=True) + RMS_EPS)
        nv = xv * r
        gv = g_ref[...]
        rows = i * tr + lax.broadcasted_iota(jnp.int32, (tr, 1), 0)
        valid = jnp.logical_and(rows >= row_lo, rows < row_hi)
        err = jnp.where(valid, nv * gv - t_ref[...], 0.0)
        dy = err * (1.0 / d)
        dn = dy * gv
        gx = r * (dn - nv * jnp.mean(dn * nv, axis=-1, keepdims=True))
        gx_ref[...] = gx
        gxb_ref[...] = gx.astype(BF16)
        part = jnp.sum(dy * nv, axis=0, keepdims=True)
        lpart = jnp.full((1, 128), 0.5 * jnp.sum(jnp.mean(err * err, axis=-1, keepdims=True)), F32)

        @pl.when(i == 0)
        def _():
            dg_ref[...] = part
            loss_ref[...] = lpart

        @pl.when(i > 0)
        def _():
            dg_ref[...] += part
            loss_ref[...] += lpart

    row = pl.BlockSpec((tr, d), lambda i: (i, 0))
    vec = pl.BlockSpec((1, d), lambda i: (0, 0))
    return pl.pallas_call(
        body, name=name, grid=(l // tr,),
        in_specs=[row, vec, row], out_specs=[row, row, vec, pl.BlockSpec((1, 128), lambda i: (0, 0))],
        out_shape=[jax.ShapeDtypeStruct((l, d), F32), jax.ShapeDtypeStruct((l, d), BF16),
                   jax.ShapeDtypeStruct((1, d), F32), jax.ShapeDtypeStruct((1, 128), F32)],
        compiler_params=_params("arbitrary"),
    )(x, gain, target)


def _log1m_beta(z):
    return -(jnp.maximum(z, 0.0) + jnp.log(1.0 + jnp.exp(-jnp.abs(z))))


def _tri(n, relation):
    r = lax.broadcasted_iota(jnp.int32, (n, n), 0)
    c = lax.broadcasted_iota(jnp.int32, (n, n), 1)
    return jnp.where(relation(r, c), 1.0, 0.0).astype(BF16)


def _head(hh):
    return slice(hh * HEAD_DIM, (hh + 1) * HEAD_DIM)


def _heads(x, hb):
    return jnp.stack([x[:, _head(hh)] for hh in range(hb)], axis=0)


def _bdot(a, b, ta=False, tb=False):
    dims = (((1 if ta else 2,), (2 if tb else 1,)), ((0,), (0,)))
    return lax.dot_general(a.astype(BF16), b.astype(BF16), dims, preferred_element_type=F32)


def _attn_specs(l, hb, n_heads):
    width = hb * HEAD_DIM
    groups = n_heads // hb

    def tile(section):
        return pl.BlockSpec((ATT_BLOCK, width), lambda h, i: (i, section * groups + h))

    def slab(section):
        return pl.BlockSpec((l, width), lambda h, i: (0, section * groups + h))

    return tile, slab


def _attn_fwd(zin, n_heads, name):
    l = zin.shape[0]
    t = ATT_BLOCK
    hb = ATT_HEADS if n_heads % ATT_HEADS == 0 else 1
    scale = HEAD_DIM ** -0.5

    def body(q_ref, k_ref, v_ref, g_ref, o_ref, oa_ref, tot_ref):
        i = pl.program_id(1)
        after = _tri(t, lambda r, c: r > c)
        causal = (lax.broadcasted_iota(jnp.int32, (t, t), 1) < lax.broadcasted_iota(jnp.int32, (t, t), 0))[None]
        q = _heads(q_ref[...], hb).astype(BF16)

        def tile(k0, run, acc, diagonal):
            z = _bdot(q, _heads(k_ref[pl.ds(k0, t), :], hb), tb=True) * scale
            lb_all = _log1m_beta(z)
            lb = jnp.where(causal, lb_all, 0.0) if diagonal else lb_all
            between = _dot_split(lb.reshape(hb * t, t), after).reshape(hb, t, t)
            a = jnp.exp(z + lb_all + between + run)
            if diagonal:
                a = jnp.where(causal, a, 0.0)
            return (run + jnp.sum(lb, axis=2, keepdims=True),
                    acc + _bdot(a, _heads(v_ref[pl.ds(k0, t), :], hb)))

        carry = tile(pl.multiple_of(i * t, t), jnp.zeros((hb, t, 1), F32), jnp.zeros((hb, t, HEAD_DIM), F32), True)
        run, o = lax.fori_loop(0, i, lambda it, c: tile(pl.multiple_of((i - 1 - it) * t, t), c[0], c[1], False), carry)
        for hh in range(hb):
            gate = g_ref[:, _head(hh)]
            o_ref[:, _head(hh)] = o[hh]
            oa_ref[:, _head(hh)] = (o[hh] * (gate * _sigmoid(gate))).astype(BF16)
            tot_ref[:, _head(hh)] = jnp.broadcast_to(run[hh], (t, HEAD_DIM))

    tile_spec, slab_spec = _attn_specs(l, hb, n_heads)
    width = n_heads * HEAD_DIM
    return pl.pallas_call(
        body, name=name, grid=(n_heads // hb, l // t),
        in_specs=[tile_spec(0), slab_spec(1), slab_spec(2), tile_spec(3)],
        out_specs=[tile_spec(0), tile_spec(0), tile_spec(0)],
        out_shape=[jax.ShapeDtypeStruct((l, width), F32), jax.ShapeDtypeStruct((l, width), BF16),
                   jax.ShapeDtypeStruct((l, width), F32)],
        compiler_params=_params("parallel", "arbitrary"),
    )(zin, zin, zin, zin)


def _attn_bwd(zin, o, tot, doa, n_heads, name, after=None):
    l = zin.shape[0]
    t = ATT_BLOCK
    nq = l // t
    hb = ATT_HEADS if n_heads % ATT_HEADS == 0 else 1
    scale = HEAD_DIM ** -0.5

    def body(q_ref, k_ref, v_ref, g_ref, o_ref, tot_ref, doa_ref, *rest):
        dq_ref, dk_ref, dv_ref, dg_ref, dk_acc, dv_acc = rest[-6:]
        i = pl.program_id(1)

        @pl.when(i == 0)
        def _():
            dk_acc[...] = jnp.zeros_like(dk_acc)
            dv_acc[...] = jnp.zeros_like(dv_acc)

        upto = _tri(t, lambda r, c: r <= c)
        before = _tri(t, lambda r, c: r < c)
        causal = (lax.broadcasted_iota(jnp.int32, (t, t), 1) < lax.broadcasted_iota(jnp.int32, (t, t), 0))[None]
        gate = g_ref[...]
        sg = _sigmoid(gate)
        doav = doa_ref[...]
        dg_ref[...] = (doav * o_ref[...] * (sg * (1.0 + gate * (1.0 - sg)))).astype(BF16)
        do = _heads(doav * (gate * sg), hb).astype(BF16)
        q = _heads(q_ref[...], hb).astype(BF16)
        total = _heads(tot_ref[...], hb)[:, :, 0:1]

        def tile(k0, run, pre, dq, diagonal):
            kb = _heads(k_ref[pl.ds(k0, t), :], hb).astype(BF16)
            z = _bdot(q, kb, tb=True) * scale
            lb_all = _log1m_beta(z)
            lb = jnp.where(causal, lb_all, 0.0) if diagonal else lb_all
            beta = jnp.exp(z + lb_all)
            a = beta * jnp.exp(total - run - _dot_split(lb.reshape(hb * t, t), upto).reshape(hb, t, t))
            if diagonal:
                a = jnp.where(causal, a, 0.0)
            e = a * _bdot(do, _heads(v_ref[pl.ds(k0, t), :], hb), tb=True)
            prefix = pre + _dot(e.reshape(hb * t, t), before).reshape(hb, t, t)
            dz = (e * (1.0 - beta) - beta * prefix) * scale
            if diagonal:
                dz = jnp.where(causal, dz, 0.0)
            dv = _bdot(a, do, ta=True)
            dk = _bdot(dz, q, ta=True)
            for hh in range(hb):
                dv_acc[pl.ds(k0, t), _head(hh)] += dv[hh]
                dk_acc[pl.ds(k0, t), _head(hh)] += dk[hh]
            return (run + jnp.sum(lb, axis=2, keepdims=True), pre + jnp.sum(e, axis=2, keepdims=True),
                    dq + _bdot(dz, kb))

        zero = jnp.zeros((hb, t, 1), F32)
        carry = lax.fori_loop(0, i, lambda j, c: tile(pl.multiple_of(j * t, t), c[0], c[1], c[2], False),
                              (zero, zero, jnp.zeros((hb, t, HEAD_DIM), F32)))
        _, _, dq = tile(pl.multiple_of(i * t, t), carry[0], carry[1], carry[2], True)
        for hh in range(hb):
            dq_ref[:, _head(hh)] = dq[hh].astype(BF16)

        @pl.when(i == nq - 1)
        def _():
            dk_ref[...] = dk_acc[...].astype(BF16)
            dv_ref[...] = dv_acc[...].astype(BF16)

    tile_spec, slab_spec = _attn_specs(l, hb, n_heads)
    out = jax.ShapeDtypeStruct((l, n_heads * HEAD_DIM), BF16)
    return pl.pallas_call(
        body, name=name, grid=(n_heads // hb, nq),
        in_specs=[tile_spec(0), slab_spec(1), slab_spec(2), tile_spec(3), tile_spec(0), tile_spec(0), tile_spec(0)]
        + _after_specs(after),
        out_specs=[tile_spec(0), slab_spec(0), slab_spec(0), tile_spec(0)],
        out_shape=[out, out, out, out],
        scratch_shapes=[pltpu.VMEM((l, hb * HEAD_DIM), F32), pltpu.VMEM((l, hb * HEAD_DIM), F32)],
        compiler_params=_params("parallel", "arbitrary"),
    )(zin, zin, zin, zin, o, tot, doa, *_after_operands(after))


def _shift_rows(x, k, down):
    n = x.shape[0]
    rows = lax.broadcasted_iota(jnp.int32, x.shape, 0)
    if down:
        return jnp.where(rows >= k, pltpu.roll(x, k, 0), 0.0)
    return jnp.where(rows < n - k, pltpu.roll(x, n - k, 0), 0.0)


def _window_sum(x, g, down):
    result = x
    total = x
    for step, k in enumerate((1, 2, 4, 8)):
        total = total + _shift_rows(total, k, down)
        result = jnp.where(g >= step, total, result)
    return result


def _pooled(u, g):
    rows = lax.broadcasted_iota(jnp.int32, (u.shape[0], 1), 0)
    window = jnp.left_shift(2, g)
    cnt = jnp.minimum(rows + 1, window).astype(F32)
    return _window_sum(u, g, True) / cnt - u, cnt


def _pool_fwd(zin, pool_w, pool_scale, u_off, name, after=None):
    l = zin.shape[0]
    n_groups, gd, _ = pool_w.shape

    def body(u_ref, g_ref, w_ref, s_ref, *rest):
        o_ref = rest[-1]
        g = pl.program_id(0)
        pooled, _ = _pooled(u_ref[...], g)
        mixed = _dot(pooled, w_ref[...])
        gate = g_ref[...]
        o_ref[...] = (mixed * s_ref[...] * (gate * _sigmoid(gate))).astype(BF16)

    return pl.pallas_call(
        body, name=name, grid=(n_groups,),
        in_specs=[pl.BlockSpec((l, gd), lambda g: (0, u_off + g)),
                  pl.BlockSpec((l, gd), lambda g: (0, u_off + n_groups + g)),
                  pl.BlockSpec((None, gd, gd), lambda g: (g, 0, 0)),
                  pl.BlockSpec((1, gd), lambda g: (0, g))] + _after_specs(after),
        out_specs=pl.BlockSpec((l, gd), lambda g: (0, g)),
        out_shape=jax.ShapeDtypeStruct((l, n_groups * gd), BF16),
        compiler_params=_params("parallel"),
    )(zin, zin, pool_w, pool_scale, *_after_operands(after))


def _pool_bwd(zin, dop, pool_w, pool_scale, u_off, name):
    l = zin.shape[0]
    n_groups, gd, _ = pool_w.shape

    def body(u_ref, g_ref, w_ref, s_ref, d_ref, du_ref, dg_ref, dw_ref, ds_ref):
        g = pl.program_id(0)
        pooled, cnt = _pooled(u_ref[...], g)
        w = w_ref[...]
        mixed = _dot(pooled, w)
        gate = g_ref[...]
        sg = _sigmoid(gate)
        silu = gate * sg
        dop_v = d_ref[...]
        sc = s_ref[...]
        ds_ref[...] = jnp.sum(dop_v * mixed * silu, axis=0, keepdims=True)
        dg_ref[...] = (dop_v * mixed * sc * (sg * (1.0 + gate * (1.0 - sg)))).astype(BF16)
        dmixed = dop_v * sc * silu
        dw_ref[...] = _dot(pooled, dmixed, ta=True).astype(BF16)
        dpooled = _dot(dmixed, w, tb=True)
        du_ref[...] = (_window_sum(dpooled / cnt, g, False) - dpooled).astype(BF16)

    slab = pl.BlockSpec((l, gd), lambda g: (0, g))
    return pl.pallas_call(
        body, name=name, grid=(n_groups,),
        in_specs=[pl.BlockSpec((l, gd), lambda g: (0, u_off + g)),
                  pl.BlockSpec((l, gd), lambda g: (0, u_off + n_groups + g)),
                  pl.BlockSpec((None, gd, gd), lambda g: (g, 0, 0)),
                  pl.BlockSpec((1, gd), lambda g: (0, g)), slab],
        out_specs=[slab, slab, pl.BlockSpec((None, gd, gd), lambda g: (g, 0, 0)),
                   pl.BlockSpec((1, gd), lambda g: (0, g))],
        out_shape=[jax.ShapeDtypeStruct((l, n_groups * gd), BF16), jax.ShapeDtypeStruct((l, n_groups * gd), BF16),
                   jax.ShapeDtypeStruct(pool_w.shape, BF16), jax.ShapeDtypeStruct((1, n_groups * gd), F32)],
        compiler_params=_params("parallel"),
    )(zin, zin, pool_w, pool_scale, dop)


def _merge_fwd(oa, op, w_au, w_pu, zin, name):
    l, wa = oa.shape
    wp = op.shape[1]
    d = w_au.shape[1]
    tm = _tile(l, 544, 16)
    tn = _tile(d, 512, 128)
    ma_off = (zin.shape[1] - 2 * d) // tn

    def body(oa_ref, op_ref, wa_ref, wp_ref, ma_ref, mp_ref, ya_ref, yp_ref, mg_ref):
        ya = _dot(oa_ref[...], wa_ref[...])
        yp = _dot(op_ref[...], wp_ref[...])
        ya_ref[...] = ya
        yp_ref[...] = yp
        mg_ref[...] = (_sigmoid(ma_ref[...]) * ya + _sigmoid(mp_ref[...]) * yp).astype(BF16)

    tile = pl.BlockSpec((tm, tn), lambda i, j: (i, j))
    return pl.pallas_call(
        body, name=name, grid=(l // tm, d // tn),
        in_specs=[pl.BlockSpec((tm, wa), lambda i, j: (i, 0)), pl.BlockSpec((tm, wp), lambda i, j: (i, 0)),
                  pl.BlockSpec((wa, tn), lambda i, j: (0, j)), pl.BlockSpec((wp, tn), lambda i, j: (0, j)),
                  pl.BlockSpec((tm, tn), lambda i, j: (i, ma_off + j)),
                  pl.BlockSpec((tm, tn), lambda i, j: (i, ma_off + d // tn + j))],
        out_specs=[tile, tile, tile],
        out_shape=[jax.ShapeDtypeStruct((l, d), F32), jax.ShapeDtypeStruct((l, d), F32),
                   jax.ShapeDtypeStruct((l, d), BF16)],
        compiler_params=_params("parallel", "parallel"),
    )(oa, op, w_au, w_pu, zin, zin)


def _merge_bwd(gout, w_out, zin, ya, yp, name):
    l, d = gout.shape
    tm = _tile(l, 544, 16)
    tn = _tile(d, 512, 128)
    ma_off = (zin.shape[1] - 2 * d) // tn

    def body(g_ref, w_ref, ma_ref, mp_ref, ya_ref, yp_ref, dya_ref, dyp_ref, dma_ref, dmp_ref):
        dm = _dot(g_ref[...], w_ref[...], tb=True)
        sa = _sigmoid(ma_ref[...])
        sp = _sigmoid(mp_ref[...])
        dya_ref[...] = (dm * sa).astype(BF16)
        dyp_ref[...] = (dm * sp).astype(BF16)
        dma_ref[...] = (dm * ya_ref[...] * (sa * (1.0 - sa))).astype(BF16)
        dmp_ref[...] = (dm * yp_ref[...] * (sp * (1.0 - sp))).astype(BF16)

    tile = pl.BlockSpec((tm, tn), lambda i, j: (i, j))
    out = jax.ShapeDtypeStruct((l, d), BF16)
    return pl.pallas_call(
        body, name=name, grid=(l // tm, d // tn),
        in_specs=[pl.BlockSpec((tm, d), lambda i, j: (i, 0)), pl.BlockSpec((tn, d), lambda i, j: (j, 0)),
                  pl.BlockSpec((tm, tn), lambda i, j: (i, ma_off + j)),
                  pl.BlockSpec((tm, tn), lambda i, j: (i, ma_off + d // tn + j)), tile, tile],
        out_specs=[tile, tile, tile, tile],
        out_shape=[out, out, out, out],
        compiler_params=_params("parallel", "parallel"),
    )(gout, w_out, zin, zin, ya, yp)


def _adamw(parts, w, m, v, name):
    n_arrays, n_parts, r, c = parts.shape
    tr = _tile(r, max(8, (128 * 1024) // c // 8 * 8), 8)
    bias1 = 1.0 - ADAM_B1 ** ADAM_STEP
    bias2 = 1.0 - ADAM_B2 ** ADAM_STEP

    def body(p_ref, w_ref, m_ref, v_ref, g_ref, d_ref, nm_ref, nv_ref):
        g = p_ref[0].astype(F32)
        for j in range(1, n_parts):
            g = g + p_ref[j].astype(F32)
        nm = ADAM_B1 * m_ref[...] + (1.0 - ADAM_B1) * g
        nv = ADAM_B2 * v_ref[...] + (1.0 - ADAM_B2) * (g * g)
        g_ref[...] = g
        nm_ref[...] = nm
        nv_ref[...] = nv
        d_ref[...] = -ADAM_LR * ((nm / bias1) / (jnp.sqrt(nv / bias2) + ADAM_EPS) + ADAM_WD * w_ref[...])

    tile = pl.BlockSpec((None, tr, c), lambda a, i: (a, i, 0))
    out = jax.ShapeDtypeStruct((n_arrays, r, c), F32)
    return pl.pallas_call(
        body, name=name, grid=(n_arrays, r // tr),
        in_specs=[pl.BlockSpec((None, n_parts, tr, c), lambda a, i: (a, 0, i, 0)), tile, tile, tile],
        out_specs=[tile, tile, tile, tile], out_shape=[out, out, out, out],
        compiler_params=_params("parallel", "parallel"),
    )(parts, w, m, v)


def _position():
    return lax.axis_index("x"), lax.axis_index("y"), lax.axis_index("c")


def _block_of(ref, axis, size, index):
    idx = [slice(None)] * len(ref.shape)
    idx[axis] = pl.ds(index * size, size)
    return ref.at[tuple(idx)]


HBM_SPEC = pl.BlockSpec(memory_space=pltpu.HBM)
SEM_SPEC = pl.BlockSpec(memory_space=pltpu.SEMAPHORE)
SIDE_EFFECT = pltpu.CompilerParams(has_side_effects=pltpu.SideEffectType.DATAFLOW_SIDE_EFFECTING)


def _split_start(make_copies, n_copies, buffers, name):
    n = len(buffers)

    def body(*refs):
        send_sems, recv_sems = refs[n], refs[n + 1]
        for cp in make_copies(refs[:n], send_sems, recv_sems):
            cp.start()
        refs[-1][...] = jnp.zeros_like(refs[-1])

    sems = pltpu.SemaphoreType.DMA((n_copies,))
    return pl.pallas_call(
        body, name=name, in_specs=[HBM_SPEC] * n,
        out_shape=(sems, sems, *[pltpu.HBM(b.shape, b.dtype) for b in buffers], jax.ShapeDtypeStruct((8, 128), F32)),
        out_specs=(SEM_SPEC, SEM_SPEC, *[HBM_SPEC] * n, pl.BlockSpec(memory_space=pltpu.VMEM)),
        input_output_aliases={i: 2 + i for i in range(n)}, compiler_params=SIDE_EFFECT,
    )(*[pltpu.with_memory_space_constraint(b, pltpu.HBM) for b in buffers])


def _split_wait(make_copies, started, after, name):
    send_sems, recv_sems, *buffers = started[:-1]
    n = len(buffers)

    def body(*refs):
        copies = make_copies(refs[:n], refs[n], refs[n + 1])
        for cp in copies:
            cp.wait_send()
        for cp in copies:
            cp.wait_recv()

    return pl.pallas_call(
        body, name=name, in_specs=[HBM_SPEC] * n + [SEM_SPEC, SEM_SPEC, pl.BlockSpec(memory_space=pl.ANY)],
        out_shape=[pltpu.HBM(b.shape, b.dtype) for b in buffers], out_specs=[HBM_SPEC] * n,
        input_output_aliases={i: i for i in range(n)}, compiler_params=SIDE_EFFECT,
    )(*buffers, send_sems, recv_sems, after)


def _gather_copies(axes, sizes, level):
    def make(fulls, send_sems, recv_sems):
        x, y, c = _position()
        chips = [(1 - x, y), (x, 1 - y), (1 - x, 1 - y)]
        copies = []
        for a, full in enumerate(fulls):
            def copy(k, block, to, full=full, a=a):
                rows = _block_of(full, axes[a], sizes[a], 4 * block[0] + 2 * block[1] + block[2])
                idx = a * (4 if level == 1 else 3) + k
                return pltpu.make_async_remote_copy(src_ref=rows, dst_ref=rows, send_sem=send_sems.at[idx],
                                                    recv_sem=recv_sems.at[idx], device_id=to, device_id_type=MESH)
            if level == 1:
                copies.append(copy(0, (x, y, c), (x, y, 1 - c)))
                copies += [copy(1 + j, (x, y, c), (*chip, c)) for j, chip in enumerate(chips)]
            else:
                copies += [copy(j, (*chip, c), (x, y, 1 - c)) for j, chip in enumerate(chips)]
        return copies
    return make


def _exchange_copies(axes, sizes, layer, n_src):
    flips = [(a, b, d) for a in (0, 1) for b in (0, 1) for d in (0, 1)][1:]

    def make(buffers, send_sems, recv_sems):
        x, y, c = _position()
        my_index = 4 * x + 2 * y + c
        copies = []
        for a in range(n_src):
            for k, flip in enumerate(flips):
                px, py, pc = x ^ flip[0], y ^ flip[1], c ^ flip[2]
                copies.append(pltpu.make_async_remote_copy(
                    src_ref=_block_of(buffers[a], axes[a], sizes[a], 4 * px + 2 * py + pc),
                    dst_ref=buffers[n_src + a].at[layer, my_index],
                    send_sem=send_sems.at[a * 7 + k], recv_sem=recv_sems.at[a * 7 + k],
                    device_id=(px, py, pc), device_id_type=MESH))
        return copies
    return make


def _put_block(buffer, block, index, axis):
    start = [0] * buffer.ndim
    start[axis] = index * block.shape[axis]
    return lax.dynamic_update_slice(buffer, block, start)


def _allgather_small(v, name):
    r, c = v.shape
    flips = [(a, b, d) for a in (0, 1) for b in (0, 1) for d in (0, 1)][1:]

    def body(v_ref, out_ref, send_sems, recv_sems):
        x, y, c_ = _position()
        my_index = 4 * x + 2 * y + c_
        out_ref[my_index] = v_ref[...]
        sends = []
        for k, flip in enumerate(flips):
            peer = (x ^ flip[0], y ^ flip[1], c_ ^ flip[2])
            cp = pltpu.make_async_remote_copy(
                src_ref=v_ref, dst_ref=out_ref.at[my_index],
                send_sem=send_sems.at[k], recv_sem=recv_sems.at[k], device_id=peer, device_id_type=MESH)
            cp.start()
            sends.append(cp)
        for k, flip in enumerate(flips):
            px, py, pc = x ^ flip[0], y ^ flip[1], c_ ^ flip[2]
            pltpu.make_async_remote_copy(
                src_ref=v_ref, dst_ref=out_ref.at[4 * px + 2 * py + pc],
                send_sem=send_sems.at[k], recv_sem=recv_sems.at[k],
                device_id=(px, py, pc), device_id_type=MESH).wait_recv()
        for cp in sends:
            cp.wait_send()

    return pl.pallas_call(
        body, name=name,
        in_specs=[pl.BlockSpec(memory_space=pltpu.VMEM)], out_specs=pl.BlockSpec(memory_space=pltpu.VMEM),
        out_shape=jax.ShapeDtypeStruct((N_DEV, r, c), v.dtype),
        scratch_shapes=[pltpu.SemaphoreType.DMA((7,)), pltpu.SemaphoreType.DMA((7,))],
    )(v)


def kernel(x, meta_tokens, norm_gain, w_in, pool_w, pool_scale, w_attn_up, w_pool_up, w_out, final_gain, loss_target, m_meta_tokens, m_norm_gain, m_w_in, m_pool_w, m_pool_scale, m_w_attn_up, m_w_pool_up, m_w_out, m_final_gain, v_meta_tokens, v_norm_gain, v_w_in, v_pool_w, v_pool_scale, v_w_attn_up, v_w_pool_up, v_w_out, v_final_gain):
    _, seq, d = x.shape
    n_meta = meta_tokens.shape[0]
    depth = w_in.shape[0]
    sb_width = w_attn_up.shape[1]
    pool_width = w_pool_up.shape[1]
    n_heads = sb_width // HEAD_DIM
    n_groups = pool_w.shape[1]
    gd = pool_w.shape[3]
    assert n_groups == len(POOL_WINDOWS) and gd * n_groups == pool_width
    assert w_in.shape[2] * N_DEV == 4 * sb_width + 2 * pool_width + 2 * d
    l_real = n_meta + seq
    l_pad = -(-l_real // ATT_BLOCK) * ATT_BLOCK
    my_index = 4 * lax.axis_index("x") + 2 * lax.axis_index("y") + lax.axis_index("c")

    pool_w3 = pool_w.reshape(depth, n_groups * pool_w.shape[2], gd)
    w_in_b = _cast_bf16(w_in, "cast_w_in")
    pool_w_b = _cast_bf16(pool_w3, "cast_pool_w").reshape(pool_w.shape)
    w_au_b = _cast_bf16(w_attn_up, "cast_w_attn_up")
    w_pu_b = _cast_bf16(w_pool_up, "cast_w_pool_up")
    w_out_b = _cast_bf16(w_out, "cast_w_out")
    g_axes = [1, 1, 1, 1, 0]
    shards = [[w_in_b[i], pool_w_b[i], w_au_b[i], w_pu_b[i], w_out_b[i]] for i in range(depth)]
    g_sizes = [s.shape[ax] for s, ax in zip(shards[0], g_axes)]
    level1 = _gather_copies(g_axes, g_sizes, 1)
    level2 = _gather_copies(g_axes, g_sizes, 2)

    def gather_start(i):
        fulls = []
        for s, ax in zip(shards[i], g_axes):
            shape = list(s.shape)
            shape[ax] *= N_DEV
            fulls.append(_put_block(lax.empty(tuple(shape), BF16), s, my_index, ax))
        return _split_start(level1, 4 * len(fulls), fulls, "gather1_start_%d" % i)

    def gather_forward(i, started, after):
        arrived = _split_wait(level1, started, after, "gather1_wait_%d" % i)
        return _split_start(level2, 3 * len(arrived), arrived, "gather2_start_%d" % i)

    meta_all = _allgather_small(meta_tokens, "allgather_meta")
    meta_full = jnp.transpose(meta_all, (1, 0, 2)).reshape(n_meta, d)
    first = gather_start(0)
    second = gather_forward(0, first, first[-1])
    weights = [_split_wait(level2, second, second[-1], "gather2_wait_0")]

    pad_rows = l_pad - l_real
    hs = jnp.concatenate([meta_full, x[0], jnp.zeros((pad_rows, d), F32)], axis=0)
    target = jnp.concatenate([jnp.zeros((n_meta, d), F32), loss_target[0], jnp.zeros((pad_rows, d), F32)], axis=0)
    u_off = 4 * sb_width // gd
    saved = []
    for i in range(depth):
        wi, pw, wau, wpu, wo = weights[i]
        more = i + 1 < depth
        first = gather_start(i + 1) if more else None
        h = _rms_fwd(hs, norm_gain[i][None], "rms_fwd", after=first[-1] if more else None)
        zin = _matmul(h, wi, name="mm_zin")
        o, oa, tot = _attn_fwd(zin, n_heads, "attn_fwd")
        second = gather_forward(i + 1, first, oa) if more else None
        op = _pool_fwd(zin, pw, pool_scale[i][None], u_off, "pool_fwd", after=second[-1] if more else None)
        ya, yp, merged = _merge_fwd(oa, op, wau, wpu, zin, "merge_fwd")
        saved.append((hs, h, zin, o, tot, oa, op, ya, yp, merged))
        hs = _matmul(merged, wo, add=hs, name="mm_out")
        if more:
            weights.append(_split_wait(level2, second, hs, "gather2_wait_%d" % (i + 1)))
    g, gb, d_final_gain, loss_part = _final_loss(hs, final_gain[None], target, n_meta, l_real, "final_loss")
    loss = lax.psum(loss_part[0, 0], ("x", "y", "c"))

    d_norm_gain = [None] * depth
    d_pool_scale = [None] * depth
    axes_a, axes_b = [1, 1, 0], [1, 1]
    blocks_a, blocks_b = [w_attn_up[0], w_pool_up[0], w_out[0]], [w_in[0], pool_w[0]]
    sizes_a = [b.shape[ax] for b, ax in zip(blocks_a, axes_a)]
    sizes_b = [b.shape[ax] for b, ax in zip(blocks_b, axes_b)]
    land_a = [lax.empty((depth, N_DEV, *b.shape), BF16) for b in blocks_a]
    land_b = [lax.empty((depth, N_DEV, *b.shape), BF16) for b in blocks_b]

    def exchange_start(grads, landing, axes, sizes, layer, name):
        landing = [lax.dynamic_update_slice(
            zone, lax.dynamic_slice_in_dim(grad, my_index * size, size, ax)[None, None],
            (layer, my_index) + (0,) * grad.ndim) for zone, grad, ax, size in zip(landing, grads, axes, sizes)]
        copies = _exchange_copies(axes, sizes, layer, len(grads))
        return copies, _split_start(copies, 7 * len(grads), list(grads) + landing, name)

    def exchange_wait(pending, n_src, after, name):
        return _split_wait(pending[0], pending[1], after, name)[n_src:]

    pend_a = pend_b = None
    for i in reversed(range(depth)):
        wi, pw, wau, wpu, wo = weights[i]
        hs_in, h, zin, o, tot, oa, op, ya, yp, merged = saved[i]
        dya, dyp, dma, dmp = _merge_bwd(gb, wo, zin, ya, yp, "merge_bwd")
        dw_out = _matmul(merged, gb, ta=True, out_dtype=BF16, name="mm_dw_out", tm=1024, tk=1088)
        doa = _matmul(dya, wau, tb=True, name="mm_doa")
        dw_au = _matmul(oa, dya, ta=True, out_dtype=BF16, name="mm_dw_au", tm=1024, tk=1088)
        dop = _matmul(dyp, wpu, tb=True, name="mm_dop")
        dw_pu = _matmul(op, dyp, ta=True, out_dtype=BF16, name="mm_dw_pu", tm=1024, tk=1088)
        if pend_a is not None:
            land_a = exchange_wait(pend_a, 3, dw_pu, "exchange_a_wait_%d" % (i + 1))
        pend_a = exchange_start([dw_au, dw_pu, dw_out], land_a, axes_a, sizes_a, i, "exchange_a_start_%d" % i)
        dq, dk, dv, dga = _attn_bwd(zin, o, tot, doa, n_heads, "attn_bwd", after=pend_a[1][-1])
        du, dgp, dpw, dps = _pool_bwd(zin, dop, pw, pool_scale[i][None], u_off, "pool_bwd")
        dzin = [dq, dk, dv, dga, du, dgp, dma, dmp]
        dw_in = _matmul(h, dzin, ta=True, out_dtype=BF16, name="mm_dw_in", tm=1024, tk=1088)
        if pend_b is not None:
            land_b = exchange_wait(pend_b, 2, dw_in, "exchange_b_wait_%d" % (i + 1))
        pend_b = exchange_start([dw_in, dpw], land_b, axes_b, sizes_b, i, "exchange_b_start_%d" % i)
        dh = _matmul(dzin, wi, tb=True, after=pend_b[1][-1], name="mm_dh")
        g, gb, dng = _rms_bwd(dh, hs_in, norm_gain[i][None], g, "rms_bwd")
        d_norm_gain[i] = dng
        d_pool_scale[i] = dps
    land_a = exchange_wait(pend_a, 3, gb, "exchange_a_wait_0")
    land_b = exchange_wait(pend_b, 2, gb, "exchange_b_wait_0")
    grad_x = g[n_meta:l_real][None]

    zeros_ps = jnp.zeros((depth, d - pool_width), F32)
    small_rows = [jnp.concatenate(d_norm_gain, axis=0),
                  jnp.concatenate([jnp.concatenate(d_pool_scale, axis=0), zeros_ps], axis=1), d_final_gain]
    n_small = 2 * depth + 1
    small_pad = -(-n_small // 8) * 8
    small = jnp.concatenate(small_rows + [jnp.zeros((small_pad - n_small, d), F32), g[:n_meta]], axis=0)
    small_all = _allgather_small(small, "allgather_small")

    def replicated(rows, width, w, m, v, name):
        parts = lax.slice(small_all, (0, rows[0], 0), (N_DEV, rows[1], width))
        return [t[0] for t in _adamw(parts[None], w[None], m[None], v[None], name)]

    out_ng = replicated((0, depth), d, norm_gain, m_norm_gain, v_norm_gain, "adamw_norm_gain")
    out_ps = replicated((depth, 2 * depth), pool_width, pool_scale, m_pool_scale, v_pool_scale, "adamw_pool_scale")
    out_fg = [t[0] for t in replicated((2 * depth, 2 * depth + 1), d, final_gain[None], m_final_gain[None],
                                       v_final_gain[None], "adamw_final_gain")]
    cols = d // N_DEV
    meta_parts = lax.dynamic_slice(small_all, (0, small_pad, my_index * cols), (N_DEV, n_meta, cols))
    out_meta = [t[0] for t in _adamw(meta_parts[None], meta_tokens[None], m_meta_tokens[None], v_meta_tokens[None],
                                     "adamw_meta")]

    def sharded(parts, w, m, v, name):
        flat = (depth, -1, w.shape[-1])
        res = _adamw(parts.reshape(depth, N_DEV, -1, w.shape[-1]), w.reshape(flat), m.reshape(flat), v.reshape(flat),
                     name)
        return [t.reshape(w.shape) for t in res]

    out_wi = sharded(land_b[0], w_in, m_w_in, v_w_in, "adamw_w_in")
    out_pw = sharded(land_b[1], pool_w, m_pool_w, v_pool_w, "adamw_pool_w")
    out_au = sharded(land_a[0], w_attn_up, m_w_attn_up, v_w_attn_up, "adamw_w_attn_up")
    out_pu = sharded(land_a[1], w_pool_up, m_w_pool_up, v_w_pool_up, "adamw_w_pool_up")
    out_wo = sharded(land_a[2], w_out, m_w_out, v_w_out, "adamw_w_out")

    by_weight = [out_meta, out_ng, out_wi, out_pw, out_ps, out_au, out_pu, out_wo, out_fg]
    return (loss, grad_x, *[o[0] for o in by_weight], *[o[1] for o in by_weight],
            *[o[2] for o in by_weight], *[o[3] for o in by_weight])
```

```python
import jax
import jax.numpy as jnp
from jax import lax
from jax.experimental import pallas as pl
from jax.experimental.pallas import tpu as pltpu

F32 = jnp.float32
BF16 = jnp.bfloat16
MESH = pl.DeviceIdType.MESH

N_DEV = 8
HEAD_DIM = 128
ATT_BLOCK = 128
ATT_HEADS = 4
POOL_WINDOWS = (2, 4, 8, 16)
RMS_EPS = 1e-6
ADAM_LR, ADAM_B1, ADAM_B2, ADAM_EPS, ADAM_WD, ADAM_STEP = 0.001, 0.9, 0.999, 1e-08, 0.01, 10
VMEM_LIMIT_BYTES = 56 * 1024 * 1024


def _tile(n, target, mult):
    if n <= target:
        return n
    best = 0
    for t in range(mult, target + 1, mult):
        if n % t == 0:
            best = t
    assert best > 0, (n, target, mult)
    return best


def _params(*sem):
    return pltpu.CompilerParams(dimension_semantics=sem, vmem_limit_bytes=VMEM_LIMIT_BYTES)


def _sigmoid(x):
    return 1.0 / (1.0 + jnp.exp(-x))


def _dot(a, b, ta=False, tb=False):
    dims = (((0 if ta else 1,), (1 if tb else 0,)), ((), ()))
    return lax.dot_general(a.astype(BF16), b.astype(BF16), dims, preferred_element_type=F32)


def _dot_split(a, b):
    hi = a.astype(BF16)
    lo = (a - hi.astype(F32)).astype(BF16)
    return _dot(hi, b) + _dot(lo, b)


def _cast_place(x, layer, axis, me, name, after=None):
    blk = x.shape[1:]
    full = list(blk)
    full[axis] *= N_DEV
    if len(blk) == 2:
        r, c = blk
        tr = _tile(r, max(16, (512 * 1024) // c // 16 * 16), 16)
        steps = r // tr
        in_spec = pl.BlockSpec((None, tr, c), lambda i, me_ref: (layer, i, 0))
        if axis == 1:
            out_spec = pl.BlockSpec((tr, c), lambda i, me_ref: (i, me_ref[0]))
        else:
            out_spec = pl.BlockSpec((tr, c), lambda i, me_ref: (me_ref[0] * steps + i, 0))
    else:
        assert len(blk) == 3 and axis == 1
        steps = 1
        in_spec = pl.BlockSpec((None, *blk), lambda i, me_ref: (layer, 0, 0, 0))
        out_spec = pl.BlockSpec(blk, lambda i, me_ref: (0, me_ref[0], 0))

    def body(me_ref, x_ref, *rest):
        rest[-1][...] = x_ref[...].astype(BF16)

    return pl.pallas_call(
        body, name=name,
        grid_spec=pltpu.PrefetchScalarGridSpec(num_scalar_prefetch=1, grid=(steps,),
                                               in_specs=[in_spec] + _after_specs(after), out_specs=out_spec),
        out_shape=jax.ShapeDtypeStruct(tuple(full), BF16),
        compiler_params=_params("arbitrary"),
    )(me, x, *_after_operands(after))


def _after_operands(after):
    return [] if after is None else [after]


def _after_specs(after):
    return [] if after is None else [pl.BlockSpec(memory_space=pl.ANY)]


def _matmul(a, b, *, ta=False, tb=False, out_dtype=F32, add=None, after=None, name, tm=1088, tn=1024, tk=2048):
    a_list = list(a) if isinstance(a, (list, tuple)) else None
    b_list = list(b) if isinstance(b, (list, tuple)) else None
    assert not (a_list and ta) and not (b_list and tb) and not (a_list and b_list)
    if a_list:
        m, k = a_list[0].shape[0], sum(p.shape[1] for p in a_list)
    else:
        m, k = (a.shape[1], a.shape[0]) if ta else a.shape
    if b_list:
        n = sum(p.shape[1] for p in b_list)
    else:
        n = b.shape[0] if tb else b.shape[1]
    tm = _tile(m, tm, 128 if ta else 16)
    tn = _tile(n, tn, 128)
    tk = _tile(k, tk, 128 if (not ta or tb) else 16)
    if a_list:
        while any(p.shape[1] % tk for p in a_list):
            tk //= 2
    if b_list:
        while any(p.shape[1] % tn for p in b_list):
            tn //= 2
    nk = k // tk

    def ranges(pieces, tile):
        out, start = [], 0
        for p in pieces:
            out.append((start, start + p.shape[1] // tile))
            start = out[-1][1]
        return out

    if a_list:
        a_ranges = ranges(a_list, tk)
        a_specs = [pl.BlockSpec((tm, tk), lambda i, j, kk, s=s, e=e: (i, jnp.clip(kk - s, 0, e - s - 1)))
                   for s, e in a_ranges]
    else:
        a_specs = [pl.BlockSpec((tk, tm), lambda i, j, kk: (kk, i)) if ta
                   else pl.BlockSpec((tm, tk), lambda i, j, kk: (i, kk))]
    if b_list:
        b_ranges = ranges(b_list, tn)
        b_specs = [pl.BlockSpec((tk, tn), lambda i, j, kk, s=s, e=e: (
            jnp.where(jnp.logical_and(j >= s, j < e), kk, 0), jnp.clip(j - s, 0, e - s - 1))) for s, e in b_ranges]
    else:
        b_specs = [pl.BlockSpec((tn, tk), lambda i, j, kk: (j, kk)) if tb
                   else pl.BlockSpec((tk, tn), lambda i, j, kk: (kk, j))]
    o_spec = pl.BlockSpec((tm, tn), lambda i, j, kk: (i, j))
    has_add = add is not None
    na, nb = len(a_specs), len(b_specs)
    simple = nk == 1 and not a_list and not b_list

    def body(*refs):
        a_refs, b_refs = refs[:na], refs[na:na + nb]
        add_ref = refs[na + nb] if has_add else None
        o_ref, acc_ref = refs[-2], refs[-1]
        j, kk = pl.program_id(1), pl.program_id(2)

        def finish(total):
            if has_add:
                total = total + add_ref[...]
            o_ref[...] = total.astype(out_dtype)

        if simple:
            finish(_dot(a_refs[0][...], b_refs[0][...], ta, tb))
            return

        @pl.when(kk == 0)
        def _():
            acc_ref[...] = jnp.zeros_like(acc_ref)

        if a_list:
            for (s, e), a_ref in zip(a_ranges, a_refs):
                @pl.when(jnp.logical_and(kk >= s, kk < e))
                def _(a_ref=a_ref):
                    acc_ref[...] += _dot(a_ref[...], b_refs[0][...], ta, tb)
        elif b_list:
            for (s, e), b_ref in zip(b_ranges, b_refs):
                @pl.when(jnp.logical_and(j >= s, j < e))
                def _(b_ref=b_ref):
                    acc_ref[...] += _dot(a_refs[0][...], b_ref[...], ta, tb)
        else:
            acc_ref[...] += _dot(a_refs[0][...], b_refs[0][...], ta, tb)

        @pl.when(kk == nk - 1)
        def _():
            finish(acc_ref[...])

    operands = (a_list or [a]) + (b_list or [b]) + ([add] if has_add else []) + _after_operands(after)
    in_specs = a_specs + b_specs + ([o_spec] if has_add else []) + _after_specs(after)
    return pl.pallas_call(
        body, name=name, grid=(m // tm, n // tn, nk),
        in_specs=in_specs, out_specs=o_spec,
        out_shape=jax.ShapeDtypeStruct((m, n), out_dtype),
        scratch_shapes=[pltpu.VMEM((8, 128) if simple else (tm, tn), F32)],
        compiler_params=_params("parallel", "parallel", "arbitrary"),
    )(*operands)


def _rms_fwd(x, gain, name, after=None):
    l, d = x.shape
    tr = _tile(l, 272, 16)

    def body(x_ref, g_ref, *rest):
        o_ref = rest[-1]
        xv = x_ref[...]
        r = lax.rsqrt(jnp.mean(xv * xv, axis=-1, keepdims=True) + RMS_EPS)
        o_ref[...] = (xv * r * g_ref[...]).astype(BF16)

    return pl.pallas_call(
        body, name=name, grid=(l // tr,),
        in_specs=[pl.BlockSpec((tr, d), lambda i: (i, 0)), pl.BlockSpec((1, d), lambda i: (0, 0))] + _after_specs(after),
        out_specs=pl.BlockSpec((tr, d), lambda i: (i, 0)),
        out_shape=jax.ShapeDtypeStruct((l, d), BF16),
        compiler_params=_params("parallel"),
    )(x, gain, *_after_operands(after))


def _rms_bwd(dh, x, gain, gout, name):
    l, d = x.shape
    tr = _tile(l, 272, 16)

    def body(dh_ref, x_ref, g_ref, go_ref, gx_ref, gxb_ref, dg_ref):
        xv = x_ref[...]
        r = lax.rsqrt(jnp.mean(xv * xv, axis=-1, keepdims=True) + RMS_EPS)
        nv = xv * r
        dhv = dh_ref[...]
        dn = dhv * g_ref[...]
        dx = r * (dn - nv * jnp.mean(dn * nv, axis=-1, keepdims=True))
        gx = go_ref[...] + dx
        gx_ref[...] = gx
        gxb_ref[...] = gx.astype(BF16)
        part = jnp.sum(dhv * nv, axis=0, keepdims=True)

        @pl.when(pl.program_id(0) == 0)
        def _():
            dg_ref[...] = part

        @pl.when(pl.program_id(0) > 0)
        def _():
            dg_ref[...] += part

    row = pl.BlockSpec((tr, d), lambda i: (i, 0))
    vec = pl.BlockSpec((1, d), lambda i: (0, 0))
    return pl.pallas_call(
        body, name=name, grid=(l // tr,),
        in_specs=[row, row, vec, row], out_specs=[row, row, vec],
        out_shape=[jax.ShapeDtypeStruct((l, d), F32), jax.ShapeDtypeStruct((l, d), BF16),
                   jax.ShapeDtypeStruct((1, d), F32)],
        compiler_params=_params("arbitrary"),
    )(dh, x, gain, gout)


def _final_loss(x, gain, target, row_lo, row_hi, name):
    l, d = x.shape
    tr = _tile(l, 272, 16)

    def body(x_ref, g_ref, t_ref, gx_ref, gxb_ref, dg_ref, loss_ref):
        i = pl.program_id(0)
        xv = x_ref[...]
        r = lax.rsqrt(jnp.mean(xv * xv, axis=-1, keepdims=True) + RMS_EPS)
        nv = xv * r
        gv = g_ref[...]
        rows = i * tr + lax.broadcasted_iota(jnp.int32, (tr, 1), 0)
        valid = jnp.logical_and(rows >= row_lo, rows < row_hi)
        err = jnp.where(valid, nv * gv - t_ref[...], 0.0)
        dy = err * (1.0 / d)
        dn = dy * gv
        gx = r * (dn - nv * jnp.mean(dn * nv, axis=-1, keepdims=True))
        gx_ref[...] = gx
        gxb_ref[...] = gx.astype(BF16)
        part = jnp.sum(dy * nv, axis=0, keepdims=True)
        lpart = jnp.full((1, 128), 0.5 * jnp.sum(jnp.mean(err * err, axis=-1, keepdims=True)), F32)

        @pl.when(i == 0)
        def _():
            dg_ref[...] = part
            loss_ref[...] = lpart

        @pl.when(i > 0)
        def _():
            dg_ref[...] += part
            loss_ref[...] += lpart

    row = pl.BlockSpec((tr, d), lambda i: (i, 0))
    vec = pl.BlockSpec((1, d), lambda i: (0, 0))
    return pl.pallas_call(
        body, name=name, grid=(l // tr,),
        in_specs=[row, vec, row], out_specs=[row, row, vec, pl.BlockSpec((1, 128), lambda i: (0, 0))],
        out_shape=[jax.ShapeDtypeStruct((l, d), F32), jax.ShapeDtypeStruct((l, d), BF16),
                   jax.ShapeDtypeStruct((1, d), F32), jax.ShapeDtypeStruct((1, 128), F32)],
        compiler_params=_params("arbitrary"),
    )(x, gain, target)


def _log1m_beta(z):
    return -(jnp.maximum(z, 0.0) + jnp.log(1.0 + jnp.exp(-jnp.abs(z))))


def _tri(n, relation):
    r = lax.broadcasted_iota(jnp.int32, (n, n), 0)
    c = lax.broadcasted_iota(jnp.int32, (n, n), 1)
    return jnp.where(relation(r, c), 1.0, 0.0).astype(BF16)


def _head(hh):
    return slice(hh * HEAD_DIM, (hh + 1) * HEAD_DIM)


def _heads(x, hb):
    return jnp.stack([x[:, _head(hh)] for hh in range(hb)], axis=0)


def _bdot(a, b, ta=False, tb=False):
    dims = (((1 if ta else 2,), (2 if tb else 1,)), ((0,), (0,)))
    return lax.dot_general(a.astype(BF16), b.astype(BF16), dims, preferred_element_type=F32)


def _attn_specs(l, hb, n_heads):
    width = hb * HEAD_DIM
    groups = n_heads // hb

    def tile(section):
        return pl.BlockSpec((ATT_BLOCK, width), lambda h, i: (i, section * groups + h))

    def slab(section):
        return pl.BlockSpec((l, width), lambda h, i: (0, section * groups + h))

    return tile, slab


def _attn_fwd(zin, n_heads, name):
    l = zin.shape[0]
    t = ATT_BLOCK
    hb = ATT_HEADS if n_heads % ATT_HEADS == 0 else 1
    scale = HEAD_DIM ** -0.5

    def body(q_ref, k_ref, v_ref, g_ref, o_ref, oa_ref, tot_ref):
        i = pl.program_id(1)
        after = {w: _tri(w, lambda r, c: r > c) for w in (t, 2 * t)}
        causal = (lax.broadcasted_iota(jnp.int32, (t, t), 1) < lax.broadcasted_iota(jnp.int32, (t, t), 0))[None]
        q = _heads(q_ref[...], hb).astype(BF16)

        def tile(k0, w, carry, diagonal=False):
            run, acc = carry
            z = _bdot(q, _heads(k_ref[pl.ds(k0, w), :], hb), tb=True) * scale
            lb_all = _log1m_beta(z)
            lb = jnp.where(causal, lb_all, 0.0) if diagonal else lb_all
            between = _dot_split(lb.reshape(hb * t, w), after[w]).reshape(hb, t, w)
            a = jnp.exp(z + lb_all + between + run)
            if diagonal:
                a = jnp.where(causal, a, 0.0)
            return (run + jnp.sum(lb, axis=2, keepdims=True),
                    acc + _bdot(a, _heads(v_ref[pl.ds(k0, w), :], hb)))

        odd = i % 2
        carry = tile(pl.multiple_of(i * t, t), t, (jnp.zeros((hb, t, 1), F32), jnp.zeros((hb, t, HEAD_DIM), F32)), True)
        carry = lax.cond(odd == 1, lambda c: tile(pl.multiple_of((i - 1) * t, t), t, c), lambda c: c, carry)
        run, o = lax.fori_loop(
            0, i // 2, lambda it, c: tile(pl.multiple_of((i - odd - 2 * it - 2) * t, 2 * t), 2 * t, c), carry)
        for hh in range(hb):
            gate = g_ref[:, _head(hh)]
            o_ref[:, _head(hh)] = o[hh]
            oa_ref[:, _head(hh)] = (o[hh] * (gate * _sigmoid(gate))).astype(BF16)
            tot_ref[:, _head(hh)] = jnp.broadcast_to(run[hh], (t, HEAD_DIM))

    tile_spec, slab_spec = _attn_specs(l, hb, n_heads)
    width = n_heads * HEAD_DIM
    return pl.pallas_call(
        body, name=name, grid=(n_heads // hb, l // t),
        in_specs=[tile_spec(0), slab_spec(1), slab_spec(2), tile_spec(3)],
        out_specs=[tile_spec(0), tile_spec(0), tile_spec(0)],
        out_shape=[jax.ShapeDtypeStruct((l, width), F32), jax.ShapeDtypeStruct((l, width), BF16),
                   jax.ShapeDtypeStruct((l, width), F32)],
        compiler_params=_params("parallel", "arbitrary"),
    )(zin, zin, zin, zin)


def _attn_bwd(zin, o, tot, doa, n_heads, name, after=None):
    l = zin.shape[0]
    t = ATT_BLOCK
    nq = l // t
    hb = ATT_HEADS if n_heads % ATT_HEADS == 0 else 1
    scale = HEAD_DIM ** -0.5

    def body(q_ref, k_ref, v_ref, g_ref, o_ref, tot_ref, doa_ref, *rest):
        dq_ref, dk_ref, dv_ref, dg_ref, dk_acc, dv_acc = rest[-6:]
        i = pl.program_id(1)

        @pl.when(i == 0)
        def _():
            dk_acc[...] = jnp.zeros_like(dk_acc)
            dv_acc[...] = jnp.zeros_like(dv_acc)

        upto = {w: _tri(w, lambda r, c: r <= c) for w in (t, 2 * t)}
        before = {w: _tri(w, lambda r, c: r < c) for w in (t, 2 * t)}
        causal = (lax.broadcasted_iota(jnp.int32, (t, t), 1) < lax.broadcasted_iota(jnp.int32, (t, t), 0))[None]
        gate = g_ref[...]
        sg = _sigmoid(gate)
        doav = doa_ref[...]
        dg_ref[...] = (doav * o_ref[...] * (sg * (1.0 + gate * (1.0 - sg)))).astype(BF16)
        do = _heads(doav * (gate * sg), hb).astype(BF16)
        q = _heads(q_ref[...], hb).astype(BF16)
        total = _heads(tot_ref[...], hb)[:, :, 0:1]

        def tile(k0, w, carry, diagonal=False):
            run, pre, dq = carry
            kb = _heads(k_ref[pl.ds(k0, w), :], hb).astype(BF16)
            z = _bdot(q, kb, tb=True) * scale
            lb_all = _log1m_beta(z)
            lb = jnp.where(causal, lb_all, 0.0) if diagonal else lb_all
            beta = jnp.exp(z + lb_all)
            a = beta * jnp.exp(total - run - _dot_split(lb.reshape(hb * t, w), upto[w]).reshape(hb, t, w))
            if diagonal:
                a = jnp.where(causal, a, 0.0)
            e = a * _bdot(do, _heads(v_ref[pl.ds(k0, w), :], hb), tb=True)
            prefix = pre + _dot(e.reshape(hb * t, w), before[w]).reshape(hb, t, w)
            dz = (e * (1.0 - beta) - beta * prefix) * scale
            if diagonal:
                dz = jnp.where(causal, dz, 0.0)
            dv = _bdot(a, do, ta=True)
            dk = _bdot(dz, q, ta=True)
            for hh in range(hb):
                dv_acc[pl.ds(k0, w), _head(hh)] += dv[hh]
                dk_acc[pl.ds(k0, w), _head(hh)] += dk[hh]
            return (run + jnp.sum(lb, axis=2, keepdims=True), pre + jnp.sum(e, axis=2, keepdims=True),
                    dq + _bdot(dz, kb))

        zero = jnp.zeros((hb, t, 1), F32)
        carry = lax.fori_loop(0, i // 2, lambda j, c: tile(pl.multiple_of(j * 2 * t, 2 * t), 2 * t, c),
                              (zero, zero, jnp.zeros((hb, t, HEAD_DIM), F32)))
        carry = lax.cond(i % 2 == 1, lambda c: tile(pl.multiple_of((i - 1) * t, t), t, c), lambda c: c, carry)
        _, _, dq = tile(pl.multiple_of(i * t, t), t, carry, True)
        for hh in range(hb):
            dq_ref[:, _head(hh)] = dq[hh].astype(BF16)

        @pl.when(i == nq - 1)
        def _():
            dk_ref[...] = dk_acc[...].astype(BF16)
            dv_ref[...] = dv_acc[...].astype(BF16)

    tile_spec, slab_spec = _attn_specs(l, hb, n_heads)
    out = jax.ShapeDtypeStruct((l, n_heads * HEAD_DIM), BF16)
    return pl.pallas_call(
        body, name=name, grid=(n_heads // hb, nq),
        in_specs=[tile_spec(0), slab_spec(1), slab_spec(2), tile_spec(3), tile_spec(0), tile_spec(0), tile_spec(0)]
        + _after_specs(after),
        out_specs=[tile_spec(0), slab_spec(0), slab_spec(0), tile_spec(0)],
        out_shape=[out, out, out, out],
        scratch_shapes=[pltpu.VMEM((l, hb * HEAD_DIM), F32), pltpu.VMEM((l, hb * HEAD_DIM), F32)],
        compiler_params=_params("parallel", "arbitrary"),
    )(zin, zin, zin, zin, o, tot, doa, *_after_operands(after))


def _shift_rows(x, k, down):
    n = x.shape[0]
    rows = lax.broadcasted_iota(jnp.int32, x.shape, 0)
    if down:
        return jnp.where(rows >= k, pltpu.roll(x, k, 0), 0.0)
    return jnp.where(rows < n - k, pltpu.roll(x, n - k, 0), 0.0)


def _window_sum(x, g, down):
    result = x
    total = x
    for step, k in enumerate((1, 2, 4, 8)):
        total = total + _shift_rows(total, k, down)
        result = jnp.where(g >= step, total, result)
    return result


def _pooled(u, g):
    rows = lax.broadcasted_iota(jnp.int32, (u.shape[0], 1), 0)
    window = jnp.left_shift(2, g)
    cnt = jnp.minimum(rows + 1, window).astype(F32)
    return _window_sum(u, g, True) / cnt - u, cnt


def _pool_fwd(zin, pool_w, pool_scale, u_off, name, after=None):
    l = zin.shape[0]
    n_groups, gd, _ = pool_w.shape

    def body(u_ref, g_ref, w_ref, s_ref, *rest):
        o_ref = rest[-1]
        g = pl.program_id(0)
        pooled, _ = _pooled(u_ref[...], g)
        mixed = _dot(pooled, w_ref[...])
        gate = g_ref[...]
        o_ref[...] = (mixed * s_ref[...] * (gate * _sigmoid(gate))).astype(BF16)

    return pl.pallas_call(
        body, name=name, grid=(n_groups,),
        in_specs=[pl.BlockSpec((l, gd), lambda g: (0, u_off + g)),
                  pl.BlockSpec((l, gd), lambda g: (0, u_off + n_groups + g)),
                  pl.BlockSpec((None, gd, gd), lambda g: (g, 0, 0)),
                  pl.BlockSpec((1, gd), lambda g: (0, g))] + _after_specs(after),
        out_specs=pl.BlockSpec((l, gd), lambda g: (0, g)),
        out_shape=jax.ShapeDtypeStruct((l, n_groups * gd), BF16),
        compiler_params=_params("parallel"),
    )(zin, zin, pool_w, pool_scale, *_after_operands(after))


def _pool_bwd(zin, dop, pool_w, pool_scale, u_off, name):
    l = zin.shape[0]
    n_groups, gd, _ = pool_w.shape

    def body(u_ref, g_ref, w_ref, s_ref, d_ref, du_ref, dg_ref, dw_ref, ds_ref):
        g = pl.program_id(0)
        pooled, cnt = _pooled(u_ref[...], g)
        w = w_ref[...]
        mixed = _dot(pooled, w)
        gate = g_ref[...]
        sg = _sigmoid(gate)
        silu = gate * sg
        dop_v = d_ref[...]
        sc = s_ref[...]
        ds_ref[...] = jnp.sum(dop_v * mixed * silu, axis=0, keepdims=True)
        dg_ref[...] = (dop_v * mixed * sc * (sg * (1.0 + gate * (1.0 - sg)))).astype(BF16)
        dmixed = dop_v * sc * silu
        dw_ref[...] = _dot(pooled, dmixed, ta=True).astype(BF16)
        dpooled = _dot(dmixed, w, tb=True)
        du_ref[...] = (_window_sum(dpooled / cnt, g, False) - dpooled).astype(BF16)

    slab = pl.BlockSpec((l, gd), lambda g: (0, g))
    return pl.pallas_call(
        body, name=name, grid=(n_groups,),
        in_specs=[pl.BlockSpec((l, gd), lambda g: (0, u_off + g)),
                  pl.BlockSpec((l, gd), lambda g: (0, u_off + n_groups + g)),
                  pl.BlockSpec((None, gd, gd), lambda g: (g, 0, 0)),
                  pl.BlockSpec((1, gd), lambda g: (0, g)), slab],
        out_specs=[slab, slab, pl.BlockSpec((None, gd, gd), lambda g: (g, 0, 0)),
                   pl.BlockSpec((1, gd), lambda g: (0, g))],
        out_shape=[jax.ShapeDtypeStruct((l, n_groups * gd), BF16), jax.ShapeDtypeStruct((l, n_groups * gd), BF16),
                   jax.ShapeDtypeStruct(pool_w.shape, BF16), jax.ShapeDtypeStruct((1, n_groups * gd), F32)],
        compiler_params=_params("parallel"),
    )(zin, zin, pool_w, pool_scale, dop)


def _merge_fwd(oa, op, w_au, w_pu, zin, name):
    l, wa = oa.shape
    wp = op.shape[1]
    d = w_au.shape[1]
    tm = _tile(l, 544, 16)
    tn = _tile(d, 512, 128)
    ma_off = (zin.shape[1] - 2 * d) // tn

    def body(oa_ref, op_ref, wa_ref, wp_ref, ma_ref, mp_ref, ya_ref, yp_ref, mg_ref):
        ya = _dot(oa_ref[...], wa_ref[...])
        yp = _dot(op_ref[...], wp_ref[...])
        ya_ref[...] = ya
        yp_ref[...] = yp
        mg_ref[...] = (_sigmoid(ma_ref[...]) * ya + _sigmoid(mp_ref[...]) * yp).astype(BF16)

    tile = pl.BlockSpec((tm, tn), lambda i, j: (i, j))
    return pl.pallas_call(
        body, name=name, grid=(l // tm, d // tn),
        in_specs=[pl.BlockSpec((tm, wa), lambda i, j: (i, 0)), pl.BlockSpec((tm, wp), lambda i, j: (i, 0)),
                  pl.BlockSpec((wa, tn), lambda i, j: (0, j)), pl.BlockSpec((wp, tn), lambda i, j: (0, j)),
                  pl.BlockSpec((tm, tn), lambda i, j: (i, ma_off + j)),
                  pl.BlockSpec((tm, tn), lambda i, j: (i, ma_off + d // tn + j))],
        out_specs=[tile, tile, tile],
        out_shape=[jax.ShapeDtypeStruct((l, d), F32), jax.ShapeDtypeStruct((l, d), F32),
                   jax.ShapeDtypeStruct((l, d), BF16)],
        compiler_params=_params("parallel", "parallel"),
    )(oa, op, w_au, w_pu, zin, zin)


def _merge_bwd(gout, w_out, zin, ya, yp, name):
    l, d = gout.shape
    tm = _tile(l, 544, 16)
    tn = _tile(d, 512, 128)
    ma_off = (zin.shape[1] - 2 * d) // tn

    def body(g_ref, w_ref, ma_ref, mp_ref, ya_ref, yp_ref, dya_ref, dyp_ref, dma_ref, dmp_ref):
        dm = _dot(g_ref[...], w_ref[...], tb=True)
        sa = _sigmoid(ma_ref[...])
        sp = _sigmoid(mp_ref[...])
        dya_ref[...] = (dm * sa).astype(BF16)
        dyp_ref[...] = (dm * sp).astype(BF16)
        dma_ref[...] = (dm * ya_ref[...] * (sa * (1.0 - sa))).astype(BF16)
        dmp_ref[...] = (dm * yp_ref[...] * (sp * (1.0 - sp))).astype(BF16)

    tile = pl.BlockSpec((tm, tn), lambda i, j: (i, j))
    out = jax.ShapeDtypeStruct((l, d), BF16)
    return pl.pallas_call(
        body, name=name, grid=(l // tm, d // tn),
        in_specs=[pl.BlockSpec((tm, d), lambda i, j: (i, 0)), pl.BlockSpec((tn, d), lambda i, j: (j, 0)),
                  pl.BlockSpec((tm, tn), lambda i, j: (i, ma_off + j)),
                  pl.BlockSpec((tm, tn), lambda i, j: (i, ma_off + d // tn + j)), tile, tile],
        out_specs=[tile, tile, tile, tile],
        out_shape=[out, out, out, out],
        compiler_params=_params("parallel", "parallel"),
    )(gout, w_out, zin, zin, ya, yp)


def _adamw(parts, w, m, v, name):
    n_arrays, n_parts, r, c = parts.shape
    tr = _tile(r, max(8, (128 * 1024) // c // 8 * 8), 8)
    bias1 = 1.0 - ADAM_B1 ** ADAM_STEP
    bias2 = 1.0 - ADAM_B2 ** ADAM_STEP

    def body(p_ref, w_ref, m_ref, v_ref, g_ref, d_ref, nm_ref, nv_ref):
        g = p_ref[0].astype(F32)
        for j in range(1, n_parts):
            g = g + p_ref[j].astype(F32)
        nm = ADAM_B1 * m_ref[...] + (1.0 - ADAM_B1) * g
        nv = ADAM_B2 * v_ref[...] + (1.0 - ADAM_B2) * (g * g)
        g_ref[...] = g
        nm_ref[...] = nm
        nv_ref[...] = nv
        d_ref[...] = -ADAM_LR * ((nm / bias1) / (jnp.sqrt(nv / bias2) + ADAM_EPS) + ADAM_WD * w_ref[...])

    tile = pl.BlockSpec((None, tr, c), lambda a, i: (a, i, 0))
    out = jax.ShapeDtypeStruct((n_arrays, r, c), F32)
    return pl.pallas_call(
        body, name=name, grid=(n_arrays, r // tr),
        in_specs=[pl.BlockSpec((None, n_parts, tr, c), lambda a, i: (a, 0, i, 0)), tile, tile, tile],
        out_specs=[tile, tile, tile, tile], out_shape=[out, out, out, out],
        compiler_params=_params("parallel", "parallel"),
    )(parts, w, m, v)


def _position():
    return lax.axis_index("x"), lax.axis_index("y"), lax.axis_index("c")


def _block_of(ref, axis, size, index):
    idx = [slice(None)] * len(ref.shape)
    idx[axis] = pl.ds(index * size, size)
    return ref.at[tuple(idx)]


HBM_SPEC = pl.BlockSpec(memory_space=pltpu.HBM)
SEM_SPEC = pl.BlockSpec(memory_space=pltpu.SEMAPHORE)
SIDE_EFFECT = pltpu.CompilerParams(has_side_effects=pltpu.SideEffectType.DATAFLOW_SIDE_EFFECTING)


def _split_start(make_copies, n_copies, buffers, name):
    n = len(buffers)

    def body(*refs):
        send_sems, recv_sems = refs[n], refs[n + 1]
        for cp in make_copies(refs[:n], send_sems, recv_sems):
            cp.start()
        refs[-1][...] = jnp.zeros_like(refs[-1])

    sems = pltpu.SemaphoreType.DMA((n_copies,))
    return pl.pallas_call(
        body, name=name, in_specs=[HBM_SPEC] * n,
        out_shape=(sems, sems, *[pltpu.HBM(b.shape, b.dtype) for b in buffers], jax.ShapeDtypeStruct((8, 128), F32)),
        out_specs=(SEM_SPEC, SEM_SPEC, *[HBM_SPEC] * n, pl.BlockSpec(memory_space=pltpu.VMEM)),
        input_output_aliases={i: 2 + i for i in range(n)}, compiler_params=SIDE_EFFECT,
    )(*[pltpu.with_memory_space_constraint(b, pltpu.HBM) for b in buffers])


def _split_wait(make_copies, started, after, name):
    send_sems, recv_sems, *buffers = started[:-1]
    n = len(buffers)

    def body(*refs):
        copies = make_copies(refs[:n], refs[n], refs[n + 1])
        for cp in copies:
            cp.wait_send()
        for cp in copies:
            cp.wait_recv()

    return pl.pallas_call(
        body, name=name, in_specs=[HBM_SPEC] * n + [SEM_SPEC, SEM_SPEC, pl.BlockSpec(memory_space=pl.ANY)],
        out_shape=[pltpu.HBM(b.shape, b.dtype) for b in buffers], out_specs=[HBM_SPEC] * n,
        input_output_aliases={i: i for i in range(n)}, compiler_params=SIDE_EFFECT,
    )(*buffers, send_sems, recv_sems, after)


def _gather_copies(axes, sizes, level):
    def make(fulls, send_sems, recv_sems):
        x, y, c = _position()
        chips = [(1 - x, y), (x, 1 - y), (1 - x, 1 - y)]
        copies = []
        for a, full in enumerate(fulls):
            def copy(k, block, to, full=full, a=a):
                rows = _block_of(full, axes[a], sizes[a], 4 * block[0] + 2 * block[1] + block[2])
                idx = a * (4 if level == 1 else 3) + k
                return pltpu.make_async_remote_copy(src_ref=rows, dst_ref=rows, send_sem=send_sems.at[idx],
                                                    recv_sem=recv_sems.at[idx], device_id=to, device_id_type=MESH)
            if level == 1:
                copies.append(copy(0, (x, y, c), (x, y, 1 - c)))
                copies += [copy(1 + j, (x, y, c), (*chip, c)) for j, chip in enumerate(chips)]
            else:
                copies += [copy(j, (*chip, c), (x, y, 1 - c)) for j, chip in enumerate(chips)]
        return copies
    return make


def _exchange_copies(axes, sizes, layer, n_src):
    flips = [(a, b, d) for a in (0, 1) for b in (0, 1) for d in (0, 1)][1:]

    def make(buffers, send_sems, recv_sems):
        x, y, c = _position()
        my_index = 4 * x + 2 * y + c
        copies = []
        for a in range(n_src):
            for k, flip in enumerate(flips):
                px, py, pc = x ^ flip[0], y ^ flip[1], c ^ flip[2]
                copies.append(pltpu.make_async_remote_copy(
                    src_ref=_block_of(buffers[a], axes[a], sizes[a], 4 * px + 2 * py + pc),
                    dst_ref=buffers[n_src + a].at[layer, my_index],
                    send_sem=send_sems.at[a * 7 + k], recv_sem=recv_sems.at[a * 7 + k],
                    device_id=(px, py, pc), device_id_type=MESH))
        return copies
    return make


def _allgather_small(v, name):
    r, c = v.shape
    flips = [(a, b, d) for a in (0, 1) for b in (0, 1) for d in (0, 1)][1:]

    def body(v_ref, out_ref, send_sems, recv_sems):
        x, y, c_ = _position()
        my_index = 4 * x + 2 * y + c_
        out_ref[my_index] = v_ref[...]
        sends = []
        for k, flip in enumerate(flips):
            peer = (x ^ flip[0], y ^ flip[1], c_ ^ flip[2])
            cp = pltpu.make_async_remote_copy(
                src_ref=v_ref, dst_ref=out_ref.at[my_index],
                send_sem=send_sems.at[k], recv_sem=recv_sems.at[k], device_id=peer, device_id_type=MESH)
            cp.start()
            sends.append(cp)
        for k, flip in enumerate(flips):
            px, py, pc = x ^ flip[0], y ^ flip[1], c_ ^ flip[2]
            pltpu.make_async_remote_copy(
                src_ref=v_ref, dst_ref=out_ref.at[4 * px + 2 * py + pc],
                send_sem=send_sems.at[k], recv_sem=recv_sems.at[k],
                device_id=(px, py, pc), device_id_type=MESH).wait_recv()
        for cp in sends:
            cp.wait_send()

    return pl.pallas_call(
        body, name=name,
        in_specs=[pl.BlockSpec(memory_space=pltpu.VMEM)], out_specs=pl.BlockSpec(memory_space=pltpu.VMEM),
        out_shape=jax.ShapeDtypeStruct((N_DEV, r, c), v.dtype),
        scratch_shapes=[pltpu.SemaphoreType.DMA((7,)), pltpu.SemaphoreType.DMA((7,))],
    )(v)


def kernel(x, meta_tokens, norm_gain, w_in, pool_w, pool_scale, w_attn_up, w_pool_up, w_out, final_gain, loss_target, m_meta_tokens, m_norm_gain, m_w_in, m_pool_w, m_pool_scale, m_w_attn_up, m_w_pool_up, m_w_out, m_final_gain, v_meta_tokens, v_norm_gain, v_w_in, v_pool_w, v_pool_scale, v_w_attn_up, v_w_pool_up, v_w_out, v_final_gain):
    _, seq, d = x.shape
    n_meta = meta_tokens.shape[0]
    depth = w_in.shape[0]
    sb_width = w_attn_up.shape[1]
    pool_width = w_pool_up.shape[1]
    n_heads = sb_width // HEAD_DIM
    n_groups = pool_w.shape[1]
    gd = pool_w.shape[3]
    assert n_groups == len(POOL_WINDOWS) and gd * n_groups == pool_width
    assert w_in.shape[2] * N_DEV == 4 * sb_width + 2 * pool_width + 2 * d
    l_real = n_meta + seq
    l_pad = -(-l_real // ATT_BLOCK) * ATT_BLOCK
    my_index = 4 * lax.axis_index("x") + 2 * lax.axis_index("y") + lax.axis_index("c")

    me = jnp.reshape(my_index, (1,)).astype(jnp.int32)
    g_named = [("w_in", w_in), ("pool_w", pool_w), ("w_attn_up", w_attn_up), ("w_pool_up", w_pool_up), ("w_out", w_out)]
    g_axes = [1, 1, 1, 1, 0]
    g_sizes = [w.shape[1 + ax] for (_, w), ax in zip(g_named, g_axes)]
    level1 = _gather_copies(g_axes, g_sizes, 1)
    level2 = _gather_copies(g_axes, g_sizes, 2)

    def gather_start(i, after=None):
        fulls = [_cast_place(w, i, ax, me, "cast_" + nm, after) for (nm, w), ax in zip(g_named, g_axes)]
        return _split_start(level1, 4 * len(fulls), fulls, "gather1_start_%d" % i)

    def gather_forward(i, started, after):
        arrived = _split_wait(level1, started, after, "gather1_wait_%d" % i)
        return _split_start(level2, 3 * len(arrived), arrived, "gather2_start_%d" % i)

    meta_all = _allgather_small(meta_tokens, "allgather_meta")
    meta_full = jnp.transpose(meta_all, (1, 0, 2)).reshape(n_meta, d)
    first = gather_start(0)
    second = gather_forward(0, first, first[-1])
    weights = [_split_wait(level2, second, second[-1], "gather2_wait_0")]

    pad_rows = l_pad - l_real
    hs = jnp.concatenate([meta_full, x[0], jnp.zeros((pad_rows, d), F32)], axis=0)
    target = jnp.concatenate([jnp.zeros((n_meta, d), F32), loss_target[0], jnp.zeros((pad_rows, d), F32)], axis=0)
    u_off = 4 * sb_width // gd
    saved = []
    for i in range(depth):
        wi, pw, wau, wpu, wo = weights[i]
        more = i + 1 < depth
        first = gather_start(i + 1, wi) if more else None
        h = _rms_fwd(hs, norm_gain[i][None], "rms_fwd", after=first[-1] if more else None)
        zin = _matmul(h, wi, name="mm_zin")
        o, oa, tot = _attn_fwd(zin, n_heads, "attn_fwd")
        second = gather_forward(i + 1, first, oa) if more else None
        op = _pool_fwd(zin, pw, pool_scale[i][None], u_off, "pool_fwd", after=second[-1] if more else None)
        ya, yp, merged = _merge_fwd(oa, op, wau, wpu, zin, "merge_fwd")
        saved.append((hs, h, zin, o, tot, oa, op, ya, yp, merged))
        hs = _matmul(merged, wo, add=hs, name="mm_out")
        if more:
            weights.append(_split_wait(level2, second, hs, "gather2_wait_%d" % (i + 1)))
    g, gb, d_final_gain, loss_part = _final_loss(hs, final_gain[None], target, n_meta, l_real, "final_loss")
    loss = lax.psum(loss_part[0, 0], ("x", "y", "c"))

    d_norm_gain = [None] * depth
    d_pool_scale = [None] * depth
    axes_a, axes_b = [1, 1, 0], [1, 1]
    blocks_a, blocks_b = [w_attn_up[0], w_pool_up[0], w_out[0]], [w_in[0], pool_w[0]]
    sizes_a = [b.shape[ax] for b, ax in zip(blocks_a, axes_a)]
    sizes_b = [b.shape[ax] for b, ax in zip(blocks_b, axes_b)]
    land_a = [lax.empty((depth, N_DEV, *b.shape), BF16) for b in blocks_a]
    land_b = [lax.empty((depth, N_DEV, *b.shape), BF16) for b in blocks_b]

    def exchange_start(grads, landing, axes, sizes, layer, name):
        landing = [lax.dynamic_update_slice(
            zone, lax.dynamic_slice_in_dim(grad, my_index * size, size, ax)[None, None],
            (layer, my_index) + (0,) * grad.ndim) for zone, grad, ax, size in zip(landing, grads, axes, sizes)]
        copies = _exchange_copies(axes, sizes, layer, len(grads))
        return copies, _split_start(copies, 7 * len(grads), list(grads) + landing, name)

    def exchange_wait(pending, n_src, after, name):
        return _split_wait(pending[0], pending[1], after, name)[n_src:]

    pend_a = pend_b = None
    for i in reversed(range(depth)):
        wi, pw, wau, wpu, wo = weights[i]
        hs_in, h, zin, o, tot, oa, op, ya, yp, merged = saved[i]
        dya, dyp, dma, dmp = _merge_bwd(gb, wo, zin, ya, yp, "merge_bwd")
        dw_out = _matmul(merged, gb, ta=True, out_dtype=BF16, name="mm_dw_out", tm=1024, tk=1088)
        doa = _matmul(dya, wau, tb=True, name="mm_doa")
        dw_au = _matmul(oa, dya, ta=True, out_dtype=BF16, name="mm_dw_au", tm=1024, tk=1088)
        dop = _matmul(dyp, wpu, tb=True, name="mm_dop")
        dw_pu = _matmul(op, dyp, ta=True, out_dtype=BF16, name="mm_dw_pu", tm=1024, tk=1088)
        if pend_a is not None:
            land_a = exchange_wait(pend_a, 3, dw_pu, "exchange_a_wait_%d" % (i + 1))
        pend_a = exchange_start([dw_au, dw_pu, dw_out], land_a, axes_a, sizes_a, i, "exchange_a_start_%d" % i)
        dq, dk, dv, dga = _attn_bwd(zin, o, tot, doa, n_heads, "attn_bwd", after=pend_a[1][-1])
        du, dgp, dpw, dps = _pool_bwd(zin, dop, pw, pool_scale[i][None], u_off, "pool_bwd")
        dzin = [dq, dk, dv, dga, du, dgp, dma, dmp]
        dw_in = _matmul(h, dzin, ta=True, out_dtype=BF16, name="mm_dw_in", tm=1024, tk=1088)
        if pend_b is not None:
            land_b = exchange_wait(pend_b, 2, dw_in, "exchange_b_wait_%d" % (i + 1))
        pend_b = exchange_start([dw_in, dpw], land_b, axes_b, sizes_b, i, "exchange_b_start_%d" % i)
        dh = _matmul(dzin, wi, tb=True, after=pend_b[1][-1], name="mm_dh")
        g, gb, dng = _rms_bwd(dh, hs_in, norm_gain[i][None], g, "rms_bwd")
        d_norm_gain[i] = dng
        d_pool_scale[i] = dps
    land_a = exchange_wait(pend_a, 3, gb, "exchange_a_wait_0")
    land_b = exchange_wait(pend_b, 2, gb, "exchange_b_wait_0")
    grad_x = g[n_meta:l_real][None]

    zeros_ps = jnp.zeros((depth, d - pool_width), F32)
    small_rows = [jnp.concatenate(d_norm_gain, axis=0),
                  jnp.concatenate([jnp.concatenate(d_pool_scale, axis=0), zeros_ps], axis=1), d_final_gain]
    n_small = 2 * depth + 1
    small_pad = -(-n_small // 8) * 8
    small = jnp.concatenate(small_rows + [jnp.zeros((small_pad - n_small, d), F32), g[:n_meta]], axis=0)
    small_all = _allgather_small(small, "allgather_small")

    def replicated(rows, width, w, m, v, name):
        parts = lax.slice(small_all, (0, rows[0], 0), (N_DEV, rows[1], width))
        return [t[0] for t in _adamw(parts[None], w[None], m[None], v[None], name)]

    out_ng = replicated((0, depth), d, norm_gain, m_norm_gain, v_norm_gain, "adamw_norm_gain")
    out_ps = replicated((depth, 2 * depth), pool_width, pool_scale, m_pool_scale, v_pool_scale, "adamw_pool_scale")
    out_fg = [t[0] for t in replicated((2 * depth, 2 * depth + 1), d, final_gain[None], m_final_gain[None],
                                       v_final_gain[None], "adamw_final_gain")]
    cols = d // N_DEV
    meta_parts = lax.dynamic_slice(small_all, (0, small_pad, my_index * cols), (N_DEV, n_meta, cols))
    out_meta = [t[0] for t in _adamw(meta_parts[None], meta_tokens[None], m_meta_tokens[None], v_meta_tokens[None],
                                     "adamw_meta")]

    def sharded(parts, w, m, v, name):
        flat = (depth, -1, w.shape[-1])
        res = _adamw(parts.reshape(depth, N_DEV, -1, w.shape[-1]), w.reshape(flat), m.reshape(flat), v.reshape(flat),
                     name)
        return [t.reshape(w.shape) for t in res]

    out_wi = sharded(land_b[0], w_in, m_w_in, v_w_in, "adamw_w_in")
    out_pw = sharded(land_b[1], pool_w, m_pool_w, v_pool_w, "adamw_pool_w")
    out_au = sharded(land_a[0], w_attn_up, m_w_attn_up, v_w_attn_up, "adamw_w_attn_up")
    out_pu = sharded(land_a[1], w_pool_up, m_w_pool_up, v_w_pool_up, "adamw_w_pool_up")
    out_wo = sharded(land_a[2], w_out, m_w_out, v_w_out, "adamw_w_out")

    by_weight = [out_meta, out_ng, out_wi, out_pw, out_ps, out_au, out_pu, out_wo, out_fg]
    return (loss, grad_x, *[o[0] for o in by_weight], *[o[1] for o in by_weight],
            *[o[2] for o in by_weight], *[o[3] for o in by_weight])
```

```python
import jax
import jax.numpy as jnp
from jax import lax
from jax.experimental import pallas as pl
from jax.experimental.pallas import tpu as pltpu

F32 = jnp.float32
BF16 = jnp.bfloat16
MESH = pl.DeviceIdType.MESH

N_DEV = 8
HEAD_DIM = 128
ATT_BLOCK = 128
ATT_HEADS = 4
POOL_WINDOWS = (2, 4, 8, 16)
RMS_EPS = 1e-6
ADAM_LR, ADAM_B1, ADAM_B2, ADAM_EPS, ADAM_WD, ADAM_STEP = 0.001, 0.9, 0.999, 1e-08, 0.01, 10
VMEM_LIMIT_BYTES = 56 * 1024 * 1024


def _tile(n, target, mult):
    if n <= target:
        return n
    best = 0
    for t in range(mult, target + 1, mult):
        if n % t == 0:
            best = t
    assert best > 0, (n, target, mult)
    return best


def _params(*sem):
    return pltpu.CompilerParams(dimension_semantics=sem, vmem_limit_bytes=VMEM_LIMIT_BYTES)


def _sigmoid(x):
    return 1.0 / (1.0 + jnp.exp(-x))


def _dot(a, b, ta=False, tb=False):
    dims = (((0 if ta else 1,), (1 if tb else 0,)), ((), ()))
    return lax.dot_general(a.astype(BF16), b.astype(BF16), dims, preferred_element_type=F32)


def _dot_split(a, b):
    hi = a.astype(BF16)
    lo = (a - hi.astype(F32)).astype(BF16)
    return _dot(hi, b) + _dot(lo, b)


def _cast_place(x, layer, axis, me, name, after=None):
    blk = x.shape[1:]
    full = list(blk)
    full[axis] *= N_DEV
    if len(blk) == 2:
        r, c = blk
        tr = _tile(r, max(16, (512 * 1024) // c // 16 * 16), 16)
        steps = r // tr
        in_spec = pl.BlockSpec((None, tr, c), lambda i, me_ref: (layer, i, 0))
        if axis == 1:
            out_spec = pl.BlockSpec((tr, c), lambda i, me_ref: (i, me_ref[0]))
        else:
            out_spec = pl.BlockSpec((tr, c), lambda i, me_ref: (me_ref[0] * steps + i, 0))
    else:
        assert len(blk) == 3 and axis == 1
        steps = 1
        in_spec = pl.BlockSpec((None, *blk), lambda i, me_ref: (layer, 0, 0, 0))
        out_spec = pl.BlockSpec(blk, lambda i, me_ref: (0, me_ref[0], 0))

    def body(me_ref, x_ref, *rest):
        rest[-1][...] = x_ref[...].astype(BF16)

    return pl.pallas_call(
        body, name=name,
        grid_spec=pltpu.PrefetchScalarGridSpec(num_scalar_prefetch=1, grid=(steps,),
                                               in_specs=[in_spec] + _after_specs(after), out_specs=out_spec),
        out_shape=jax.ShapeDtypeStruct(tuple(full), BF16),
        compiler_params=_params("arbitrary"),
    )(me, x, *_after_operands(after))


def _after_operands(after):
    return [] if after is None else [after]


def _after_specs(after):
    return [] if after is None else [pl.BlockSpec(memory_space=pl.ANY)]


def _matmul(a, b, *, ta=False, tb=False, out_dtype=F32, add=None, after=None, name, tm=1088, tn=1024, tk=2048):
    a_list = list(a) if isinstance(a, (list, tuple)) else None
    b_list = list(b) if isinstance(b, (list, tuple)) else None
    assert not (a_list and ta) and not (b_list and tb) and not (a_list and b_list)
    if a_list:
        m, k = a_list[0].shape[0], sum(p.shape[1] for p in a_list)
    else:
        m, k = (a.shape[1], a.shape[0]) if ta else a.shape
    if b_list:
        n = sum(p.shape[1] for p in b_list)
    else:
        n = b.shape[0] if tb else b.shape[1]
    tm = _tile(m, tm, 128 if ta else 16)
    tn = _tile(n, tn, 128)
    tk = _tile(k, tk, 128 if (not ta or tb) else 16)
    if a_list:
        while any(p.shape[1] % tk for p in a_list):
            tk //= 2
    if b_list:
        while any(p.shape[1] % tn for p in b_list):
            tn //= 2
    nk = k // tk

    def ranges(pieces, tile):
        out, start = [], 0
        for p in pieces:
            out.append((start, start + p.shape[1] // tile))
            start = out[-1][1]
        return out

    if a_list:
        a_ranges = ranges(a_list, tk)
        a_specs = [pl.BlockSpec((tm, tk), lambda i, j, kk, s=s, e=e: (i, jnp.clip(kk - s, 0, e - s - 1)))
                   for s, e in a_ranges]
    else:
        a_specs = [pl.BlockSpec((tk, tm), lambda i, j, kk: (kk, i)) if ta
                   else pl.BlockSpec((tm, tk), lambda i, j, kk: (i, kk))]
    if b_list:
        b_ranges = ranges(b_list, tn)
        b_specs = [pl.BlockSpec((tk, tn), lambda i, j, kk, s=s, e=e: (
            jnp.where(jnp.logical_and(j >= s, j < e), kk, 0), jnp.clip(j - s, 0, e - s - 1))) for s, e in b_ranges]
    else:
        b_specs = [pl.BlockSpec((tn, tk), lambda i, j, kk: (j, kk)) if tb
                   else pl.BlockSpec((tk, tn), lambda i, j, kk: (kk, j))]
    o_spec = pl.BlockSpec((tm, tn), lambda i, j, kk: (i, j))
    has_add = add is not None
    na, nb = len(a_specs), len(b_specs)
    simple = nk == 1 and not a_list and not b_list

    def body(*refs):
        a_refs, b_refs = refs[:na], refs[na:na + nb]
        add_ref = refs[na + nb] if has_add else None
        o_ref, acc_ref = refs[-2], refs[-1]
        j, kk = pl.program_id(1), pl.program_id(2)

        def finish(total):
            if has_add:
                total = total + add_ref[...]
            o_ref[...] = total.astype(out_dtype)

        if simple:
            finish(_dot(a_refs[0][...], b_refs[0][...], ta, tb))
            return

        @pl.when(kk == 0)
        def _():
            acc_ref[...] = jnp.zeros_like(acc_ref)

        if a_list:
            for (s, e), a_ref in zip(a_ranges, a_refs):
                @pl.when(jnp.logical_and(kk >= s, kk < e))
                def _(a_ref=a_ref):
                    acc_ref[...] += _dot(a_ref[...], b_refs[0][...], ta, tb)
        elif b_list:
            for (s, e), b_ref in zip(b_ranges, b_refs):
                @pl.when(jnp.logical_and(j >= s, j < e))
                def _(b_ref=b_ref):
                    acc_ref[...] += _dot(a_refs[0][...], b_ref[...], ta, tb)
        else:
            acc_ref[...] += _dot(a_refs[0][...], b_refs[0][...], ta, tb)

        @pl.when(kk == nk - 1)
        def _():
            finish(acc_ref[...])

    operands = (a_list or [a]) + (b_list or [b]) + ([add] if has_add else []) + _after_operands(after)
    in_specs = a_specs + b_specs + ([o_spec] if has_add else []) + _after_specs(after)
    return pl.pallas_call(
        body, name=name, grid=(m // tm, n // tn, nk),
        in_specs=in_specs, out_specs=o_spec,
        out_shape=jax.ShapeDtypeStruct((m, n), out_dtype),
        scratch_shapes=[pltpu.VMEM((8, 128) if simple else (tm, tn), F32)],
        compiler_params=_params("parallel", "parallel", "arbitrary"),
    )(*operands)


def _rms_fwd(x, gain, name, after=None):
    l, d = x.shape
    tr = _tile(l, 272, 16)

    def body(x_ref, g_ref, *rest):
        o_ref = rest[-1]
        xv = x_ref[...]
        r = lax.rsqrt(jnp.mean(xv * xv, axis=-1, keepdims=True) + RMS_EPS)
        o_ref[...] = (xv * r * g_ref[...]).astype(BF16)

    return pl.pallas_call(
        body, name=name, grid=(l // tr,),
        in_specs=[pl.BlockSpec((tr, d), lambda i: (i, 0)), pl.BlockSpec((1, d), lambda i: (0, 0))] + _after_specs(after),
        out_specs=pl.BlockSpec((tr, d), lambda i: (i, 0)),
        out_shape=jax.ShapeDtypeStruct((l, d), BF16),
        compiler_params=_params("parallel"),
    )(x, gain, *_after_operands(after))


def _rms_bwd(dh, x, gain, gout, name):
    l, d = x.shape
    tr = _tile(l, 272, 16)

    def body(dh_ref, x_ref, g_ref, go_ref, gx_ref, gxb_ref, dg_ref):
        xv = x_ref[...]
        r = lax.rsqrt(jnp.mean(xv * xv, axis=-1, keepdims=True) + RMS_EPS)
        nv = xv * r
        dhv = dh_ref[...]
        dn = dhv * g_ref[...]
        dx = r * (dn - nv * jnp.mean(dn * nv, axis=-1, keepdims=True))
        gx = go_ref[...] + dx
        gx_ref[...] = gx
        gxb_ref[...] = gx.astype(BF16)
        part = jnp.sum(dhv * nv, axis=0, keepdims=True)

        @pl.when(pl.program_id(0) == 0)
        def _():
            dg_ref[...] = part

        @pl.when(pl.program_id(0) > 0)
        def _():
            dg_ref[...] += part

    row = pl.BlockSpec((tr, d), lambda i: (i, 0))
    vec = pl.BlockSpec((1, d), lambda i: (0, 0))
    return pl.pallas_call(
        body, name=name, grid=(l // tr,),
        in_specs=[row, row, vec, row], out_specs=[row, row, vec],
        out_shape=[jax.ShapeDtypeStruct((l, d), F32), jax.ShapeDtypeStruct((l, d), BF16),
                   jax.ShapeDtypeStruct((1, d), F32)],
        compiler_params=_params("arbitrary"),
    )(dh, x, gain, gout)


def _final_loss(x, gain, target, row_lo, row_hi, name):
    l, d = x.shape
    tr = _tile(l, 272, 16)

    def body(x_ref, g_ref, t_ref, gx_ref, gxb_ref, dg_ref, loss_ref):
        i = pl.program_id(0)
        xv = x_ref[...]
        r = lax.rsqrt(jnp.mean(xv * xv, axis=-1, keepdims=True) + RMS_EPS)
        nv = xv * r
        gv = g_ref[...]
        rows = i * tr + lax.broadcasted_iota(jnp.int32, (tr, 1), 0)
        valid = jnp.logical_and(rows >= row_lo, rows < row_hi)
        err = jnp.where(valid, nv * gv - t_ref[...], 0.0)
        dy = err * (1.0 / d)
        dn = dy * gv
        gx = r * (dn - nv * jnp.mean(dn * nv, axis=-1, keepdims=True))
        gx_ref[...] = gx
        gxb_ref[...] = gx.astype(BF16)
        part = jnp.sum(dy * nv, axis=0, keepdims=True)
        lpart = jnp.full((1, 128), 0.5 * jnp.sum(jnp.mean(err * err, axis=-1, keepdims=True)), F32)

        @pl.when(i == 0)
        def _():
            dg_ref[...] = part
            loss_ref[...] = lpart

        @pl.when(i > 0)
        def _():
            dg_ref[...] += part
            loss_ref[...] += lpart

    row = pl.BlockSpec((tr, d), lambda i: (i, 0))
    vec = pl.BlockSpec((1, d), lambda i: (0, 0))
    return pl.pallas_call(
        body, name=name, grid=(l // tr,),
        in_specs=[row, vec, row], out_specs=[row, row, vec, pl.BlockSpec((1, 128), lambda i: (0, 0))],
        out_shape=[jax.ShapeDtypeStruct((l, d), F32), jax.ShapeDtypeStruct((l, d), BF16),
                   jax.ShapeDtypeStruct((1, d), F32), jax.ShapeDtypeStruct((1, 128), F32)],
        compiler_params=_params("arbitrary"),
    )(x, gain, target)


def _log1m_beta(z):
    return -(jnp.maximum(z, 0.0) + jnp.log(1.0 + jnp.exp(-jnp.abs(z))))


def _tri(n, relation):
    r = lax.broadcasted_iota(jnp.int32, (n, n), 0)
    c = lax.broadcasted_iota(jnp.int32, (n, n), 1)
    return jnp.where(relation(r, c), 1.0, 0.0).astype(BF16)


def _head(hh):
    return slice(hh * HEAD_DIM, (hh + 1) * HEAD_DIM)


def _heads(x, hb):
    return jnp.stack([x[:, _head(hh)] for hh in range(hb)], axis=0)


def _bdot(a, b, ta=False, tb=False):
    dims = (((1 if ta else 2,), (2 if tb else 1,)), ((0,), (0,)))
    return lax.dot_general(a.astype(BF16), b.astype(BF16), dims, preferred_element_type=F32)


def _attn_specs(l, hb, n_heads):
    width = hb * HEAD_DIM
    groups = n_heads // hb

    def tile(section):
        return pl.BlockSpec((ATT_BLOCK, width), lambda h, i: (i, section * groups + h))

    def slab(section):
        return pl.BlockSpec((l, width), lambda h, i: (0, section * groups + h))

    return tile, slab


def _attn_fwd(zin, n_heads, name):
    l = zin.shape[0]
    t = ATT_BLOCK
    hb = ATT_HEADS if n_heads % ATT_HEADS == 0 else 1
    scale = HEAD_DIM ** -0.5

    def body(q_ref, k_ref, v_ref, g_ref, o_ref, oa_ref, tot_ref):
        i = pl.program_id(1)
        after = {w: _tri(w, lambda r, c: r > c) for w in (t, 2 * t, 4 * t)}
        causal = (lax.broadcasted_iota(jnp.int32, (t, t), 1) < lax.broadcasted_iota(jnp.int32, (t, t), 0))[None]
        q = _heads(q_ref[...], hb).astype(BF16)

        def tile(k0, w, carry, diagonal=False):
            run, acc = carry
            z = _bdot(q, _heads(k_ref[pl.ds(k0, w), :], hb), tb=True) * scale
            lb_all = _log1m_beta(z)
            lb = jnp.where(causal, lb_all, 0.0) if diagonal else lb_all
            between = _dot_split(lb.reshape(hb * t, w), after[w]).reshape(hb, t, w)
            a = jnp.exp(z + lb_all + between + run)
            if diagonal:
                a = jnp.where(causal, a, 0.0)
            return (run + jnp.sum(lb, axis=2, keepdims=True),
                    acc + _bdot(a, _heads(v_ref[pl.ds(k0, w), :], hb)))

        rem = i % 4
        carry = tile(pl.multiple_of(i * t, t), t, (jnp.zeros((hb, t, 1), F32), jnp.zeros((hb, t, HEAD_DIM), F32)), True)
        carry = lax.cond(rem % 2 == 1, lambda c: tile(pl.multiple_of((i - 1) * t, t), t, c), lambda c: c, carry)
        carry = lax.cond(rem >= 2, lambda c: tile(pl.multiple_of((i - rem % 2 - 2) * t, 2 * t), 2 * t, c),
                         lambda c: c, carry)
        run, o = lax.fori_loop(
            0, i // 4, lambda it, c: tile(pl.multiple_of((i - rem - 4 * it - 4) * t, 4 * t), 4 * t, c), carry)
        for hh in range(hb):
            gate = g_ref[:, _head(hh)]
            o_ref[:, _head(hh)] = o[hh]
            oa_ref[:, _head(hh)] = (o[hh] * (gate * _sigmoid(gate))).astype(BF16)
            tot_ref[:, _head(hh)] = jnp.broadcast_to(run[hh], (t, HEAD_DIM))

    tile_spec, slab_spec = _attn_specs(l, hb, n_heads)
    width = n_heads * HEAD_DIM
    return pl.pallas_call(
        body, name=name, grid=(n_heads // hb, l // t),
        in_specs=[tile_spec(0), slab_spec(1), slab_spec(2), tile_spec(3)],
        out_specs=[tile_spec(0), tile_spec(0), tile_spec(0)],
        out_shape=[jax.ShapeDtypeStruct((l, width), F32), jax.ShapeDtypeStruct((l, width), BF16),
                   jax.ShapeDtypeStruct((l, width), F32)],
        compiler_params=_params("parallel", "arbitrary"),
    )(zin, zin, zin, zin)


def _attn_bwd(zin, o, tot, doa, n_heads, name, after=None):
    l = zin.shape[0]
    t = ATT_BLOCK
    nq = l // t
    hb = ATT_HEADS if n_heads % ATT_HEADS == 0 else 1
    scale = HEAD_DIM ** -0.5

    def body(q_ref, k_ref, v_ref, g_ref, o_ref, tot_ref, doa_ref, *rest):
        dq_ref, dk_ref, dv_ref, dg_ref, dk_acc, dv_acc = rest[-6:]
        i = pl.program_id(1)

        @pl.when(i == 0)
        def _():
            dk_acc[...] = jnp.zeros_like(dk_acc)
            dv_acc[...] = jnp.zeros_like(dv_acc)

        upto = {w: _tri(w, lambda r, c: r <= c) for w in (t, 2 * t, 4 * t)}
        before = {w: _tri(w, lambda r, c: r < c) for w in (t, 2 * t, 4 * t)}
        causal = (lax.broadcasted_iota(jnp.int32, (t, t), 1) < lax.broadcasted_iota(jnp.int32, (t, t), 0))[None]
        gate = g_ref[...]
        sg = _sigmoid(gate)
        doav = doa_ref[...]
        dg_ref[...] = (doav * o_ref[...] * (sg * (1.0 + gate * (1.0 - sg)))).astype(BF16)
        do = _heads(doav * (gate * sg), hb).astype(BF16)
        q = _heads(q_ref[...], hb).astype(BF16)
        total = _heads(tot_ref[...], hb)[:, :, 0:1]

        def tile(k0, w, carry, diagonal=False):
            run, pre, dq = carry
            kb = _heads(k_ref[pl.ds(k0, w), :], hb).astype(BF16)
            z = _bdot(q, kb, tb=True) * scale
            lb_all = _log1m_beta(z)
            lb = jnp.where(causal, lb_all, 0.0) if diagonal else lb_all
            beta = jnp.exp(z + lb_all)
            a = beta * jnp.exp(total - run - _dot_split(lb.reshape(hb * t, w), upto[w]).reshape(hb, t, w))
            if diagonal:
                a = jnp.where(causal, a, 0.0)
            e = a * _bdot(do, _heads(v_ref[pl.ds(k0, w), :], hb), tb=True)
            prefix = pre + _dot(e.reshape(hb * t, w), before[w]).reshape(hb, t, w)
            dz = (e * (1.0 - beta) - beta * prefix) * scale
            if diagonal:
                dz = jnp.where(causal, dz, 0.0)
            dv = _bdot(a, do, ta=True)
            dk = _bdot(dz, q, ta=True)
            for hh in range(hb):
                dv_acc[pl.ds(k0, w), _head(hh)] += dv[hh]
                dk_acc[pl.ds(k0, w), _head(hh)] += dk[hh]
            return (run + jnp.sum(lb, axis=2, keepdims=True), pre + jnp.sum(e, axis=2, keepdims=True),
                    dq + _bdot(dz, kb))

        rem = i % 4
        zero = jnp.zeros((hb, t, 1), F32)
        carry = lax.fori_loop(0, i // 4, lambda j, c: tile(pl.multiple_of(j * 4 * t, 4 * t), 4 * t, c),
                              (zero, zero, jnp.zeros((hb, t, HEAD_DIM), F32)))
        carry = lax.cond(rem >= 2, lambda c: tile(pl.multiple_of((i - rem) * t, 2 * t), 2 * t, c), lambda c: c, carry)
        carry = lax.cond(rem % 2 == 1, lambda c: tile(pl.multiple_of((i - 1) * t, t), t, c), lambda c: c, carry)
        _, _, dq = tile(pl.multiple_of(i * t, t), t, carry, True)
        for hh in range(hb):
            dq_ref[:, _head(hh)] = dq[hh].astype(BF16)

        @pl.when(i == nq - 1)
        def _():
            dk_ref[...] = dk_acc[...].astype(BF16)
            dv_ref[...] = dv_acc[...].astype(BF16)

    tile_spec, slab_spec = _attn_specs(l, hb, n_heads)
    out = jax.ShapeDtypeStruct((l, n_heads * HEAD_DIM), BF16)
    return pl.pallas_call(
        body, name=name, grid=(n_heads // hb, nq),
        in_specs=[tile_spec(0), slab_spec(1), slab_spec(2), tile_spec(3), tile_spec(0), tile_spec(0), tile_spec(0)]
        + _after_specs(after),
        out_specs=[tile_spec(0), slab_spec(0), slab_spec(0), tile_spec(0)],
        out_shape=[out, out, out, out],
        scratch_shapes=[pltpu.VMEM((l, hb * HEAD_DIM), F32), pltpu.VMEM((l, hb * HEAD_DIM), F32)],
        compiler_params=_params("parallel", "arbitrary"),
    )(zin, zin, zin, zin, o, tot, doa, *_after_operands(after))


def _shift_rows(x, k, down):
    n = x.shape[0]
    rows = lax.broadcasted_iota(jnp.int32, x.shape, 0)
    if down:
        return jnp.where(rows >= k, pltpu.roll(x, k, 0), 0.0)
    return jnp.where(rows < n - k, pltpu.roll(x, n - k, 0), 0.0)


def _window_sum(x, g, down):
    result = x
    total = x
    for step, k in enumerate((1, 2, 4, 8)):
        total = total + _shift_rows(total, k, down)
        result = jnp.where(g >= step, total, result)
    return result


def _pooled(u, g):
    rows = lax.broadcasted_iota(jnp.int32, (u.shape[0], 1), 0)
    window = jnp.left_shift(2, g)
    cnt = jnp.minimum(rows + 1, window).astype(F32)
    return _window_sum(u, g, True) / cnt - u, cnt


def _pool_fwd(zin, pool_w, pool_scale, u_off, name, after=None):
    l = zin.shape[0]
    n_groups, gd, _ = pool_w.shape

    def body(u_ref, g_ref, w_ref, s_ref, *rest):
        o_ref = rest[-1]
        g = pl.program_id(0)
        pooled, _ = _pooled(u_ref[...], g)
        mixed = _dot(pooled, w_ref[...])
        gate = g_ref[...]
        o_ref[...] = (mixed * s_ref[...] * (gate * _sigmoid(gate))).astype(BF16)

    return pl.pallas_call(
        body, name=name, grid=(n_groups,),
        in_specs=[pl.BlockSpec((l, gd), lambda g: (0, u_off + g)),
                  pl.BlockSpec((l, gd), lambda g: (0, u_off + n_groups + g)),
                  pl.BlockSpec((None, gd, gd), lambda g: (g, 0, 0)),
                  pl.BlockSpec((1, gd), lambda g: (0, g))] + _after_specs(after),
        out_specs=pl.BlockSpec((l, gd), lambda g: (0, g)),
        out_shape=jax.ShapeDtypeStruct((l, n_groups * gd), BF16),
        compiler_params=_params("parallel"),
    )(zin, zin, pool_w, pool_scale, *_after_operands(after))


def _pool_bwd(zin, dop, pool_w, pool_scale, u_off, name):
    l = zin.shape[0]
    n_groups, gd, _ = pool_w.shape

    def body(u_ref, g_ref, w_ref, s_ref, d_ref, du_ref, dg_ref, dw_ref, ds_ref):
        g = pl.program_id(0)
        pooled, cnt = _pooled(u_ref[...], g)
        w = w_ref[...]
        mixed = _dot(pooled, w)
        gate = g_ref[...]
        sg = _sigmoid(gate)
        silu = gate * sg
        dop_v = d_ref[...]
        sc = s_ref[...]
        ds_ref[...] = jnp.sum(dop_v * mixed * silu, axis=0, keepdims=True)
        dg_ref[...] = (dop_v * mixed * sc * (sg * (1.0 + gate * (1.0 - sg)))).astype(BF16)
        dmixed = dop_v * sc * silu
        dw_ref[...] = _dot(pooled, dmixed, ta=True).astype(BF16)
        dpooled = _dot(dmixed, w, tb=True)
        du_ref[...] = (_window_sum(dpooled / cnt, g, False) - dpooled).astype(BF16)

    slab = pl.BlockSpec((l, gd), lambda g: (0, g))
    return pl.pallas_call(
        body, name=name, grid=(n_groups,),
        in_specs=[pl.BlockSpec((l, gd), lambda g: (0, u_off + g)),
                  pl.BlockSpec((l, gd), lambda g: (0, u_off + n_groups + g)),
                  pl.BlockSpec((None, gd, gd), lambda g: (g, 0, 0)),
                  pl.BlockSpec((1, gd), lambda g: (0, g)), slab],
        out_specs=[slab, slab, pl.BlockSpec((None, gd, gd), lambda g: (g, 0, 0)),
                   pl.BlockSpec((1, gd), lambda g: (0, g))],
        out_shape=[jax.ShapeDtypeStruct((l, n_groups * gd), BF16), jax.ShapeDtypeStruct((l, n_groups * gd), BF16),
                   jax.ShapeDtypeStruct(pool_w.shape, BF16), jax.ShapeDtypeStruct((1, n_groups * gd), F32)],
        compiler_params=_params("parallel"),
    )(zin, zin, pool_w, pool_scale, dop)


def _merge_fwd(oa, op, w_au, w_pu, zin, name):
    l, wa = oa.shape
    wp = op.shape[1]
    d = w_au.shape[1]
    tm = _tile(l, 1088, 16)
    tn = _tile(d, 512, 128)
    ma_off = (zin.shape[1] - 2 * d) // tn

    def body(oa_ref, op_ref, wa_ref, wp_ref, ma_ref, mp_ref, ya_ref, yp_ref, mg_ref):
        ya = _dot(oa_ref[...], wa_ref[...])
        yp = _dot(op_ref[...], wp_ref[...])
        ya_ref[...] = ya.astype(BF16)
        yp_ref[...] = yp.astype(BF16)
        mg_ref[...] = (_sigmoid(ma_ref[...]) * ya + _sigmoid(mp_ref[...]) * yp).astype(BF16)

    tile = pl.BlockSpec((tm, tn), lambda i, j: (i, j))
    return pl.pallas_call(
        body, name=name, grid=(l // tm, d // tn),
        in_specs=[pl.BlockSpec((tm, wa), lambda i, j: (i, 0)), pl.BlockSpec((tm, wp), lambda i, j: (i, 0)),
                  pl.BlockSpec((wa, tn), lambda i, j: (0, j)), pl.BlockSpec((wp, tn), lambda i, j: (0, j)),
                  pl.BlockSpec((tm, tn), lambda i, j: (i, ma_off + j)),
                  pl.BlockSpec((tm, tn), lambda i, j: (i, ma_off + d // tn + j))],
        out_specs=[tile, tile, tile],
        out_shape=[jax.ShapeDtypeStruct((l, d), BF16)] * 3,
        compiler_params=_params("parallel", "parallel"),
    )(oa, op, w_au, w_pu, zin, zin)


def _merge_bwd(gout, w_out, zin, ya, yp, name):
    l, d = gout.shape
    tm = _tile(l, 1088, 16)
    tn = _tile(d, 512, 128)
    ma_off = (zin.shape[1] - 2 * d) // tn

    def body(g_ref, w_ref, ma_ref, mp_ref, ya_ref, yp_ref, dya_ref, dyp_ref, dma_ref, dmp_ref):
        dm = _dot(g_ref[...], w_ref[...], tb=True)
        sa = _sigmoid(ma_ref[...])
        sp = _sigmoid(mp_ref[...])
        dya_ref[...] = (dm * sa).astype(BF16)
        dyp_ref[...] = (dm * sp).astype(BF16)
        dma_ref[...] = (dm * ya_ref[...].astype(F32) * (sa * (1.0 - sa))).astype(BF16)
        dmp_ref[...] = (dm * yp_ref[...].astype(F32) * (sp * (1.0 - sp))).astype(BF16)

    tile = pl.BlockSpec((tm, tn), lambda i, j: (i, j))
    out = jax.ShapeDtypeStruct((l, d), BF16)
    return pl.pallas_call(
        body, name=name, grid=(l // tm, d // tn),
        in_specs=[pl.BlockSpec((tm, d), lambda i, j: (i, 0)), pl.BlockSpec((tn, d), lambda i, j: (j, 0)),
                  pl.BlockSpec((tm, tn), lambda i, j: (i, ma_off + j)),
                  pl.BlockSpec((tm, tn), lambda i, j: (i, ma_off + d // tn + j)), tile, tile],
        out_specs=[tile, tile, tile, tile],
        out_shape=[out, out, out, out],
        compiler_params=_params("parallel", "parallel"),
    )(gout, w_out, zin, zin, ya, yp)


def _adamw(parts, w, m, v, name):
    n_arrays, n_parts, r, c = parts.shape
    tr = _tile(r, max(8, (128 * 1024) // c // 8 * 8), 8)
    bias1 = 1.0 - ADAM_B1 ** ADAM_STEP
    bias2 = 1.0 - ADAM_B2 ** ADAM_STEP

    def body(p_ref, w_ref, m_ref, v_ref, g_ref, d_ref, nm_ref, nv_ref):
        g = p_ref[0].astype(F32)
        for j in range(1, n_parts):
            g = g + p_ref[j].astype(F32)
        nm = ADAM_B1 * m_ref[...] + (1.0 - ADAM_B1) * g
        nv = ADAM_B2 * v_ref[...] + (1.0 - ADAM_B2) * (g * g)
        g_ref[...] = g
        nm_ref[...] = nm
        nv_ref[...] = nv
        d_ref[...] = -ADAM_LR * ((nm / bias1) / (jnp.sqrt(nv / bias2) + ADAM_EPS) + ADAM_WD * w_ref[...])

    tile = pl.BlockSpec((None, tr, c), lambda a, i: (a, i, 0))
    out = jax.ShapeDtypeStruct((n_arrays, r, c), F32)
    return pl.pallas_call(
        body, name=name, grid=(n_arrays, r // tr),
        in_specs=[pl.BlockSpec((None, n_parts, tr, c), lambda a, i: (a, 0, i, 0)), tile, tile, tile],
        out_specs=[tile, tile, tile, tile], out_shape=[out, out, out, out],
        compiler_params=_params("parallel", "parallel"),
    )(parts, w, m, v)


def _position():
    return lax.axis_index("x"), lax.axis_index("y"), lax.axis_index("c")


def _block_of(ref, axis, size, index):
    idx = [slice(None)] * len(ref.shape)
    idx[axis] = pl.ds(index * size, size)
    return ref.at[tuple(idx)]


HBM_SPEC = pl.BlockSpec(memory_space=pltpu.HBM)
SEM_SPEC = pl.BlockSpec(memory_space=pltpu.SEMAPHORE)
SIDE_EFFECT = pltpu.CompilerParams(has_side_effects=pltpu.SideEffectType.DATAFLOW_SIDE_EFFECTING)


def _split_start(make_copies, n_copies, buffers, name):
    n = len(buffers)

    def body(*refs):
        send_sems, recv_sems = refs[n], refs[n + 1]
        for cp in make_copies(refs[:n], send_sems, recv_sems):
            cp.start()
        refs[-1][...] = jnp.zeros_like(refs[-1])

    sems = pltpu.SemaphoreType.DMA((n_copies,))
    return pl.pallas_call(
        body, name=name, in_specs=[HBM_SPEC] * n,
        out_shape=(sems, sems, *[pltpu.HBM(b.shape, b.dtype) for b in buffers], jax.ShapeDtypeStruct((8, 128), F32)),
        out_specs=(SEM_SPEC, SEM_SPEC, *[HBM_SPEC] * n, pl.BlockSpec(memory_space=pltpu.VMEM)),
        input_output_aliases={i: 2 + i for i in range(n)}, compiler_params=SIDE_EFFECT,
    )(*[pltpu.with_memory_space_constraint(b, pltpu.HBM) for b in buffers])


def _split_wait(make_copies, started, after, name):
    send_sems, recv_sems, *buffers = started[:-1]
    n = len(buffers)

    def body(*refs):
        copies = make_copies(refs[:n], refs[n], refs[n + 1])
        for cp in copies:
            cp.wait_send()
        for cp in copies:
            cp.wait_recv()

    return pl.pallas_call(
        body, name=name, in_specs=[HBM_SPEC] * n + [SEM_SPEC, SEM_SPEC, pl.BlockSpec(memory_space=pl.ANY)],
        out_shape=[pltpu.HBM(b.shape, b.dtype) for b in buffers], out_specs=[HBM_SPEC] * n,
        input_output_aliases={i: i for i in range(n)}, compiler_params=SIDE_EFFECT,
    )(*buffers, send_sems, recv_sems, after)


def _gather_copies(axes, sizes, level):
    def make(fulls, send_sems, recv_sems):
        x, y, c = _position()
        chips = [(1 - x, y), (x, 1 - y), (1 - x, 1 - y)]
        copies = []
        for a, full in enumerate(fulls):
            def copy(k, block, to, full=full, a=a):
                rows = _block_of(full, axes[a], sizes[a], 4 * block[0] + 2 * block[1] + block[2])
                idx = a * (4 if level == 1 else 3) + k
                return pltpu.make_async_remote_copy(src_ref=rows, dst_ref=rows, send_sem=send_sems.at[idx],
                                                    recv_sem=recv_sems.at[idx], device_id=to, device_id_type=MESH)
            if level == 1:
                copies.append(copy(0, (x, y, c), (x, y, 1 - c)))
                copies += [copy(1 + j, (x, y, c), (*chip, c)) for j, chip in enumerate(chips)]
            else:
                copies += [copy(j, (*chip, c), (x, y, 1 - c)) for j, chip in enumerate(chips)]
        return copies
    return make


def _exchange_copies(axes, sizes, layer, n_src):
    flips = [(a, b, d) for a in (0, 1) for b in (0, 1) for d in (0, 1)][1:]

    def make(buffers, send_sems, recv_sems):
        x, y, c = _position()
        my_index = 4 * x + 2 * y + c
        copies = []
        for a in range(n_src):
            for k, flip in enumerate(flips):
                px, py, pc = x ^ flip[0], y ^ flip[1], c ^ flip[2]
                copies.append(pltpu.make_async_remote_copy(
                    src_ref=_block_of(buffers[a], axes[a], sizes[a], 4 * px + 2 * py + pc),
                    dst_ref=buffers[n_src + a].at[layer, my_index],
                    send_sem=send_sems.at[a * 7 + k], recv_sem=recv_sems.at[a * 7 + k],
                    device_id=(px, py, pc), device_id_type=MESH))
        return copies
    return make


def _allgather_small(v, name):
    r, c = v.shape
    flips = [(a, b, d) for a in (0, 1) for b in (0, 1) for d in (0, 1)][1:]

    def body(v_ref, out_ref, send_sems, recv_sems):
        x, y, c_ = _position()
        my_index = 4 * x + 2 * y + c_
        out_ref[my_index] = v_ref[...]
        sends = []
        for k, flip in enumerate(flips):
            peer = (x ^ flip[0], y ^ flip[1], c_ ^ flip[2])
            cp = pltpu.make_async_remote_copy(
                src_ref=v_ref, dst_ref=out_ref.at[my_index],
                send_sem=send_sems.at[k], recv_sem=recv_sems.at[k], device_id=peer, device_id_type=MESH)
            cp.start()
            sends.append(cp)
        for k, flip in enumerate(flips):
            px, py, pc = x ^ flip[0], y ^ flip[1], c_ ^ flip[2]
            pltpu.make_async_remote_copy(
                src_ref=v_ref, dst_ref=out_ref.at[4 * px + 2 * py + pc],
                send_sem=send_sems.at[k], recv_sem=recv_sems.at[k],
                device_id=(px, py, pc), device_id_type=MESH).wait_recv()
        for cp in sends:
            cp.wait_send()

    return pl.pallas_call(
        body, name=name,
        in_specs=[pl.BlockSpec(memory_space=pltpu.VMEM)], out_specs=pl.BlockSpec(memory_space=pltpu.VMEM),
        out_shape=jax.ShapeDtypeStruct((N_DEV, r, c), v.dtype),
        scratch_shapes=[pltpu.SemaphoreType.DMA((7,)), pltpu.SemaphoreType.DMA((7,))],
    )(v)


def kernel(x, meta_tokens, norm_gain, w_in, pool_w, pool_scale, w_attn_up, w_pool_up, w_out, final_gain, loss_target, m_meta_tokens, m_norm_gain, m_w_in, m_pool_w, m_pool_scale, m_w_attn_up, m_w_pool_up, m_w_out, m_final_gain, v_meta_tokens, v_norm_gain, v_w_in, v_pool_w, v_pool_scale, v_w_attn_up, v_w_pool_up, v_w_out, v_final_gain):
    _, seq, d = x.shape
    n_meta = meta_tokens.shape[0]
    depth = w_in.shape[0]
    sb_width = w_attn_up.shape[1]
    pool_width = w_pool_up.shape[1]
    n_heads = sb_width // HEAD_DIM
    n_groups = pool_w.shape[1]
    gd = pool_w.shape[3]
    assert n_groups == len(POOL_WINDOWS) and gd * n_groups == pool_width
    assert w_in.shape[2] * N_DEV == 4 * sb_width + 2 * pool_width + 2 * d
    l_real = n_meta + seq
    l_pad = -(-l_real // ATT_BLOCK) * ATT_BLOCK
    my_index = 4 * lax.axis_index("x") + 2 * lax.axis_index("y") + lax.axis_index("c")

    me = jnp.reshape(my_index, (1,)).astype(jnp.int32)
    g_named = [("w_in", w_in), ("pool_w", pool_w), ("w_attn_up", w_attn_up), ("w_pool_up", w_pool_up), ("w_out", w_out)]
    g_axes = [1, 1, 1, 1, 0]
    g_sizes = [w.shape[1 + ax] for (_, w), ax in zip(g_named, g_axes)]
    level1 = _gather_copies(g_axes, g_sizes, 1)
    level2 = _gather_copies(g_axes, g_sizes, 2)

    def gather_start(i, after=None):
        fulls = [_cast_place(w, i, ax, me, "cast_" + nm, after) for (nm, w), ax in zip(g_named, g_axes)]
        return _split_start(level1, 4 * len(fulls), fulls, "gather1_start_%d" % i)

    def gather_forward(i, started, after):
        arrived = _split_wait(level1, started, after, "gather1_wait_%d" % i)
        return _split_start(level2, 3 * len(arrived), arrived, "gather2_start_%d" % i)

    meta_all = _allgather_small(meta_tokens, "allgather_meta")
    meta_full = jnp.transpose(meta_all, (1, 0, 2)).reshape(n_meta, d)
    first = gather_start(0)
    second = gather_forward(0, first, first[-1])

    pad_rows = l_pad - l_real
    hs = jnp.concatenate([meta_full, x[0], jnp.zeros((pad_rows, d), F32)], axis=0)
    target = jnp.concatenate([jnp.zeros((n_meta, d), F32), loss_target[0], jnp.zeros((pad_rows, d), F32)], axis=0)
    u_off = 4 * sb_width // gd
    saved, weights = [], []
    for i in range(depth):
        h = _rms_fwd(hs, norm_gain[i][None], "rms_fwd")
        weights.append(_split_wait(level2, second, h, "gather2_wait_%d" % i))
        wi, pw, wau, wpu, wo = weights[i]
        more = i + 1 < depth
        first = gather_start(i + 1, wi) if more else None
        zin = _matmul(h, wi, after=first[-1] if more else None, name="mm_zin")
        o, oa, tot = _attn_fwd(zin, n_heads, "attn_fwd")
        op = _pool_fwd(zin, pw, pool_scale[i][None], u_off, "pool_fwd")
        ya, yp, merged = _merge_fwd(oa, op, wau, wpu, zin, "merge_fwd")
        second = gather_forward(i + 1, first, merged) if more else None
        saved.append((hs, h, zin, o, tot, oa, op, ya, yp, merged))
        hs = _matmul(merged, wo, add=hs, after=second[-1] if more else None, name="mm_out")
    g, gb, d_final_gain, loss_part = _final_loss(hs, final_gain[None], target, n_meta, l_real, "final_loss")
    loss = lax.psum(loss_part[0, 0], ("x", "y", "c"))

    d_norm_gain = [None] * depth
    d_pool_scale = [None] * depth
    axes_a, axes_b = [1, 1, 0], [1, 1]
    blocks_a, blocks_b = [w_attn_up[0], w_pool_up[0], w_out[0]], [w_in[0], pool_w[0]]
    sizes_a = [b.shape[ax] for b, ax in zip(blocks_a, axes_a)]
    sizes_b = [b.shape[ax] for b, ax in zip(blocks_b, axes_b)]
    land_a = [lax.empty((depth, N_DEV, *b.shape), BF16) for b in blocks_a]
    land_b = [lax.empty((depth, N_DEV, *b.shape), BF16) for b in blocks_b]

    def exchange_start(grads, landing, axes, sizes, layer, name):
        landing = [lax.dynamic_update_slice(
            zone, lax.dynamic_slice_in_dim(grad, my_index * size, size, ax)[None, None],
            (layer, my_index) + (0,) * grad.ndim) for zone, grad, ax, size in zip(landing, grads, axes, sizes)]
        copies = _exchange_copies(axes, sizes, layer, len(grads))
        return copies, _split_start(copies, 7 * len(grads), list(grads) + landing, name)

    def exchange_wait(pending, n_src, after, name):
        return _split_wait(pending[0], pending[1], after, name)[n_src:]

    pend_a = pend_b = None
    for i in reversed(range(depth)):
        wi, pw, wau, wpu, wo = weights[i]
        hs_in, h, zin, o, tot, oa, op, ya, yp, merged = saved[i]
        dya, dyp, dma, dmp = _merge_bwd(gb, wo, zin, ya, yp, "merge_bwd")
        dw_out = _matmul(merged, gb, ta=True, out_dtype=BF16, name="mm_dw_out", tm=1024, tk=1088)
        doa = _matmul(dya, wau, tb=True, name="mm_doa")
        dw_au = _matmul(oa, dya, ta=True, out_dtype=BF16, name="mm_dw_au", tm=1024, tk=1088)
        dop = _matmul(dyp, wpu, tb=True, name="mm_dop")
        dw_pu = _matmul(op, dyp, ta=True, out_dtype=BF16, name="mm_dw_pu", tm=1024, tk=1088)
        if pend_a is not None:
            land_a = exchange_wait(pend_a, 3, dw_pu, "exchange_a_wait_%d" % (i + 1))
        pend_a = exchange_start([dw_au, dw_pu, dw_out], land_a, axes_a, sizes_a, i, "exchange_a_start_%d" % i)
        dq, dk, dv, dga = _attn_bwd(zin, o, tot, doa, n_heads, "attn_bwd", after=pend_a[1][-1])
        du, dgp, dpw, dps = _pool_bwd(zin, dop, pw, pool_scale[i][None], u_off, "pool_bwd")
        dzin = [dq, dk, dv, dga, du, dgp, dma, dmp]
        dw_in = _matmul(h, dzin, ta=True, out_dtype=BF16, name="mm_dw_in", tm=1024, tk=1088)
        if pend_b is not None:
            land_b = exchange_wait(pend_b, 2, dw_in, "exchange_b_wait_%d" % (i + 1))
        pend_b = exchange_start([dw_in, dpw], land_b, axes_b, sizes_b, i, "exchange_b_start_%d" % i)
        dh = _matmul(dzin, wi, tb=True, after=pend_b[1][-1], name="mm_dh")
        g, gb, dng = _rms_bwd(dh, hs_in, norm_gain[i][None], g, "rms_bwd")
        d_norm_gain[i] = dng
        d_pool_scale[i] = dps
    land_a = exchange_wait(pend_a, 3, gb, "exchange_a_wait_0")
    land_b = exchange_wait(pend_b, 2, gb, "exchange_b_wait_0")
    grad_x = g[n_meta:l_real][None]

    zeros_ps = jnp.zeros((depth, d - pool_width), F32)
    small_rows = [jnp.concatenate(d_norm_gain, axis=0),
                  jnp.concatenate([jnp.concatenate(d_pool_scale, axis=0), zeros_ps], axis=1), d_final_gain]
    n_small = 2 * depth + 1
    small_pad = -(-n_small // 8) * 8
    small = jnp.concatenate(small_rows + [jnp.zeros((small_pad - n_small, d), F32), g[:n_meta]], axis=0)
    small_all = _allgather_small(small, "allgather_small")

    def replicated(rows, width, w, m, v, name):
        parts = lax.slice(small_all, (0, rows[0], 0), (N_DEV, rows[1], width))
        return [t[0] for t in _adamw(parts[None], w[None], m[None], v[None], name)]

    out_ng = replicated((0, depth), d, norm_gain, m_norm_gain, v_norm_gain, "adamw_norm_gain")
    out_ps = replicated((depth, 2 * depth), pool_width, pool_scale, m_pool_scale, v_pool_scale, "adamw_pool_scale")
    out_fg = [t[0] for t in replicated((2 * depth, 2 * depth + 1), d, final_gain[None], m_final_gain[None],
                                       v_final_gain[None], "adamw_final_gain")]
    cols = d // N_DEV
    meta_parts = lax.dynamic_slice(small_all, (0, small_pad, my_index * cols), (N_DEV, n_meta, cols))
    out_meta = [t[0] for t in _adamw(meta_parts[None], meta_tokens[None], m_meta_tokens[None], v_meta_tokens[None],
                                     "adamw_meta")]

    def sharded(parts, w, m, v, name):
        flat = (depth, -1, w.shape[-1])
        res = _adamw(parts.reshape(depth, N_DEV, -1, w.shape[-1]), w.reshape(flat), m.reshape(flat), v.reshape(flat),
                     name)
        return [t.reshape(w.shape) for t in res]

    out_wi = sharded(land_b[0], w_in, m_w_in, v_w_in, "adamw_w_in")
    out_pw = sharded(land_b[1], pool_w, m_pool_w, v_pool_w, "adamw_pool_w")
    out_au = sharded(land_a[0], w_attn_up, m_w_attn_up, v_w_attn_up, "adamw_w_attn_up")
    out_pu = sharded(land_a[1], w_pool_up, m_w_pool_up, v_w_pool_up, "adamw_w_pool_up")
    out_wo = sharded(land_a[2], w_out, m_w_out, v_w_out, "adamw_w_out")

    by_weight = [out_meta, out_ng, out_wi, out_pw, out_ps, out_au, out_pu, out_wo, out_fg]
    return (loss, grad_x, *[o[0] for o in by_weight], *[o[1] for o in by_weight],
            *[o[2] for o in by_weight], *[o[3] for o in by_weight])
```

```python
import jax
import jax.numpy as jnp
from jax import lax
from jax.experimental import pallas as pl
from jax.experimental.pallas import tpu as pltpu

F32 = jnp.float32
BF16 = jnp.bfloat16
MESH = pl.DeviceIdType.MESH

N_DEV = 8
HEAD_DIM = 128
ATT_BLOCK = 128
ATT_HEADS = 4
POOL_WINDOWS = (2, 4, 8, 16)
RMS_EPS = 1e-6
ADAM_LR, ADAM_B1, ADAM_B2, ADAM_EPS, ADAM_WD, ADAM_STEP = 0.001, 0.9, 0.999, 1e-08, 0.01, 10
VMEM_LIMIT_BYTES = 56 * 1024 * 1024


def _tile(n, target, mult):
    if n <= target:
        return n
    best = 0
    for t in range(mult, target + 1, mult):
        if n % t == 0:
            best = t
    assert best > 0, (n, target, mult)
    return best


def _params(*sem):
    return pltpu.CompilerParams(dimension_semantics=sem, vmem_limit_bytes=VMEM_LIMIT_BYTES)


def _sigmoid(x):
    return 1.0 / (1.0 + jnp.exp(-x))


def _dot(a, b, ta=False, tb=False):
    dims = (((0 if ta else 1,), (1 if tb else 0,)), ((), ()))
    return lax.dot_general(a.astype(BF16), b.astype(BF16), dims, preferred_element_type=F32)


def _dot_split(a, b):
    hi = a.astype(BF16)
    lo = (a - hi.astype(F32)).astype(BF16)
    return _dot(hi, b) + _dot(lo, b)


def _cast_place(x, layer, axis, me, name, after=None):
    blk = x.shape[1:]
    full = list(blk)
    full[axis] *= N_DEV
    if len(blk) == 2:
        r, c = blk
        tr = _tile(r, max(16, (512 * 1024) // c // 16 * 16), 16)
        steps = r // tr
        in_spec = pl.BlockSpec((None, tr, c), lambda i, me_ref: (layer, i, 0))
        if axis == 1:
            out_spec = pl.BlockSpec((tr, c), lambda i, me_ref: (i, me_ref[0]))
        else:
            out_spec = pl.BlockSpec((tr, c), lambda i, me_ref: (me_ref[0] * steps + i, 0))
    else:
        assert len(blk) == 3 and axis == 1
        steps = 1
        in_spec = pl.BlockSpec((None, *blk), lambda i, me_ref: (layer, 0, 0, 0))
        out_spec = pl.BlockSpec(blk, lambda i, me_ref: (0, me_ref[0], 0))

    def body(me_ref, x_ref, *rest):
        rest[-1][...] = x_ref[...].astype(BF16)

    return pl.pallas_call(
        body, name=name,
        grid_spec=pltpu.PrefetchScalarGridSpec(num_scalar_prefetch=1, grid=(steps,),
                                               in_specs=[in_spec] + _after_specs(after), out_specs=out_spec),
        out_shape=jax.ShapeDtypeStruct(tuple(full), BF16),
        compiler_params=_params("arbitrary"),
    )(me, x, *_after_operands(after))


def _after_operands(after):
    return [] if after is None else [after]


def _after_specs(after):
    return [] if after is None else [pl.BlockSpec(memory_space=pl.ANY)]


def _matmul(a, b, *, ta=False, tb=False, out_dtype=F32, add=None, after=None, name, tm=1088, tn=1024, tk=2048):
    a_list = list(a) if isinstance(a, (list, tuple)) else None
    b_list = list(b) if isinstance(b, (list, tuple)) else None
    assert not (a_list and ta) and not (b_list and tb) and not (a_list and b_list)
    if a_list:
        m, k = a_list[0].shape[0], sum(p.shape[1] for p in a_list)
    else:
        m, k = (a.shape[1], a.shape[0]) if ta else a.shape
    if b_list:
        n = sum(p.shape[1] for p in b_list)
    else:
        n = b.shape[0] if tb else b.shape[1]
    tm = _tile(m, tm, 128 if ta else 16)
    tn = _tile(n, tn, 128)
    tk = _tile(k, tk, 128 if (not ta or tb) else 16)
    if a_list:
        while any(p.shape[1] % tk for p in a_list):
            tk //= 2
    if b_list:
        while any(p.shape[1] % tn for p in b_list):
            tn //= 2
    nk = k // tk

    def ranges(pieces, tile):
        out, start = [], 0
        for p in pieces:
            out.append((start, start + p.shape[1] // tile))
            start = out[-1][1]
        return out

    if a_list:
        a_ranges = ranges(a_list, tk)
        a_specs = [pl.BlockSpec((tm, tk), lambda i, j, kk, s=s, e=e: (i, jnp.clip(kk - s, 0, e - s - 1)))
                   for s, e in a_ranges]
    else:
        a_specs = [pl.BlockSpec((tk, tm), lambda i, j, kk: (kk, i)) if ta
                   else pl.BlockSpec((tm, tk), lambda i, j, kk: (i, kk))]
    if b_list:
        b_ranges = ranges(b_list, tn)
        b_specs = [pl.BlockSpec((tk, tn), lambda i, j, kk, s=s, e=e: (
            jnp.where(jnp.logical_and(j >= s, j < e), kk, 0), jnp.clip(j - s, 0, e - s - 1))) for s, e in b_ranges]
    else:
        b_specs = [pl.BlockSpec((tn, tk), lambda i, j, kk: (j, kk)) if tb
                   else pl.BlockSpec((tk, tn), lambda i, j, kk: (kk, j))]
    o_spec = pl.BlockSpec((tm, tn), lambda i, j, kk: (i, j))
    has_add = add is not None
    na, nb = len(a_specs), len(b_specs)
    simple = nk == 1 and not a_list and not b_list

    def body(*refs):
        a_refs, b_refs = refs[:na], refs[na:na + nb]
        add_ref = refs[na + nb] if has_add else None
        o_ref, acc_ref = refs[-2], refs[-1]
        j, kk = pl.program_id(1), pl.program_id(2)

        def finish(total):
            if has_add:
                total = total + add_ref[...]
            o_ref[...] = total.astype(out_dtype)

        if simple:
            finish(_dot(a_refs[0][...], b_refs[0][...], ta, tb))
            return

        @pl.when(kk == 0)
        def _():
            acc_ref[...] = jnp.zeros_like(acc_ref)

        if a_list:
            for (s, e), a_ref in zip(a_ranges, a_refs):
                @pl.when(jnp.logical_and(kk >= s, kk < e))
                def _(a_ref=a_ref):
                    acc_ref[...] += _dot(a_ref[...], b_refs[0][...], ta, tb)
        elif b_list:
            for (s, e), b_ref in zip(b_ranges, b_refs):
                @pl.when(jnp.logical_and(j >= s, j < e))
                def _(b_ref=b_ref):
                    acc_ref[...] += _dot(a_refs[0][...], b_ref[...], ta, tb)
        else:
            acc_ref[...] += _dot(a_refs[0][...], b_refs[0][...], ta, tb)

        @pl.when(kk == nk - 1)
        def _():
            finish(acc_ref[...])

    operands = (a_list or [a]) + (b_list or [b]) + ([add] if has_add else []) + _after_operands(after)
    in_specs = a_specs + b_specs + ([o_spec] if has_add else []) + _after_specs(after)
    return pl.pallas_call(
        body, name=name, grid=(m // tm, n // tn, nk),
        in_specs=in_specs, out_specs=o_spec,
        out_shape=jax.ShapeDtypeStruct((m, n), out_dtype),
        scratch_shapes=[pltpu.VMEM((8, 128) if simple else (tm, tn), F32)],
        compiler_params=_params("parallel", "parallel", "arbitrary"),
    )(*operands)


def _rms_fwd(x, gain, name, after=None):
    l, d = x.shape
    tr = _tile(l, 272, 16)

    def body(x_ref, g_ref, *rest):
        o_ref = rest[-1]
        xv = x_ref[...]
        r = lax.rsqrt(jnp.mean(xv * xv, axis=-1, keepdims=True) + RMS_EPS)
        o_ref[...] = (xv * r * g_ref[...]).astype(BF16)

    return pl.pallas_call(
        body, name=name, grid=(l // tr,),
        in_specs=[pl.BlockSpec((tr, d), lambda i: (i, 0)), pl.BlockSpec((1, d), lambda i: (0, 0))] + _after_specs(after),
        out_specs=pl.BlockSpec((tr, d), lambda i: (i, 0)),
        out_shape=jax.ShapeDtypeStruct((l, d), BF16),
        compiler_params=_params("parallel"),
    )(x, gain, *_after_operands(after))


def _rms_bwd(dh, x, gain, gout, name):
    l, d = x.shape
    tr = _tile(l, 272, 16)

    def body(dh_ref, x_ref, g_ref, go_ref, gx_ref, gxb_ref, dg_ref):
        xv = x_ref[...]
        r = lax.rsqrt(jnp.mean(xv * xv, axis=-1, keepdims=True) + RMS_EPS)
        nv = xv * r
        dhv = dh_ref[...]
        dn = dhv * g_ref[...]
        dx = r * (dn - nv * jnp.mean(dn * nv, axis=-1, keepdims=True))
        gx = go_ref[...] + dx
        gx_ref[...] = gx
        gxb_ref[...] = gx.astype(BF16)
        part = jnp.sum(dhv * nv, axis=0, keepdims=True)

        @pl.when(pl.program_id(0) == 0)
        def _():
            dg_ref[...] = part

        @pl.when(pl.program_id(0) > 0)
        def _():
            dg_ref[...] += part

    row = pl.BlockSpec((tr, d), lambda i: (i, 0))
    vec = pl.BlockSpec((1, d), lambda i: (0, 0))
    return pl.pallas_call(
        body, name=name, grid=(l // tr,),
        in_specs=[row, row, vec, row], out_specs=[row, row, vec],
        out_shape=[jax.ShapeDtypeStruct((l, d), F32), jax.ShapeDtypeStruct((l, d), BF16),
                   jax.ShapeDtypeStruct((1, d), F32)],
        compiler_params=_params("arbitrary"),
    )(dh, x, gain, gout)


def _final_loss(x, gain, target, row_lo, row_hi, name):
    l, d = x.shape
    tr = _tile(l, 272, 16)

    def body(x_ref, g_ref, t_ref, gx_ref, gxb_ref, dg_ref, loss_ref):
        i = pl.program_id(0)
        xv = x_ref[...]
        r = lax.rsqrt(jnp.mean(xv * xv, axis=-1, keepdims=True) + RMS_EPS)
        nv = xv * r
        gv = g_ref[...]
        rows = i * tr + lax.broadcasted_iota(jnp.int32, (tr, 1), 0)
        valid = jnp.logical_and(rows >= row_lo, rows < row_hi)
        err = jnp.where(valid, nv * gv - t_ref[...], 0.0)
        dy = err * (1.0 / d)
        dn = dy * gv
        gx = r * (dn - nv * jnp.mean(dn * nv, axis=-1, keepdims=True))
        gx_ref[...] = gx
        gxb_ref[...] = gx.astype(BF16)
        part = jnp.sum(dy * nv, axis=0, keepdims=True)
        lpart = jnp.full((1, 128), 0.5 * jnp.sum(jnp.mean(err * err, axis=-1, keepdims=True)), F32)

        @pl.when(i == 0)
        def _():
            dg_ref[...] = part
            loss_ref[...] = lpart

        @pl.when(i > 0)
        def _():
            dg_ref[...] += part
            loss_ref[...] += lpart

    row = pl.BlockSpec((tr, d), lambda i: (i, 0))
    vec = pl.BlockSpec((1, d), lambda i: (0, 0))
    return pl.pallas_call(
        body, name=name, grid=(l // tr,),
        in_specs=[row, vec, row], out_specs=[row, row, vec, pl.BlockSpec((1, 128), lambda i: (0, 0))],
        out_shape=[jax.ShapeDtypeStruct((l, d), F32), jax.ShapeDtypeStruct((l, d), BF16),
                   jax.ShapeDtypeStruct((1, d), F32), jax.ShapeDtypeStruct((1, 128), F32)],
        compiler_params=_params("arbitrary"),
    )(x, gain, target)


def _log1m_beta(z):
    return -(jnp.maximum(z, 0.0) + jnp.log(1.0 + jnp.exp(-jnp.abs(z))))


def _tri(n, relation):
    r = lax.broadcasted_iota(jnp.int32, (n, n), 0)
    c = lax.broadcasted_iota(jnp.int32, (n, n), 1)
    return jnp.where(relation(r, c), 1.0, 0.0).astype(BF16)


def _head(hh):
    return slice(hh * HEAD_DIM, (hh + 1) * HEAD_DIM)


def _heads(x, hb):
    return jnp.stack([x[:, _head(hh)] for hh in range(hb)], axis=0)


def _bdot(a, b, ta=False, tb=False):
    dims = (((1 if ta else 2,), (2 if tb else 1,)), ((0,), (0,)))
    return lax.dot_general(a.astype(BF16), b.astype(BF16), dims, preferred_element_type=F32)


def _attn_specs(l, hb, n_heads):
    width = hb * HEAD_DIM
    groups = n_heads // hb

    def tile(section):
        return pl.BlockSpec((ATT_BLOCK, width), lambda h, i: (i, section * groups + h))

    def slab(section):
        return pl.BlockSpec((l, width), lambda h, i: (0, section * groups + h))

    return tile, slab


def _attn_fwd(zin, n_heads, name):
    l = zin.shape[0]
    t = ATT_BLOCK
    hb = ATT_HEADS if n_heads % ATT_HEADS == 0 else 1
    scale = HEAD_DIM ** -0.5

    def body(q_ref, k_ref, v_ref, g_ref, o_ref, oa_ref, tot_ref):
        i = pl.program_id(1)
        after = {w: _tri(w, lambda r, c: r > c) for w in (t, 2 * t, 4 * t)}
        causal = (lax.broadcasted_iota(jnp.int32, (t, t), 1) < lax.broadcasted_iota(jnp.int32, (t, t), 0))[None]
        q = _heads(q_ref[...], hb).astype(BF16)

        def tile(k0, w, carry, diagonal=False):
            run, acc = carry
            z = _bdot(q, _heads(k_ref[pl.ds(k0, w), :], hb), tb=True) * scale
            lb_all = _log1m_beta(z)
            lb = jnp.where(causal, lb_all, 0.0) if diagonal else lb_all
            between = _dot_split(lb.reshape(hb * t, w), after[w]).reshape(hb, t, w)
            a = jnp.exp(z + lb_all + between + run)
            if diagonal:
                a = jnp.where(causal, a, 0.0)
            return (run + jnp.sum(lb, axis=2, keepdims=True),
                    acc + _bdot(a, _heads(v_ref[pl.ds(k0, w), :], hb)))

        rem = i % 4
        carry = tile(pl.multiple_of(i * t, t), t, (jnp.zeros((hb, t, 1), F32), jnp.zeros((hb, t, HEAD_DIM), F32)), True)
        carry = lax.cond(rem % 2 == 1, lambda c: tile(pl.multiple_of((i - 1) * t, t), t, c), lambda c: c, carry)
        carry = lax.cond(rem >= 2, lambda c: tile(pl.multiple_of((i - rem % 2 - 2) * t, 2 * t), 2 * t, c),
                         lambda c: c, carry)
        run, o = lax.fori_loop(
            0, i // 4, lambda it, c: tile(pl.multiple_of((i - rem - 4 * it - 4) * t, 4 * t), 4 * t, c), carry)
        for hh in range(hb):
            gate = g_ref[:, _head(hh)]
            o_ref[:, _head(hh)] = o[hh]
            oa_ref[:, _head(hh)] = (o[hh] * (gate * _sigmoid(gate))).astype(BF16)
            tot_ref[:, _head(hh)] = jnp.broadcast_to(run[hh], (t, HEAD_DIM))

    tile_spec, slab_spec = _attn_specs(l, hb, n_heads)
    width = n_heads * HEAD_DIM
    return pl.pallas_call(
        body, name=name, grid=(n_heads // hb, l // t),
        in_specs=[tile_spec(0), slab_spec(1), slab_spec(2), tile_spec(3)],
        out_specs=[tile_spec(0), tile_spec(0), tile_spec(0)],
        out_shape=[jax.ShapeDtypeStruct((l, width), F32), jax.ShapeDtypeStruct((l, width), BF16),
                   jax.ShapeDtypeStruct((l, width), F32)],
        compiler_params=_params("parallel", "arbitrary"),
    )(zin, zin, zin, zin)


def _attn_bwd(zin, o, tot, doa, n_heads, name, after=None):
    l = zin.shape[0]
    t = ATT_BLOCK
    nq = l // t
    hb = ATT_HEADS if n_heads % ATT_HEADS == 0 else 1
    scale = HEAD_DIM ** -0.5

    def body(q_ref, k_ref, v_ref, g_ref, o_ref, tot_ref, doa_ref, *rest):
        dq_ref, dk_ref, dv_ref, dg_ref, dk_acc, dv_acc = rest[-6:]
        i = pl.program_id(1)

        @pl.when(i == 0)
        def _():
            dk_acc[...] = jnp.zeros_like(dk_acc)
            dv_acc[...] = jnp.zeros_like(dv_acc)

        upto = {w: _tri(w, lambda r, c: r <= c) for w in (t, 2 * t, 4 * t)}
        before = {w: _tri(w, lambda r, c: r < c) for w in (t, 2 * t, 4 * t)}
        causal = (lax.broadcasted_iota(jnp.int32, (t, t), 1) < lax.broadcasted_iota(jnp.int32, (t, t), 0))[None]
        gate = g_ref[...]
        sg = _sigmoid(gate)
        doav = doa_ref[...]
        dg_ref[...] = (doav * o_ref[...] * (sg * (1.0 + gate * (1.0 - sg)))).astype(BF16)
        do = _heads(doav * (gate * sg), hb).astype(BF16)
        q = _heads(q_ref[...], hb).astype(BF16)
        total = _heads(tot_ref[...], hb)[:, :, 0:1]

        def tile(k0, w, carry, diagonal=False):
            run, pre, dq = carry
            kb = _heads(k_ref[pl.ds(k0, w), :], hb).astype(BF16)
            z = _bdot(q, kb, tb=True) * scale
            lb_all = _log1m_beta(z)
            lb = jnp.where(causal, lb_all, 0.0) if diagonal else lb_all
            beta = jnp.exp(z + lb_all)
            a = beta * jnp.exp(total - run - _dot_split(lb.reshape(hb * t, w), upto[w]).reshape(hb, t, w))
            if diagonal:
                a = jnp.where(causal, a, 0.0)
            e = a * _bdot(do, _heads(v_ref[pl.ds(k0, w), :], hb), tb=True)
            prefix = pre + _dot(e.reshape(hb * t, w), before[w]).reshape(hb, t, w)
            dz = (e * (1.0 - beta) - beta * prefix) * scale
            if diagonal:
                dz = jnp.where(causal, dz, 0.0)
            dv = _bdot(a, do, ta=True)
            dk = _bdot(dz, q, ta=True)
            for hh in range(hb):
                dv_acc[pl.ds(k0, w), _head(hh)] += dv[hh]
                dk_acc[pl.ds(k0, w), _head(hh)] += dk[hh]
            return (run + jnp.sum(lb, axis=2, keepdims=True), pre + jnp.sum(e, axis=2, keepdims=True),
                    dq + _bdot(dz, kb))

        rem = i % 4
        zero = jnp.zeros((hb, t, 1), F32)
        carry = lax.fori_loop(0, i // 4, lambda j, c: tile(pl.multiple_of(j * 4 * t, 4 * t), 4 * t, c),
                              (zero, zero, jnp.zeros((hb, t, HEAD_DIM), F32)))
        carry = lax.cond(rem >= 2, lambda c: tile(pl.multiple_of((i - rem) * t, 2 * t), 2 * t, c), lambda c: c, carry)
        carry = lax.cond(rem % 2 == 1, lambda c: tile(pl.multiple_of((i - 1) * t, t), t, c), lambda c: c, carry)
        _, _, dq = tile(pl.multiple_of(i * t, t), t, carry, True)
        for hh in range(hb):
            dq_ref[:, _head(hh)] = dq[hh].astype(BF16)

        @pl.when(i == nq - 1)
        def _():
            dk_ref[...] = dk_acc[...].astype(BF16)
            dv_ref[...] = dv_acc[...].astype(BF16)

    tile_spec, slab_spec = _attn_specs(l, hb, n_heads)
    out = jax.ShapeDtypeStruct((l, n_heads * HEAD_DIM), BF16)
    return pl.pallas_call(
        body, name=name, grid=(n_heads // hb, nq),
        in_specs=[tile_spec(0), slab_spec(1), slab_spec(2), tile_spec(3), tile_spec(0), tile_spec(0), tile_spec(0)]
        + _after_specs(after),
        out_specs=[tile_spec(0), slab_spec(0), slab_spec(0), tile_spec(0)],
        out_shape=[out, out, out, out],
        scratch_shapes=[pltpu.VMEM((l, hb * HEAD_DIM), F32), pltpu.VMEM((l, hb * HEAD_DIM), F32)],
        compiler_params=_params("parallel", "arbitrary"),
    )(zin, zin, zin, zin, o, tot, doa, *_after_operands(after))


def _shift_rows(x, k, down):
    n = x.shape[0]
    rows = lax.broadcasted_iota(jnp.int32, x.shape, 0)
    if down:
        return jnp.where(rows >= k, pltpu.roll(x, k, 0), 0.0)
    return jnp.where(rows < n - k, pltpu.roll(x, n - k, 0), 0.0)


def _window_sum(x, g, down):
    result = x
    total = x
    for step, k in enumerate((1, 2, 4, 8)):
        total = total + _shift_rows(total, k, down)
        result = jnp.where(g >= step, total, result)
    return result


def _pooled(u, g):
    rows = lax.broadcasted_iota(jnp.int32, (u.shape[0], 1), 0)
    window = jnp.left_shift(2, g)
    cnt = jnp.minimum(rows + 1, window).astype(F32)
    return _window_sum(u, g, True) / cnt - u, cnt


def _pool_fwd(zin, pool_w, pool_scale, u_off, name, after=None):
    l = zin.shape[0]
    n_groups, gd, _ = pool_w.shape

    def body(u_ref, g_ref, w_ref, s_ref, *rest):
        o_ref = rest[-1]
        g = pl.program_id(0)
        pooled, _ = _pooled(u_ref[...], g)
        mixed = _dot(pooled, w_ref[...])
        gate = g_ref[...]
        o_ref[...] = (mixed * s_ref[...] * (gate * _sigmoid(gate))).astype(BF16)

    return pl.pallas_call(
        body, name=name, grid=(n_groups,),
        in_specs=[pl.BlockSpec((l, gd), lambda g: (0, u_off + g)),
                  pl.BlockSpec((l, gd), lambda g: (0, u_off + n_groups + g)),
                  pl.BlockSpec((None, gd, gd), lambda g: (g, 0, 0)),
                  pl.BlockSpec((1, gd), lambda g: (0, g))] + _after_specs(after),
        out_specs=pl.BlockSpec((l, gd), lambda g: (0, g)),
        out_shape=jax.ShapeDtypeStruct((l, n_groups * gd), BF16),
        compiler_params=_params("parallel"),
    )(zin, zin, pool_w, pool_scale, *_after_operands(after))


def _pool_bwd(zin, dop, pool_w, pool_scale, u_off, name):
    l = zin.shape[0]
    n_groups, gd, _ = pool_w.shape

    def body(u_ref, g_ref, w_ref, s_ref, d_ref, du_ref, dg_ref, dw_ref, ds_ref):
        g = pl.program_id(0)
        pooled, cnt = _pooled(u_ref[...], g)
        w = w_ref[...]
        mixed = _dot(pooled, w)
        gate = g_ref[...]
        sg = _sigmoid(gate)
        silu = gate * sg
        dop_v = d_ref[...]
        sc = s_ref[...]
        ds_ref[...] = jnp.sum(dop_v * mixed * silu, axis=0, keepdims=True)
        dg_ref[...] = (dop_v * mixed * sc * (sg * (1.0 + gate * (1.0 - sg)))).astype(BF16)
        dmixed = dop_v * sc * silu
        dw_ref[...] = _dot(pooled, dmixed, ta=True).astype(BF16)
        dpooled = _dot(dmixed, w, tb=True)
        du_ref[...] = (_window_sum(dpooled / cnt, g, False) - dpooled).astype(BF16)

    slab = pl.BlockSpec((l, gd), lambda g: (0, g))
    return pl.pallas_call(
        body, name=name, grid=(n_groups,),
        in_specs=[pl.BlockSpec((l, gd), lambda g: (0, u_off + g)),
                  pl.BlockSpec((l, gd), lambda g: (0, u_off + n_groups + g)),
                  pl.BlockSpec((None, gd, gd), lambda g: (g, 0, 0)),
                  pl.BlockSpec((1, gd), lambda g: (0, g)), slab],
        out_specs=[slab, slab, pl.BlockSpec((None, gd, gd), lambda g: (g, 0, 0)),
                   pl.BlockSpec((1, gd), lambda g: (0, g))],
        out_shape=[jax.ShapeDtypeStruct((l, n_groups * gd), BF16), jax.ShapeDtypeStruct((l, n_groups * gd), BF16),
                   jax.ShapeDtypeStruct(pool_w.shape, BF16), jax.ShapeDtypeStruct((1, n_groups * gd), F32)],
        compiler_params=_params("parallel"),
    )(zin, zin, pool_w, pool_scale, dop)


def _merge_fwd(oa, op, w_au, w_pu, zin, name):
    l, wa = oa.shape
    wp = op.shape[1]
    d = w_au.shape[1]
    tm = _tile(l, 1088, 16)
    tn = _tile(d, 512, 128)
    ma_off = (zin.shape[1] - 2 * d) // tn

    def body(oa_ref, op_ref, wa_ref, wp_ref, ma_ref, mp_ref, ya_ref, yp_ref, mg_ref):
        ya = _dot(oa_ref[...], wa_ref[...])
        yp = _dot(op_ref[...], wp_ref[...])
        ya_ref[...] = ya.astype(BF16)
        yp_ref[...] = yp.astype(BF16)
        mg_ref[...] = (_sigmoid(ma_ref[...]) * ya + _sigmoid(mp_ref[...]) * yp).astype(BF16)

    tile = pl.BlockSpec((tm, tn), lambda i, j: (i, j))
    return pl.pallas_call(
        body, name=name, grid=(l // tm, d // tn),
        in_specs=[pl.BlockSpec((tm, wa), lambda i, j: (i, 0)), pl.BlockSpec((tm, wp), lambda i, j: (i, 0)),
                  pl.BlockSpec((wa, tn), lambda i, j: (0, j)), pl.BlockSpec((wp, tn), lambda i, j: (0, j)),
                  pl.BlockSpec((tm, tn), lambda i, j: (i, ma_off + j)),
                  pl.BlockSpec((tm, tn), lambda i, j: (i, ma_off + d // tn + j))],
        out_specs=[tile, tile, tile],
        out_shape=[jax.ShapeDtypeStruct((l, d), BF16)] * 3,
        compiler_params=_params("parallel", "parallel"),
    )(oa, op, w_au, w_pu, zin, zin)


def _merge_bwd(gout, w_out, zin, ya, yp, name):
    l, d = gout.shape
    tm = _tile(l, 1088, 16)
    tn = _tile(d, 512, 128)
    ma_off = (zin.shape[1] - 2 * d) // tn

    def body(g_ref, w_ref, ma_ref, mp_ref, ya_ref, yp_ref, dya_ref, dyp_ref, dma_ref, dmp_ref):
        dm = _dot(g_ref[...], w_ref[...], tb=True)
        sa = _sigmoid(ma_ref[...])
        sp = _sigmoid(mp_ref[...])
        dya_ref[...] = (dm * sa).astype(BF16)
        dyp_ref[...] = (dm * sp).astype(BF16)
        dma_ref[...] = (dm * ya_ref[...].astype(F32) * (sa * (1.0 - sa))).astype(BF16)
        dmp_ref[...] = (dm * yp_ref[...].astype(F32) * (sp * (1.0 - sp))).astype(BF16)

    tile = pl.BlockSpec((tm, tn), lambda i, j: (i, j))
    out = jax.ShapeDtypeStruct((l, d), BF16)
    return pl.pallas_call(
        body, name=name, grid=(l // tm, d // tn),
        in_specs=[pl.BlockSpec((tm, d), lambda i, j: (i, 0)), pl.BlockSpec((tn, d), lambda i, j: (j, 0)),
                  pl.BlockSpec((tm, tn), lambda i, j: (i, ma_off + j)),
                  pl.BlockSpec((tm, tn), lambda i, j: (i, ma_off + d // tn + j)), tile, tile],
        out_specs=[tile, tile, tile, tile],
        out_shape=[out, out, out, out],
        compiler_params=_params("parallel", "parallel"),
    )(gout, w_out, zin, zin, ya, yp)


def _adamw(parts, w, m, v, name, first=0, into=None):
    n_arrays, n_parts, r, c = parts.shape
    tr = _tile(r, max(8, (128 * 1024) // c // 8 * 8), 8)
    bias1 = 1.0 - ADAM_B1 ** ADAM_STEP
    bias2 = 1.0 - ADAM_B2 ** ADAM_STEP

    def body(p_ref, w_ref, m_ref, v_ref, *rest):
        g_ref, d_ref, nm_ref, nv_ref = rest[-4:]
        g = p_ref[0].astype(F32)
        for j in range(1, n_parts):
            g = g + p_ref[j].astype(F32)
        nm = ADAM_B1 * m_ref[...] + (1.0 - ADAM_B1) * g
        nv = ADAM_B2 * v_ref[...] + (1.0 - ADAM_B2) * (g * g)
        g_ref[...] = g
        nm_ref[...] = nm
        nv_ref[...] = nv
        d_ref[...] = -ADAM_LR * ((nm / bias1) / (jnp.sqrt(nv / bias2) + ADAM_EPS) + ADAM_WD * w_ref[...])

    tile = pl.BlockSpec((None, tr, c), lambda a, i: (a + first, i, 0))
    out = jax.ShapeDtypeStruct(w.shape, F32)
    kept = [] if into is None else list(into)
    return pl.pallas_call(
        body, name=name, grid=(n_arrays, r // tr),
        in_specs=[pl.BlockSpec((None, n_parts, tr, c), lambda a, i: (a, 0, i, 0)), tile, tile, tile]
        + [pl.BlockSpec(memory_space=pl.ANY)] * len(kept),
        out_specs=[tile, tile, tile, tile], out_shape=[out, out, out, out],
        input_output_aliases={4 + k: k for k in range(len(kept))},
        compiler_params=_params("parallel", "parallel"),
    )(parts, w, m, v, *kept)


def _position():
    return lax.axis_index("x"), lax.axis_index("y"), lax.axis_index("c")


def _block_of(ref, axis, size, index):
    idx = [slice(None)] * len(ref.shape)
    idx[axis] = pl.ds(index * size, size)
    return ref.at[tuple(idx)]


HBM_SPEC = pl.BlockSpec(memory_space=pltpu.HBM)
SEM_SPEC = pl.BlockSpec(memory_space=pltpu.SEMAPHORE)
SIDE_EFFECT = pltpu.CompilerParams(has_side_effects=pltpu.SideEffectType.DATAFLOW_SIDE_EFFECTING)


def _split_start(make_copies, n_copies, buffers, name):
    n = len(buffers)

    def body(*refs):
        send_sems, recv_sems = refs[n], refs[n + 1]
        for cp in make_copies(refs[:n], send_sems, recv_sems):
            cp.start()
        refs[-1][...] = jnp.zeros_like(refs[-1])

    sems = pltpu.SemaphoreType.DMA((n_copies,))
    return pl.pallas_call(
        body, name=name, in_specs=[HBM_SPEC] * n,
        out_shape=(sems, sems, *[pltpu.HBM(b.shape, b.dtype) for b in buffers], jax.ShapeDtypeStruct((8, 128), F32)),
        out_specs=(SEM_SPEC, SEM_SPEC, *[HBM_SPEC] * n, pl.BlockSpec(memory_space=pltpu.VMEM)),
        input_output_aliases={i: 2 + i for i in range(n)}, compiler_params=SIDE_EFFECT,
    )(*[pltpu.with_memory_space_constraint(b, pltpu.HBM) for b in buffers])


def _split_wait(make_copies, started, after, name):
    send_sems, recv_sems, *buffers = started[:-1]
    n = len(buffers)

    def body(*refs):
        copies = make_copies(refs[:n], refs[n], refs[n + 1])
        for cp in copies:
            cp.wait_send()
        for cp in copies:
            cp.wait_recv()

    return pl.pallas_call(
        body, name=name, in_specs=[HBM_SPEC] * n + [SEM_SPEC, SEM_SPEC, pl.BlockSpec(memory_space=pl.ANY)],
        out_shape=[pltpu.HBM(b.shape, b.dtype) for b in buffers], out_specs=[HBM_SPEC] * n,
        input_output_aliases={i: i for i in range(n)}, compiler_params=SIDE_EFFECT,
    )(*buffers, send_sems, recv_sems, after)


def _gather_copies(axes, sizes, level):
    def make(fulls, send_sems, recv_sems):
        x, y, c = _position()
        chips = [(1 - x, y), (x, 1 - y), (1 - x, 1 - y)]
        copies = []
        for a, full in enumerate(fulls):
            def copy(k, block, to, full=full, a=a):
                rows = _block_of(full, axes[a], sizes[a], 4 * block[0] + 2 * block[1] + block[2])
                idx = a * (4 if level == 1 else 3) + k
                return pltpu.make_async_remote_copy(src_ref=rows, dst_ref=rows, send_sem=send_sems.at[idx],
                                                    recv_sem=recv_sems.at[idx], device_id=to, device_id_type=MESH)
            if level == 1:
                copies.append(copy(0, (x, y, c), (x, y, 1 - c)))
                copies += [copy(1 + j, (x, y, c), (*chip, c)) for j, chip in enumerate(chips)]
            else:
                copies += [copy(j, (*chip, c), (x, y, 1 - c)) for j, chip in enumerate(chips)]
        return copies
    return make


def _exchange_copies(axes, sizes, layer, n_src):
    flips = [(a, b, d) for a in (0, 1) for b in (0, 1) for d in (0, 1)][1:]

    def make(buffers, send_sems, recv_sems):
        x, y, c = _position()
        my_index = 4 * x + 2 * y + c
        copies = []
        for a in range(n_src):
            for k, flip in enumerate(flips):
                px, py, pc = x ^ flip[0], y ^ flip[1], c ^ flip[2]
                copies.append(pltpu.make_async_remote_copy(
                    src_ref=_block_of(buffers[a], axes[a], sizes[a], 4 * px + 2 * py + pc),
                    dst_ref=buffers[n_src + a].at[layer, my_index],
                    send_sem=send_sems.at[a * 7 + k], recv_sem=recv_sems.at[a * 7 + k],
                    device_id=(px, py, pc), device_id_type=MESH))
        return copies
    return make


def _allgather_small(v, name, after=None):
    r, c = v.shape
    flips = [(a, b, d) for a in (0, 1) for b in (0, 1) for d in (0, 1)][1:]

    def body(v_ref, *rest):
        out_ref, send_sems, recv_sems = rest[-3:]
        x, y, c_ = _position()
        my_index = 4 * x + 2 * y + c_
        out_ref[my_index] = v_ref[...]
        sends = []
        for k, flip in enumerate(flips):
            peer = (x ^ flip[0], y ^ flip[1], c_ ^ flip[2])
            cp = pltpu.make_async_remote_copy(
                src_ref=v_ref, dst_ref=out_ref.at[my_index],
                send_sem=send_sems.at[k], recv_sem=recv_sems.at[k], device_id=peer, device_id_type=MESH)
            cp.start()
            sends.append(cp)
        for k, flip in enumerate(flips):
            px, py, pc = x ^ flip[0], y ^ flip[1], c_ ^ flip[2]
            pltpu.make_async_remote_copy(
                src_ref=v_ref, dst_ref=out_ref.at[4 * px + 2 * py + pc],
                send_sem=send_sems.at[k], recv_sem=recv_sems.at[k],
                device_id=(px, py, pc), device_id_type=MESH).wait_recv()
        for cp in sends:
            cp.wait_send()

    return pl.pallas_call(
        body, name=name,
        in_specs=[pl.BlockSpec(memory_space=pltpu.VMEM)] + _after_specs(after),
        out_specs=pl.BlockSpec(memory_space=pltpu.VMEM),
        out_shape=jax.ShapeDtypeStruct((N_DEV, r, c), v.dtype),
        scratch_shapes=[pltpu.SemaphoreType.DMA((7,)), pltpu.SemaphoreType.DMA((7,))],
    )(v, *_after_operands(after))


def kernel(x, meta_tokens, norm_gain, w_in, pool_w, pool_scale, w_attn_up, w_pool_up, w_out, final_gain, loss_target, m_meta_tokens, m_norm_gain, m_w_in, m_pool_w, m_pool_scale, m_w_attn_up, m_w_pool_up, m_w_out, m_final_gain, v_meta_tokens, v_norm_gain, v_w_in, v_pool_w, v_pool_scale, v_w_attn_up, v_w_pool_up, v_w_out, v_final_gain):
    _, seq, d = x.shape
    n_meta = meta_tokens.shape[0]
    depth = w_in.shape[0]
    sb_width = w_attn_up.shape[1]
    pool_width = w_pool_up.shape[1]
    n_heads = sb_width // HEAD_DIM
    n_groups = pool_w.shape[1]
    gd = pool_w.shape[3]
    assert n_groups == len(POOL_WINDOWS) and gd * n_groups == pool_width
    assert w_in.shape[2] * N_DEV == 4 * sb_width + 2 * pool_width + 2 * d
    l_real = n_meta + seq
    l_pad = -(-l_real // ATT_BLOCK) * ATT_BLOCK
    my_index = 4 * lax.axis_index("x") + 2 * lax.axis_index("y") + lax.axis_index("c")

    me = jnp.reshape(my_index, (1,)).astype(jnp.int32)
    g_named = [("w_in", w_in), ("pool_w", pool_w), ("w_attn_up", w_attn_up), ("w_pool_up", w_pool_up), ("w_out", w_out)]
    g_axes = [1, 1, 1, 1, 0]
    g_sizes = [w.shape[1 + ax] for (_, w), ax in zip(g_named, g_axes)]
    level1 = _gather_copies(g_axes, g_sizes, 1)
    level2 = _gather_copies(g_axes, g_sizes, 2)

    def gather_start(i, after=None):
        fulls = [_cast_place(w, i, ax, me, "cast_" + nm, after) for (nm, w), ax in zip(g_named, g_axes)]
        return _split_start(level1, 4 * len(fulls), fulls, "gather1_start_%d" % i)

    def gather_forward(i, started, after):
        arrived = _split_wait(level1, started, after, "gather1_wait_%d" % i)
        return _split_start(level2, 3 * len(arrived), arrived, "gather2_start_%d" % i)

    meta_all = _allgather_small(meta_tokens, "allgather_meta")
    meta_full = jnp.transpose(meta_all, (1, 0, 2)).reshape(n_meta, d)
    first = gather_start(0)
    second = gather_forward(0, first, first[-1])

    pad_rows = l_pad - l_real
    hs = jnp.concatenate([meta_full, x[0], jnp.zeros((pad_rows, d), F32)], axis=0)
    target = jnp.concatenate([jnp.zeros((n_meta, d), F32), loss_target[0], jnp.zeros((pad_rows, d), F32)], axis=0)
    u_off = 4 * sb_width // gd
    saved, weights = [], []
    for i in range(depth):
        h = _rms_fwd(hs, norm_gain[i][None], "rms_fwd")
        weights.append(_split_wait(level2, second, h, "gather2_wait_%d" % i))
        wi, pw, wau, wpu, wo = weights[i]
        more = i + 1 < depth
        first = gather_start(i + 1, wi) if more else None
        zin = _matmul(h, wi, after=first[-1] if more else None, name="mm_zin")
        o, oa, tot = _attn_fwd(zin, n_heads, "attn_fwd")
        op = _pool_fwd(zin, pw, pool_scale[i][None], u_off, "pool_fwd")
        ya, yp, merged = _merge_fwd(oa, op, wau, wpu, zin, "merge_fwd")
        second = gather_forward(i + 1, first, merged) if more else None
        saved.append((hs, h, zin, o, tot, oa, op, ya, yp, merged))
        hs = _matmul(merged, wo, add=hs, after=second[-1] if more else None, name="mm_out")
    g, gb, d_final_gain, loss_part = _final_loss(hs, final_gain[None], target, n_meta, l_real, "final_loss")
    loss = lax.psum(loss_part[0, 0], ("x", "y", "c"))

    d_norm_gain = [None] * depth
    d_pool_scale = [None] * depth
    axes_a, axes_b = [1, 1, 0], [1, 1]
    blocks_a, blocks_b = [w_attn_up[0], w_pool_up[0], w_out[0]], [w_in[0], pool_w[0]]
    sizes_a = [b.shape[ax] for b, ax in zip(blocks_a, axes_a)]
    sizes_b = [b.shape[ax] for b, ax in zip(blocks_b, axes_b)]
    land_a = [lax.empty((depth, N_DEV, *b.shape), BF16) for b in blocks_a]
    land_b = [lax.empty((depth - 1, N_DEV, *b.shape), BF16) for b in blocks_b]
    land_b0 = [lax.empty((1, N_DEV, *b.shape), BF16) for b in blocks_b]

    def exchange_start(grads, landing, axes, sizes, slot, name):
        landing = [lax.dynamic_update_slice(
            zone, lax.dynamic_slice_in_dim(grad, my_index * size, size, ax)[None, None],
            (slot, my_index) + (0,) * grad.ndim) for zone, grad, ax, size in zip(landing, grads, axes, sizes)]
        copies = _exchange_copies(axes, sizes, slot, len(grads))
        return copies, _split_start(copies, 7 * len(grads), list(grads) + landing, name)

    def exchange_wait(pending, n_src, after, name):
        return _split_wait(pending[0], pending[1], after, name)[n_src:]

    pend_a = pend_b = None
    for i in reversed(range(depth)):
        wi, pw, wau, wpu, wo = weights[i]
        hs_in, h, zin, o, tot, oa, op, ya, yp, merged = saved[i]
        dya, dyp, dma, dmp = _merge_bwd(gb, wo, zin, ya, yp, "merge_bwd")
        dw_out = _matmul(merged, gb, ta=True, out_dtype=BF16, name="mm_dw_out", tm=1024, tk=1088)
        doa = _matmul(dya, wau, tb=True, name="mm_doa")
        dw_au = _matmul(oa, dya, ta=True, out_dtype=BF16, name="mm_dw_au", tm=1024, tk=1088)
        dop = _matmul(dyp, wpu, tb=True, name="mm_dop")
        dw_pu = _matmul(op, dyp, ta=True, out_dtype=BF16, name="mm_dw_pu", tm=1024, tk=1088)
        if pend_a is not None:
            land_a = exchange_wait(pend_a, 3, dw_pu, "exchange_a_wait_%d" % (i + 1))
        pend_a = exchange_start([dw_au, dw_pu, dw_out], land_a, axes_a, sizes_a, i, "exchange_a_start_%d" % i)
        dq, dk, dv, dga = _attn_bwd(zin, o, tot, doa, n_heads, "attn_bwd", after=pend_a[1][-1])
        du, dgp, dpw, dps = _pool_bwd(zin, dop, pw, pool_scale[i][None], u_off, "pool_bwd")
        dzin = [dq, dk, dv, dga, du, dgp, dma, dmp]
        dw_in = _matmul(h, dzin, ta=True, out_dtype=BF16, name="mm_dw_in", tm=1024, tk=1088)
        if pend_b is not None:
            land_b = exchange_wait(pend_b, 2, dw_in, "exchange_b_wait_%d" % (i + 1))
        pend_b = exchange_start([dw_in, dpw], land_b if i > 0 else land_b0, axes_b, sizes_b, max(i - 1, 0),
                                "exchange_b_start_%d" % i)
        dh = _matmul(dzin, wi, tb=True, after=pend_b[1][-1], name="mm_dh")
        g, gb, dng = _rms_bwd(dh, hs_in, norm_gain[i][None], g, "rms_bwd")
        d_norm_gain[i] = dng
        d_pool_scale[i] = dps
    land_a = exchange_wait(pend_a, 3, gb, "exchange_a_wait_0")
    grad_x = g[n_meta:l_real][None]

    def sharded(parts, w, m, v, name, first=0, into=None):
        flat = (depth, -1, w.shape[-1])
        return _adamw(parts.reshape(parts.shape[0], N_DEV, -1, w.shape[-1]), w.reshape(flat), m.reshape(flat),
                      v.reshape(flat), name, first, into)

    out_au = sharded(land_a[0], w_attn_up, m_w_attn_up, v_w_attn_up, "adamw_w_attn_up")
    out_pu = sharded(land_a[1], w_pool_up, m_w_pool_up, v_w_pool_up, "adamw_w_pool_up")
    out_wo = sharded(land_a[2], w_out, m_w_out, v_w_out, "adamw_w_out")
    out_wi = out_pw = None
    if depth > 1:
        out_wi = sharded(land_b[0], w_in, m_w_in, v_w_in, "adamw_w_in", 1)
        out_pw = sharded(land_b[1], pool_w, m_pool_w, v_pool_w, "adamw_pool_w", 1)
    land_b0 = exchange_wait(pend_b, 2, out_wo[0] if out_wi is None else out_wi[0], "exchange_b_wait_0")
    out_wi = sharded(land_b0[0], w_in, m_w_in, v_w_in, "adamw_w_in_0", 0, out_wi)
    out_pw = sharded(land_b0[1], pool_w, m_pool_w, v_pool_w, "adamw_pool_w_0", 0, out_pw)
    out_wi, out_pw, out_au, out_pu, out_wo = [
        [t.reshape(w.shape) for t in res] for res, w in
        zip([out_wi, out_pw, out_au, out_pu, out_wo], [w_in, pool_w, w_attn_up, w_pool_up, w_out])]

    zeros_ps = jnp.zeros((depth, d - pool_width), F32)
    small_rows = [jnp.concatenate(d_norm_gain, axis=0),
                  jnp.concatenate([jnp.concatenate(d_pool_scale, axis=0), zeros_ps], axis=1), d_final_gain]
    n_small = 2 * depth + 1
    small_pad = -(-n_small // 8) * 8
    small = jnp.concatenate(small_rows + [jnp.zeros((small_pad - n_small, d), F32), g[:n_meta]], axis=0)
    small_all = _allgather_small(small, "allgather_small", after=land_b0[0])

    def replicated(rows, width, w, m, v, name):
        parts = lax.slice(small_all, (0, rows[0], 0), (N_DEV, rows[1], width))
        return [t[0] for t in _adamw(parts[None], w[None], m[None], v[None], name)]

    out_ng = replicated((0, depth), d, norm_gain, m_norm_gain, v_norm_gain, "adamw_norm_gain")
    out_ps = replicated((depth, 2 * depth), pool_width, pool_scale, m_pool_scale, v_pool_scale, "adamw_pool_scale")
    out_fg = [t[0] for t in replicated((2 * depth, 2 * depth + 1), d, final_gain[None], m_final_gain[None],
                                       v_final_gain[None], "adamw_final_gain")]
    cols = d // N_DEV
    meta_parts = lax.dynamic_slice(small_all, (0, small_pad, my_index * cols), (N_DEV, n_meta, cols))
    out_meta = [t[0] for t in _adamw(meta_parts[None], meta_tokens[None], m_meta_tokens[None], v_meta_tokens[None],
                                     "adamw_meta")]

    by_weight = [out_meta, out_ng, out_wi, out_pw, out_ps, out_au, out_pu, out_wo, out_fg]
    return (loss, grad_x, *[o[0] for o in by_weight], *[o[1] for o in by_weight],
            *[o[2] for o in by_weight], *[o[3] for o in by_weight])
```

```python
import jax
import jax.numpy as jnp
from jax import lax
from jax.experimental import pallas as pl
from jax.experimental.pallas import tpu as pltpu

F32 = jnp.float32
BF16 = jnp.bfloat16
MESH = pl.DeviceIdType.MESH

N_DEV = 8
HEAD_DIM = 128
ATT_BLOCK = 128
ATT_HEADS = 4
POOL_WINDOWS = (2, 4, 8, 16)
RMS_EPS = 1e-6
ADAM_LR, ADAM_B1, ADAM_B2, ADAM_EPS, ADAM_WD, ADAM_STEP = 0.001, 0.9, 0.999, 1e-08, 0.01, 10
VMEM_LIMIT_BYTES = 56 * 1024 * 1024


def _tile(n, target, mult):
    if n <= target:
        return n
    best = 0
    for t in range(mult, target + 1, mult):
        if n % t == 0:
            best = t
    assert best > 0, (n, target, mult)
    return best


def _params(*sem):
    return pltpu.CompilerParams(dimension_semantics=sem, vmem_limit_bytes=VMEM_LIMIT_BYTES)


def _sigmoid(x):
    return 1.0 / (1.0 + jnp.exp(-x))


def _dot(a, b, ta=False, tb=False):
    dims = (((0 if ta else 1,), (1 if tb else 0,)), ((), ()))
    return lax.dot_general(a.astype(BF16), b.astype(BF16), dims, preferred_element_type=F32)


def _dot_split(a, b):
    hi = a.astype(BF16)
    lo = (a - hi.astype(F32)).astype(BF16)
    return _dot(hi, b) + _dot(lo, b)


def _cast_place(x, layer, axis, me, name, after=None):
    blk = x.shape[1:]
    full = list(blk)
    full[axis] *= N_DEV
    if len(blk) == 2:
        r, c = blk
        tr = _tile(r, max(16, (512 * 1024) // c // 16 * 16), 16)
        steps = r // tr
        in_spec = pl.BlockSpec((None, tr, c), lambda i, me_ref: (layer, i, 0))
        if axis == 1:
            out_spec = pl.BlockSpec((tr, c), lambda i, me_ref: (i, me_ref[0]))
        else:
            out_spec = pl.BlockSpec((tr, c), lambda i, me_ref: (me_ref[0] * steps + i, 0))
    else:
        assert len(blk) == 3 and axis == 1
        steps = 1
        in_spec = pl.BlockSpec((None, *blk), lambda i, me_ref: (layer, 0, 0, 0))
        out_spec = pl.BlockSpec(blk, lambda i, me_ref: (0, me_ref[0], 0))

    def body(me_ref, x_ref, *rest):
        rest[-1][...] = x_ref[...].astype(BF16)

    return pl.pallas_call(
        body, name=name,
        grid_spec=pltpu.PrefetchScalarGridSpec(num_scalar_prefetch=1, grid=(steps,),
                                               in_specs=[in_spec] + _after_specs(after), out_specs=out_spec),
        out_shape=jax.ShapeDtypeStruct(tuple(full), BF16),
        compiler_params=_params("arbitrary"),
    )(me, x, *_after_operands(after))


def _after_operands(after):
    return [] if after is None else [after]


def _after_specs(after):
    return [] if after is None else [pl.BlockSpec(memory_space=pl.ANY)]


def _matmul(a, b, *, ta=False, tb=False, out_dtype=F32, add=None, after=None, name, tm=1088, tn=1024, tk=2048):
    a_list = list(a) if isinstance(a, (list, tuple)) else None
    b_list = list(b) if isinstance(b, (list, tuple)) else None
    assert not (a_list and ta) and not (b_list and tb) and not (a_list and b_list)
    if a_list:
        m, k = a_list[0].shape[0], sum(p.shape[1] for p in a_list)
    else:
        m, k = (a.shape[1], a.shape[0]) if ta else a.shape
    if b_list:
        n = sum(p.shape[1] for p in b_list)
    else:
        n = b.shape[0] if tb else b.shape[1]
    tm = _tile(m, tm, 128 if ta else 16)
    tn = _tile(n, tn, 128)
    tk = _tile(k, tk, 128 if (not ta or tb) else 16)
    if a_list:
        while any(p.shape[1] % tk for p in a_list):
            tk //= 2
    if b_list:
        while any(p.shape[1] % tn for p in b_list):
            tn //= 2
    nk = k // tk

    def ranges(pieces, tile):
        out, start = [], 0
        for p in pieces:
            out.append((start, start + p.shape[1] // tile))
            start = out[-1][1]
        return out

    if a_list:
        a_ranges = ranges(a_list, tk)
        a_specs = [pl.BlockSpec((tm, tk), lambda i, j, kk, s=s, e=e: (i, jnp.clip(kk - s, 0, e - s - 1)))
                   for s, e in a_ranges]
    else:
        a_specs = [pl.BlockSpec((tk, tm), lambda i, j, kk: (kk, i)) if ta
                   else pl.BlockSpec((tm, tk), lambda i, j, kk: (i, kk))]
    if b_list:
        b_ranges = ranges(b_list, tn)
        b_specs = [pl.BlockSpec((tk, tn), lambda i, j, kk, s=s, e=e: (
            jnp.where(jnp.logical_and(j >= s, j < e), kk, 0), jnp.clip(j - s, 0, e - s - 1))) for s, e in b_ranges]
    else:
        b_specs = [pl.BlockSpec((tn, tk), lambda i, j, kk: (j, kk)) if tb
                   else pl.BlockSpec((tk, tn), lambda i, j, kk: (kk, j))]
    o_spec = pl.BlockSpec((tm, tn), lambda i, j, kk: (i, j))
    has_add = add is not None
    na, nb = len(a_specs), len(b_specs)
    simple = nk == 1 and not a_list and not b_list

    def body(*refs):
        a_refs, b_refs = refs[:na], refs[na:na + nb]
        add_ref = refs[na + nb] if has_add else None
        o_ref, acc_ref = refs[-2], refs[-1]
        j, kk = pl.program_id(1), pl.program_id(2)

        def finish(total):
            if has_add:
                total = total + add_ref[...]
            o_ref[...] = total.astype(out_dtype)

        if simple:
            finish(_dot(a_refs[0][...], b_refs[0][...], ta, tb))
            return

        @pl.when(kk == 0)
        def _():
            acc_ref[...] = jnp.zeros_like(acc_ref)

        if a_list:
            for (s, e), a_ref in zip(a_ranges, a_refs):
                @pl.when(jnp.logical_and(kk >= s, kk < e))
                def _(a_ref=a_ref):
                    acc_ref[...] += _dot(a_ref[...], b_refs[0][...], ta, tb)
        elif b_list:
            for (s, e), b_ref in zip(b_ranges, b_refs):
                @pl.when(jnp.logical_and(j >= s, j < e))
                def _(b_ref=b_ref):
                    acc_ref[...] += _dot(a_refs[0][...], b_ref[...], ta, tb)
        else:
            acc_ref[...] += _dot(a_refs[0][...], b_refs[0][...], ta, tb)

        @pl.when(kk == nk - 1)
        def _():
            finish(acc_ref[...])

    operands = (a_list or [a]) + (b_list or [b]) + ([add] if has_add else []) + _after_operands(after)
    in_specs = a_specs + b_specs + ([o_spec] if has_add else []) + _after_specs(after)
    return pl.pallas_call(
        body, name=name, grid=(m // tm, n // tn, nk),
        in_specs=in_specs, out_specs=o_spec,
        out_shape=jax.ShapeDtypeStruct((m, n), out_dtype),
        scratch_shapes=[pltpu.VMEM((8, 128) if simple else (tm, tn), F32)],
        compiler_params=_params("parallel", "parallel", "arbitrary"),
    )(*operands)


def _rms_fwd(x, gain, name, after=None):
    l, d = x.shape
    tr = _tile(l, 272, 16)

    def body(x_ref, g_ref, *rest):
        o_ref = rest[-1]
        xv = x_ref[...]
        r = lax.rsqrt(jnp.mean(xv * xv, axis=-1, keepdims=True) + RMS_EPS)
        o_ref[...] = (xv * r * g_ref[...]).astype(BF16)

    return pl.pallas_call(
        body, name=name, grid=(l // tr,),
        in_specs=[pl.BlockSpec((tr, d), lambda i: (i, 0)), pl.BlockSpec((1, d), lambda i: (0, 0))] + _after_specs(after),
        out_specs=pl.BlockSpec((tr, d), lambda i: (i, 0)),
        out_shape=jax.ShapeDtypeStruct((l, d), BF16),
        compiler_params=_params("parallel"),
    )(x, gain, *_after_operands(after))


def _rms_bwd(dh, x, gain, gout, name):
    l, d = x.shape
    tr = _tile(l, 272, 16)

    def body(dh_ref, x_ref, g_ref, go_ref, gx_ref, gxb_ref, dg_ref):
        xv = x_ref[...]
        r = lax.rsqrt(jnp.mean(xv * xv, axis=-1, keepdims=True) + RMS_EPS)
        nv = xv * r
        dhv = dh_ref[...]
        dn = dhv * g_ref[...]
        dx = r * (dn - nv * jnp.mean(dn * nv, axis=-1, keepdims=True))
        gx = go_ref[...] + dx
        gx_ref[...] = gx
        gxb_ref[...] = gx.astype(BF16)
        part = jnp.sum(dhv * nv, axis=0, keepdims=True)

        @pl.when(pl.program_id(0) == 0)
        def _():
            dg_ref[...] = part

        @pl.when(pl.program_id(0) > 0)
        def _():
            dg_ref[...] += part

    row = pl.BlockSpec((tr, d), lambda i: (i, 0))
    vec = pl.BlockSpec((1, d), lambda i: (0, 0))
    return pl.pallas_call(
        body, name=name, grid=(l // tr,),
        in_specs=[row, row, vec, row], out_specs=[row, row, vec],
        out_shape=[jax.ShapeDtypeStruct((l, d), F32), jax.ShapeDtypeStruct((l, d), BF16),
                   jax.ShapeDtypeStruct((1, d), F32)],
        compiler_params=_params("arbitrary"),
    )(dh, x, gain, gout)


def _final_loss(x, gain, target, row_lo, row_hi, name):
    l, d = x.shape
    tr = _tile(l, 272, 16)

    def body(x_ref, g_ref, t_ref, gx_ref, gxb_ref, dg_ref, loss_ref):
        i = pl.program_id(0)
        xv = x_ref[...]
        r = lax.rsqrt(jnp.mean(xv * xv, axis=-1, keepdims=True) + RMS_EPS)
        nv = xv * r
        gv = g_ref[...]
        rows = i * tr + lax.broadcasted_iota(jnp.int32, (tr, 1), 0)
        valid = jnp.logical_and(rows >= row_lo, rows < row_hi)
        err = jnp.where(valid, nv * gv - t_ref[...], 0.0)
        dy = err * (1.0 / d)
        dn = dy * gv
        gx = r * (dn - nv * jnp.mean(dn * nv, axis=-1, keepdims=True))
        gx_ref[...] = gx
        gxb_ref[...] = gx.astype(BF16)
        part = jnp.sum(dy * nv, axis=0, keepdims=True)
        lpart = jnp.full((1, 128), 0.5 * jnp.sum(jnp.mean(err * err, axis=-1, keepdims=True)), F32)

        @pl.when(i == 0)
        def _():
            dg_ref[...] = part
            loss_ref[...] = lpart

        @pl.when(i > 0)
        def _():
            dg_ref[...] += part
            loss_ref[...] += lpart

    row = pl.BlockSpec((tr, d), lambda i: (i, 0))
    vec = pl.BlockSpec((1, d), lambda i: (0, 0))
    return pl.pallas_call(
        body, name=name, grid=(l // tr,),
        in_specs=[row, vec, row], out_specs=[row, row, vec, pl.BlockSpec((1, 128), lambda i: (0, 0))],
        out_shape=[jax.ShapeDtypeStruct((l, d), F32), jax.ShapeDtypeStruct((l, d), BF16),
                   jax.ShapeDtypeStruct((1, d), F32), jax.ShapeDtypeStruct((1, 128), F32)],
        compiler_params=_params("arbitrary"),
    )(x, gain, target)


def _log1m_beta(z):
    return -(jnp.maximum(z, 0.0) + jnp.log(1.0 + jnp.exp(-jnp.abs(z))))


def _tri(n, relation):
    r = lax.broadcasted_iota(jnp.int32, (n, n), 0)
    c = lax.broadcasted_iota(jnp.int32, (n, n), 1)
    return jnp.where(relation(r, c), 1.0, 0.0).astype(BF16)


def _running_sum(x, tri, backward, split):
    n = tri.shape[0]
    blocks = [x[:, s * n:(s + 1) * n] for s in range(x.shape[1] // n)]
    inner = [(_dot_split if split else _dot)(b, tri) for b in blocks]
    totals = [jnp.sum(b, axis=1, keepdims=True) for b in blocks]
    order = list(reversed(range(len(blocks)))) if backward else list(range(len(blocks)))
    out, offset = [None] * len(blocks), None
    for s in order:
        out[s] = inner[s] if offset is None else inner[s] + offset
        offset = totals[s] if offset is None else offset + totals[s]
    return (out[0] if len(out) == 1 else jnp.concatenate(out, axis=1)), offset


def _head(hh):
    return slice(hh * HEAD_DIM, (hh + 1) * HEAD_DIM)


def _heads(x, hb):
    return jnp.stack([x[:, _head(hh)] for hh in range(hb)], axis=0)


def _bdot(a, b, ta=False, tb=False):
    dims = (((1 if ta else 2,), (2 if tb else 1,)), ((0,), (0,)))
    return lax.dot_general(a.astype(BF16), b.astype(BF16), dims, preferred_element_type=F32)


def _attn_specs(l, hb, n_heads):
    width = hb * HEAD_DIM
    groups = n_heads // hb

    def tile(section):
        return pl.BlockSpec((ATT_BLOCK, width), lambda h, i: (i, section * groups + h))

    def slab(section):
        return pl.BlockSpec((l, width), lambda h, i: (0, section * groups + h))

    return tile, slab


def _attn_fwd(zin, n_heads, name):
    l = zin.shape[0]
    t = ATT_BLOCK
    hb = ATT_HEADS if n_heads % ATT_HEADS == 0 else 1
    scale = HEAD_DIM ** -0.5

    def body(q_ref, k_ref, v_ref, g_ref, o_ref, oa_ref, tot_ref):
        i = pl.program_id(1)
        after = _tri(t, lambda r, c: r > c)
        causal = (lax.broadcasted_iota(jnp.int32, (t, t), 1) < lax.broadcasted_iota(jnp.int32, (t, t), 0))[None]
        q = _heads(q_ref[...], hb).astype(BF16)

        def tile(k0, w, carry, diagonal=False):
            run, acc = carry
            z = _bdot(q, _heads(k_ref[pl.ds(k0, w), :], hb), tb=True) * scale
            lb_all = _log1m_beta(z)
            lb = jnp.where(causal, lb_all, 0.0) if diagonal else lb_all
            between, total = _running_sum(lb.reshape(hb * t, w), after, True, True)
            a = jnp.exp(z + lb_all + between.reshape(hb, t, w) + run)
            if diagonal:
                a = jnp.where(causal, a, 0.0)
            return run + total.reshape(hb, t, 1), acc + _bdot(a, _heads(v_ref[pl.ds(k0, w), :], hb))

        rem = i % 4
        carry = tile(pl.multiple_of(i * t, t), t, (jnp.zeros((hb, t, 1), F32), jnp.zeros((hb, t, HEAD_DIM), F32)), True)
        carry = lax.cond(rem % 2 == 1, lambda c: tile(pl.multiple_of((i - 1) * t, t), t, c), lambda c: c, carry)
        carry = lax.cond(rem >= 2, lambda c: tile(pl.multiple_of((i - rem % 2 - 2) * t, 2 * t), 2 * t, c),
                         lambda c: c, carry)
        run, o = lax.fori_loop(
            0, i // 4, lambda it, c: tile(pl.multiple_of((i - rem - 4 * it - 4) * t, 4 * t), 4 * t, c), carry)
        for hh in range(hb):
            gate = g_ref[:, _head(hh)]
            o_ref[:, _head(hh)] = o[hh]
            oa_ref[:, _head(hh)] = (o[hh] * (gate * _sigmoid(gate))).astype(BF16)
            tot_ref[:, _head(hh)] = jnp.broadcast_to(run[hh], (t, HEAD_DIM))

    tile_spec, slab_spec = _attn_specs(l, hb, n_heads)
    width = n_heads * HEAD_DIM
    return pl.pallas_call(
        body, name=name, grid=(n_heads // hb, l // t),
        in_specs=[tile_spec(0), slab_spec(1), slab_spec(2), tile_spec(3)],
        out_specs=[tile_spec(0), tile_spec(0), tile_spec(0)],
        out_shape=[jax.ShapeDtypeStruct((l, width), F32), jax.ShapeDtypeStruct((l, width), BF16),
                   jax.ShapeDtypeStruct((l, width), F32)],
        compiler_params=_params("parallel", "arbitrary"),
    )(zin, zin, zin, zin)


def _attn_bwd(zin, o, tot, doa, n_heads, name, after=None):
    l = zin.shape[0]
    t = ATT_BLOCK
    nq = l // t
    hb = ATT_HEADS if n_heads % ATT_HEADS == 0 else 1
    scale = HEAD_DIM ** -0.5

    def body(q_ref, k_ref, v_ref, g_ref, o_ref, tot_ref, doa_ref, *rest):
        dq_ref, dk_ref, dv_ref, dg_ref, dk_acc, dv_acc = rest[-6:]
        i = pl.program_id(1)

        @pl.when(i == 0)
        def _():
            dk_acc[...] = jnp.zeros_like(dk_acc)
            dv_acc[...] = jnp.zeros_like(dv_acc)

        upto = _tri(t, lambda r, c: r <= c)
        before = _tri(t, lambda r, c: r < c)
        causal = (lax.broadcasted_iota(jnp.int32, (t, t), 1) < lax.broadcasted_iota(jnp.int32, (t, t), 0))[None]
        gate = g_ref[...]
        sg = _sigmoid(gate)
        doav = doa_ref[...]
        dg_ref[...] = (doav * o_ref[...] * (sg * (1.0 + gate * (1.0 - sg)))).astype(BF16)
        do = _heads(doav * (gate * sg), hb).astype(BF16)
        q = _heads(q_ref[...], hb).astype(BF16)
        total = _heads(tot_ref[...], hb)[:, :, 0:1]

        def tile(k0, w, carry, diagonal=False):
            run, pre, dq = carry
            kb = _heads(k_ref[pl.ds(k0, w), :], hb).astype(BF16)
            z = _bdot(q, kb, tb=True) * scale
            lb_all = _log1m_beta(z)
            lb = jnp.where(causal, lb_all, 0.0) if diagonal else lb_all
            beta = jnp.exp(z + lb_all)
            sofar, lb_total = _running_sum(lb.reshape(hb * t, w), upto, False, True)
            a = beta * jnp.exp(total - run - sofar.reshape(hb, t, w))
            if diagonal:
                a = jnp.where(causal, a, 0.0)
            e = a * _bdot(do, _heads(v_ref[pl.ds(k0, w), :], hb), tb=True)
            earlier, e_total = _running_sum(e.reshape(hb * t, w), before, False, False)
            dz = (e * (1.0 - beta) - beta * (pre + earlier.reshape(hb, t, w))) * scale
            if diagonal:
                dz = jnp.where(causal, dz, 0.0)
            dv = _bdot(a, do, ta=True)
            dk = _bdot(dz, q, ta=True)
            for hh in range(hb):
                dv_acc[pl.ds(k0, w), _head(hh)] += dv[hh]
                dk_acc[pl.ds(k0, w), _head(hh)] += dk[hh]
            return run + lb_total.reshape(hb, t, 1), pre + e_total.reshape(hb, t, 1), dq + _bdot(dz, kb)

        rem = i % 4
        zero = jnp.zeros((hb, t, 1), F32)
        carry = lax.fori_loop(0, i // 4, lambda j, c: tile(pl.multiple_of(j * 4 * t, 4 * t), 4 * t, c),
                              (zero, zero, jnp.zeros((hb, t, HEAD_DIM), F32)))
        carry = lax.cond(rem >= 2, lambda c: tile(pl.multiple_of((i - rem) * t, 2 * t), 2 * t, c), lambda c: c, carry)
        carry = lax.cond(rem % 2 == 1, lambda c: tile(pl.multiple_of((i - 1) * t, t), t, c), lambda c: c, carry)
        _, _, dq = tile(pl.multiple_of(i * t, t), t, carry, True)
        for hh in range(hb):
            dq_ref[:, _head(hh)] = dq[hh].astype(BF16)

        @pl.when(i == nq - 1)
        def _():
            dk_ref[...] = dk_acc[...].astype(BF16)
            dv_ref[...] = dv_acc[...].astype(BF16)

    tile_spec, slab_spec = _attn_specs(l, hb, n_heads)
    out = jax.ShapeDtypeStruct((l, n_heads * HEAD_DIM), BF16)
    return pl.pallas_call(
        body, name=name, grid=(n_heads // hb, nq),
        in_specs=[tile_spec(0), slab_spec(1), slab_spec(2), tile_spec(3), tile_spec(0), tile_spec(0), tile_spec(0)]
        + _after_specs(after),
        out_specs=[tile_spec(0), slab_spec(0), slab_spec(0), tile_spec(0)],
        out_shape=[out, out, out, out],
        scratch_shapes=[pltpu.VMEM((l, hb * HEAD_DIM), F32), pltpu.VMEM((l, hb * HEAD_DIM), F32)],
        compiler_params=_params("parallel", "arbitrary"),
    )(zin, zin, zin, zin, o, tot, doa, *_after_operands(after))


def _shift_rows(x, k, down):
    n = x.shape[0]
    rows = lax.broadcasted_iota(jnp.int32, x.shape, 0)
    if down:
        return jnp.where(rows >= k, pltpu.roll(x, k, 0), 0.0)
    return jnp.where(rows < n - k, pltpu.roll(x, n - k, 0), 0.0)


def _window_sum(x, g, down):
    result = x
    total = x
    for step, k in enumerate((1, 2, 4, 8)):
        total = total + _shift_rows(total, k, down)
        result = jnp.where(g >= step, total, result)
    return result


def _pooled(u, g):
    rows = lax.broadcasted_iota(jnp.int32, (u.shape[0], 1), 0)
    window = jnp.left_shift(2, g)
    cnt = jnp.minimum(rows + 1, window).astype(F32)
    return _window_sum(u, g, True) / cnt - u, cnt


def _pool_fwd(zin, pool_w, pool_scale, u_off, name, after=None):
    l = zin.shape[0]
    n_groups, gd, _ = pool_w.shape

    def body(u_ref, g_ref, w_ref, s_ref, *rest):
        o_ref = rest[-1]
        g = pl.program_id(0)
        pooled, _ = _pooled(u_ref[...], g)
        mixed = _dot(pooled, w_ref[...])
        gate = g_ref[...]
        o_ref[...] = (mixed * s_ref[...] * (gate * _sigmoid(gate))).astype(BF16)

    return pl.pallas_call(
        body, name=name, grid=(n_groups,),
        in_specs=[pl.BlockSpec((l, gd), lambda g: (0, u_off + g)),
                  pl.BlockSpec((l, gd), lambda g: (0, u_off + n_groups + g)),
                  pl.BlockSpec((None, gd, gd), lambda g: (g, 0, 0)),
                  pl.BlockSpec((1, gd), lambda g: (0, g))] + _after_specs(after),
        out_specs=pl.BlockSpec((l, gd), lambda g: (0, g)),
        out_shape=jax.ShapeDtypeStruct((l, n_groups * gd), BF16),
        compiler_params=_params("parallel"),
    )(zin, zin, pool_w, pool_scale, *_after_operands(after))


def _pool_bwd(zin, dop, pool_w, pool_scale, u_off, name):
    l = zin.shape[0]
    n_groups, gd, _ = pool_w.shape

    def body(u_ref, g_ref, w_ref, s_ref, d_ref, du_ref, dg_ref, dw_ref, ds_ref):
        g = pl.program_id(0)
        pooled, cnt = _pooled(u_ref[...], g)
        w = w_ref[...]
        mixed = _dot(pooled, w)
        gate = g_ref[...]
        sg = _sigmoid(gate)
        silu = gate * sg
        dop_v = d_ref[...]
        sc = s_ref[...]
        ds_ref[...] = jnp.sum(dop_v * mixed * silu, axis=0, keepdims=True)
        dg_ref[...] = (dop_v * mixed * sc * (sg * (1.0 + gate * (1.0 - sg)))).astype(BF16)
        dmixed = dop_v * sc * silu
        dw_ref[...] = _dot(pooled, dmixed, ta=True).astype(BF16)
        dpooled = _dot(dmixed, w, tb=True)
        du_ref[...] = (_window_sum(dpooled / cnt, g, False) - dpooled).astype(BF16)

    slab = pl.BlockSpec((l, gd), lambda g: (0, g))
    return pl.pallas_call(
        body, name=name, grid=(n_groups,),
        in_specs=[pl.BlockSpec((l, gd), lambda g: (0, u_off + g)),
                  pl.BlockSpec((l, gd), lambda g: (0, u_off + n_groups + g)),
                  pl.BlockSpec((None, gd, gd), lambda g: (g, 0, 0)),
                  pl.BlockSpec((1, gd), lambda g: (0, g)), slab],
        out_specs=[slab, slab, pl.BlockSpec((None, gd, gd), lambda g: (g, 0, 0)),
                   pl.BlockSpec((1, gd), lambda g: (0, g))],
        out_shape=[jax.ShapeDtypeStruct((l, n_groups * gd), BF16), jax.ShapeDtypeStruct((l, n_groups * gd), BF16),
                   jax.ShapeDtypeStruct(pool_w.shape, BF16), jax.ShapeDtypeStruct((1, n_groups * gd), F32)],
        compiler_params=_params("parallel"),
    )(zin, zin, pool_w, pool_scale, dop)


def _merge_fwd(oa, op, w_au, w_pu, zin, name):
    l, wa = oa.shape
    wp = op.shape[1]
    d = w_au.shape[1]
    tm = _tile(l, 1088, 16)
    tn = _tile(d, 512, 128)
    ma_off = (zin.shape[1] - 2 * d) // tn

    def body(oa_ref, op_ref, wa_ref, wp_ref, ma_ref, mp_ref, ya_ref, yp_ref, mg_ref):
        ya = _dot(oa_ref[...], wa_ref[...])
        yp = _dot(op_ref[...], wp_ref[...])
        ya_ref[...] = ya.astype(BF16)
        yp_ref[...] = yp.astype(BF16)
        mg_ref[...] = (_sigmoid(ma_ref[...]) * ya + _sigmoid(mp_ref[...]) * yp).astype(BF16)

    tile = pl.BlockSpec((tm, tn), lambda i, j: (i, j))
    return pl.pallas_call(
        body, name=name, grid=(l // tm, d // tn),
        in_specs=[pl.BlockSpec((tm, wa), lambda i, j: (i, 0)), pl.BlockSpec((tm, wp), lambda i, j: (i, 0)),
                  pl.BlockSpec((wa, tn), lambda i, j: (0, j)), pl.BlockSpec((wp, tn), lambda i, j: (0, j)),
                  pl.BlockSpec((tm, tn), lambda i, j: (i, ma_off + j)),
                  pl.BlockSpec((tm, tn), lambda i, j: (i, ma_off + d // tn + j))],
        out_specs=[tile, tile, tile],
        out_shape=[jax.ShapeDtypeStruct((l, d), BF16)] * 3,
        compiler_params=_params("parallel", "parallel"),
    )(oa, op, w_au, w_pu, zin, zin)


def _merge_bwd(gout, w_out, zin, ya, yp, name):
    l, d = gout.shape
    tm = _tile(l, 1088, 16)
    tn = _tile(d, 512, 128)
    ma_off = (zin.shape[1] - 2 * d) // tn

    def body(g_ref, w_ref, ma_ref, mp_ref, ya_ref, yp_ref, dya_ref, dyp_ref, dma_ref, dmp_ref):
        dm = _dot(g_ref[...], w_ref[...], tb=True)
        sa = _sigmoid(ma_ref[...])
        sp = _sigmoid(mp_ref[...])
        dya_ref[...] = (dm * sa).astype(BF16)
        dyp_ref[...] = (dm * sp).astype(BF16)
        dma_ref[...] = (dm * ya_ref[...].astype(F32) * (sa * (1.0 - sa))).astype(BF16)
        dmp_ref[...] = (dm * yp_ref[...].astype(F32) * (sp * (1.0 - sp))).astype(BF16)

    tile = pl.BlockSpec((tm, tn), lambda i, j: (i, j))
    out = jax.ShapeDtypeStruct((l, d), BF16)
    return pl.pallas_call(
        body, name=name, grid=(l // tm, d // tn),
        in_specs=[pl.BlockSpec((tm, d), lambda i, j: (i, 0)), pl.BlockSpec((tn, d), lambda i, j: (j, 0)),
                  pl.BlockSpec((tm, tn), lambda i, j: (i, ma_off + j)),
                  pl.BlockSpec((tm, tn), lambda i, j: (i, ma_off + d // tn + j)), tile, tile],
        out_specs=[tile, tile, tile, tile],
        out_shape=[out, out, out, out],
        compiler_params=_params("parallel", "parallel"),
    )(gout, w_out, zin, zin, ya, yp)


def _adamw(parts, w, m, v, name, first=0, into=None):
    n_arrays, n_parts, r, c = parts.shape
    tr = _tile(r, max(8, (128 * 1024) // c // 8 * 8), 8)
    bias1 = 1.0 - ADAM_B1 ** ADAM_STEP
    bias2 = 1.0 - ADAM_B2 ** ADAM_STEP

    def body(p_ref, w_ref, m_ref, v_ref, *rest):
        g_ref, d_ref, nm_ref, nv_ref = rest[-4:]
        g = p_ref[0].astype(F32)
        for j in range(1, n_parts):
            g = g + p_ref[j].astype(F32)
        nm = ADAM_B1 * m_ref[...] + (1.0 - ADAM_B1) * g
        nv = ADAM_B2 * v_ref[...] + (1.0 - ADAM_B2) * (g * g)
        g_ref[...] = g
        nm_ref[...] = nm
        nv_ref[...] = nv
        d_ref[...] = -ADAM_LR * ((nm / bias1) / (jnp.sqrt(nv / bias2) + ADAM_EPS) + ADAM_WD * w_ref[...])

    tile = pl.BlockSpec((None, tr, c), lambda a, i: (a + first, i, 0))
    out = jax.ShapeDtypeStruct(w.shape, F32)
    kept = [] if into is None else list(into)
    return pl.pallas_call(
        body, name=name, grid=(n_arrays, r // tr),
        in_specs=[pl.BlockSpec((None, n_parts, tr, c), lambda a, i: (a, 0, i, 0)), tile, tile, tile]
        + [pl.BlockSpec(memory_space=pl.ANY)] * len(kept),
        out_specs=[tile, tile, tile, tile], out_shape=[out, out, out, out],
        input_output_aliases={4 + k: k for k in range(len(kept))},
        compiler_params=_params("parallel", "parallel"),
    )(parts, w, m, v, *kept)


def _position():
    return lax.axis_index("x"), lax.axis_index("y"), lax.axis_index("c")


def _block_of(ref, axis, size, index):
    idx = [slice(None)] * len(ref.shape)
    idx[axis] = pl.ds(index * size, size)
    return ref.at[tuple(idx)]


HBM_SPEC = pl.BlockSpec(memory_space=pltpu.HBM)
SEM_SPEC = pl.BlockSpec(memory_space=pltpu.SEMAPHORE)
SIDE_EFFECT = pltpu.CompilerParams(has_side_effects=pltpu.SideEffectType.DATAFLOW_SIDE_EFFECTING)


def _split_start(make_copies, n_copies, buffers, name):
    n = len(buffers)

    def body(*refs):
        send_sems, recv_sems = refs[n], refs[n + 1]
        for cp in make_copies(refs[:n], send_sems, recv_sems):
            cp.start()
        refs[-1][...] = jnp.zeros_like(refs[-1])

    sems = pltpu.SemaphoreType.DMA((n_copies,))
    return pl.pallas_call(
        body, name=name, in_specs=[HBM_SPEC] * n,
        out_shape=(sems, sems, *[pltpu.HBM(b.shape, b.dtype) for b in buffers], jax.ShapeDtypeStruct((8, 128), F32)),
        out_specs=(SEM_SPEC, SEM_SPEC, *[HBM_SPEC] * n, pl.BlockSpec(memory_space=pltpu.VMEM)),
        input_output_aliases={i: 2 + i for i in range(n)}, compiler_params=SIDE_EFFECT,
    )(*[pltpu.with_memory_space_constraint(b, pltpu.HBM) for b in buffers])


def _split_wait(make_copies, started, after, name):
    send_sems, recv_sems, *buffers = started[:-1]
    n = len(buffers)

    def body(*refs):
        copies = make_copies(refs[:n], refs[n], refs[n + 1])
        for cp in copies:
            cp.wait_send()
        for cp in copies:
            cp.wait_recv()

    return pl.pallas_call(
        body, name=name, in_specs=[HBM_SPEC] * n + [SEM_SPEC, SEM_SPEC, pl.BlockSpec(memory_space=pl.ANY)],
        out_shape=[pltpu.HBM(b.shape, b.dtype) for b in buffers], out_specs=[HBM_SPEC] * n,
        input_output_aliases={i: i for i in range(n)}, compiler_params=SIDE_EFFECT,
    )(*buffers, send_sems, recv_sems, after)


def _gather_copies(axes, sizes, level):
    def make(fulls, send_sems, recv_sems):
        x, y, c = _position()
        chips = [(1 - x, y), (x, 1 - y), (1 - x, 1 - y)]
        copies = []
        for a, full in enumerate(fulls):
            def copy(k, block, to, full=full, a=a):
                rows = _block_of(full, axes[a], sizes[a], 4 * block[0] + 2 * block[1] + block[2])
                idx = a * (4 if level == 1 else 3) + k
                return pltpu.make_async_remote_copy(src_ref=rows, dst_ref=rows, send_sem=send_sems.at[idx],
                                                    recv_sem=recv_sems.at[idx], device_id=to, device_id_type=MESH)
            if level == 1:
                copies.append(copy(0, (x, y, c), (x, y, 1 - c)))
                copies += [copy(1 + j, (x, y, c), (*chip, c)) for j, chip in enumerate(chips)]
            else:
                copies += [copy(j, (*chip, c), (x, y, 1 - c)) for j, chip in enumerate(chips)]
        return copies
    return make


def _exchange_copies(axes, sizes, layer, n_src):
    flips = [(a, b, d) for a in (0, 1) for b in (0, 1) for d in (0, 1)][1:]

    def make(buffers, send_sems, recv_sems):
        x, y, c = _position()
        my_index = 4 * x + 2 * y + c
        copies = []
        for a in range(n_src):
            for k, flip in enumerate(flips):
                px, py, pc = x ^ flip[0], y ^ flip[1], c ^ flip[2]
                copies.append(pltpu.make_async_remote_copy(
                    src_ref=_block_of(buffers[a], axes[a], sizes[a], 4 * px + 2 * py + pc),
                    dst_ref=buffers[n_src + a].at[layer, my_index],
                    send_sem=send_sems.at[a * 7 + k], recv_sem=recv_sems.at[a * 7 + k],
                    device_id=(px, py, pc), device_id_type=MESH))
        return copies
    return make


def _allgather_small(v, name, after=None):
    r, c = v.shape
    flips = [(a, b, d) for a in (0, 1) for b in (0, 1) for d in (0, 1)][1:]

    def body(v_ref, *rest):
        out_ref, send_sems, recv_sems = rest[-3:]
        x, y, c_ = _position()
        my_index = 4 * x + 2 * y + c_
        out_ref[my_index] = v_ref[...]
        sends = []
        for k, flip in enumerate(flips):
            peer = (x ^ flip[0], y ^ flip[1], c_ ^ flip[2])
            cp = pltpu.make_async_remote_copy(
                src_ref=v_ref, dst_ref=out_ref.at[my_index],
                send_sem=send_sems.at[k], recv_sem=recv_sems.at[k], device_id=peer, device_id_type=MESH)
            cp.start()
            sends.append(cp)
        for k, flip in enumerate(flips):
            px, py, pc = x ^ flip[0], y ^ flip[1], c_ ^ flip[2]
            pltpu.make_async_remote_copy(
                src_ref=v_ref, dst_ref=out_ref.at[4 * px + 2 * py + pc],
                send_sem=send_sems.at[k], recv_sem=recv_sems.at[k],
                device_id=(px, py, pc), device_id_type=MESH).wait_recv()
        for cp in sends:
            cp.wait_send()

    return pl.pallas_call(
        body, name=name,
        in_specs=[pl.BlockSpec(memory_space=pltpu.VMEM)] + _after_specs(after),
        out_specs=pl.BlockSpec(memory_space=pltpu.VMEM),
        out_shape=jax.ShapeDtypeStruct((N_DEV, r, c), v.dtype),
        scratch_shapes=[pltpu.SemaphoreType.DMA((7,)), pltpu.SemaphoreType.DMA((7,))],
    )(v, *_after_operands(after))


def kernel(x, meta_tokens, norm_gain, w_in, pool_w, pool_scale, w_attn_up, w_pool_up, w_out, final_gain, loss_target, m_meta_tokens, m_norm_gain, m_w_in, m_pool_w, m_pool_scale, m_w_attn_up, m_w_pool_up, m_w_out, m_final_gain, v_meta_tokens, v_norm_gain, v_w_in, v_pool_w, v_pool_scale, v_w_attn_up, v_w_pool_up, v_w_out, v_final_gain):
    _, seq, d = x.shape
    n_meta = meta_tokens.shape[0]
    depth = w_in.shape[0]
    sb_width = w_attn_up.shape[1]
    pool_width = w_pool_up.shape[1]
    n_heads = sb_width // HEAD_DIM
    n_groups = pool_w.shape[1]
    gd = pool_w.shape[3]
    assert n_groups == len(POOL_WINDOWS) and gd * n_groups == pool_width
    assert w_in.shape[2] * N_DEV == 4 * sb_width + 2 * pool_width + 2 * d
    l_real = n_meta + seq
    l_pad = -(-l_real // ATT_BLOCK) * ATT_BLOCK
    my_index = 4 * lax.axis_index("x") + 2 * lax.axis_index("y") + lax.axis_index("c")

    me = jnp.reshape(my_index, (1,)).astype(jnp.int32)
    g_named = [("w_in", w_in), ("pool_w", pool_w), ("w_attn_up", w_attn_up), ("w_pool_up", w_pool_up), ("w_out", w_out)]
    g_axes = [1, 1, 1, 1, 0]
    g_sizes = [w.shape[1 + ax] for (_, w), ax in zip(g_named, g_axes)]
    level1 = _gather_copies(g_axes, g_sizes, 1)
    level2 = _gather_copies(g_axes, g_sizes, 2)

    def gather_start(i, after=None):
        fulls = [_cast_place(w, i, ax, me, "cast_" + nm, after) for (nm, w), ax in zip(g_named, g_axes)]
        return _split_start(level1, 4 * len(fulls), fulls, "gather1_start_%d" % i)

    def gather_forward(i, started, after):
        arrived = _split_wait(level1, started, after, "gather1_wait_%d" % i)
        return _split_start(level2, 3 * len(arrived), arrived, "gather2_start_%d" % i)

    meta_all = _allgather_small(meta_tokens, "allgather_meta")
    meta_full = jnp.transpose(meta_all, (1, 0, 2)).reshape(n_meta, d)
    first = gather_start(0)
    second = gather_forward(0, first, first[-1])

    pad_rows = l_pad - l_real
    hs = jnp.concatenate([meta_full, x[0], jnp.zeros((pad_rows, d), F32)], axis=0)
    target = jnp.concatenate([jnp.zeros((n_meta, d), F32), loss_target[0], jnp.zeros((pad_rows, d), F32)], axis=0)
    u_off = 4 * sb_width // gd
    saved, weights = [], []
    for i in range(depth):
        h = _rms_fwd(hs, norm_gain[i][None], "rms_fwd")
        weights.append(_split_wait(level2, second, h, "gather2_wait_%d" % i))
        wi, pw, wau, wpu, wo = weights[i]
        more = i + 1 < depth
        first = gather_start(i + 1, wi) if more else None
        zin = _matmul(h, wi, after=first[-1] if more else None, name="mm_zin")
        o, oa, tot = _attn_fwd(zin, n_heads, "attn_fwd")
        op = _pool_fwd(zin, pw, pool_scale[i][None], u_off, "pool_fwd")
        ya, yp, merged = _merge_fwd(oa, op, wau, wpu, zin, "merge_fwd")
        second = gather_forward(i + 1, first, merged) if more else None
        saved.append((hs, h, zin, o, tot, oa, op, ya, yp, merged))
        hs = _matmul(merged, wo, add=hs, after=second[-1] if more else None, name="mm_out")
    g, gb, d_final_gain, loss_part = _final_loss(hs, final_gain[None], target, n_meta, l_real, "final_loss")
    loss = lax.psum(loss_part[0, 0], ("x", "y", "c"))

    d_norm_gain = [None] * depth
    d_pool_scale = [None] * depth
    axes_a, axes_b = [1, 1, 0], [1, 1]
    blocks_a, blocks_b = [w_attn_up[0], w_pool_up[0], w_out[0]], [w_in[0], pool_w[0]]
    sizes_a = [b.shape[ax] for b, ax in zip(blocks_a, axes_a)]
    sizes_b = [b.shape[ax] for b, ax in zip(blocks_b, axes_b)]
    land_a = [lax.empty((depth, N_DEV, *b.shape), BF16) for b in blocks_a]
    land_b = [lax.empty((depth - 1, N_DEV, *b.shape), BF16) for b in blocks_b]
    land_b0 = [lax.empty((1, N_DEV, *b.shape), BF16) for b in blocks_b]

    def exchange_start(grads, landing, axes, sizes, slot, name):
        landing = [lax.dynamic_update_slice(
            zone, lax.dynamic_slice_in_dim(grad, my_index * size, size, ax)[None, None],
            (slot, my_index) + (0,) * grad.ndim) for zone, grad, ax, size in zip(landing, grads, axes, sizes)]
        copies = _exchange_copies(axes, sizes, slot, len(grads))
        return copies, _split_start(copies, 7 * len(grads), list(grads) + landing, name)

    def exchange_wait(pending, n_src, after, name):
        return _split_wait(pending[0], pending[1], after, name)[n_src:]

    pend_a = pend_b = None
    for i in reversed(range(depth)):
        wi, pw, wau, wpu, wo = weights[i]
        hs_in, h, zin, o, tot, oa, op, ya, yp, merged = saved[i]
        dya, dyp, dma, dmp = _merge_bwd(gb, wo, zin, ya, yp, "merge_bwd")
        dw_out = _matmul(merged, gb, ta=True, out_dtype=BF16, name="mm_dw_out", tm=1024, tk=1088)
        doa = _matmul(dya, wau, tb=True, name="mm_doa")
        dw_au = _matmul(oa, dya, ta=True, out_dtype=BF16, name="mm_dw_au", tm=1024, tk=1088)
        dop = _matmul(dyp, wpu, tb=True, name="mm_dop")
        dw_pu = _matmul(op, dyp, ta=True, out_dtype=BF16, name="mm_dw_pu", tm=1024, tk=1088)
        if pend_a is not None:
            land_a = exchange_wait(pend_a, 3, dw_pu, "exchange_a_wait_%d" % (i + 1))
        pend_a = exchange_start([dw_au, dw_pu, dw_out], land_a, axes_a, sizes_a, i, "exchange_a_start_%d" % i)
        dq, dk, dv, dga = _attn_bwd(zin, o, tot, doa, n_heads, "attn_bwd", after=pend_a[1][-1])
        du, dgp, dpw, dps = _pool_bwd(zin, dop, pw, pool_scale[i][None], u_off, "pool_bwd")
        dzin = [dq, dk, dv, dga, du, dgp, dma, dmp]
        dw_in = _matmul(h, dzin, ta=True, out_dtype=BF16, name="mm_dw_in", tm=1024, tk=1088)
        if pend_b is not None:
            land_b = exchange_wait(pend_b, 2, dw_in, "exchange_b_wait_%d" % (i + 1))
        pend_b = exchange_start([dw_in, dpw], land_b if i > 0 else land_b0, axes_b, sizes_b, max(i - 1, 0),
                                "exchange_b_start_%d" % i)
        dh = _matmul(dzin, wi, tb=True, after=pend_b[1][-1], name="mm_dh")
        g, gb, dng = _rms_bwd(dh, hs_in, norm_gain[i][None], g, "rms_bwd")
        d_norm_gain[i] = dng
        d_pool_scale[i] = dps
    land_a = exchange_wait(pend_a, 3, gb, "exchange_a_wait_0")
    grad_x = g[n_meta:l_real][None]

    def sharded(parts, w, m, v, name, first=0, into=None):
        flat = (depth, -1, w.shape[-1])
        return _adamw(parts.reshape(parts.shape[0], N_DEV, -1, w.shape[-1]), w.reshape(flat), m.reshape(flat),
                      v.reshape(flat), name, first, into)

    out_au = sharded(land_a[0], w_attn_up, m_w_attn_up, v_w_attn_up, "adamw_w_attn_up")
    out_pu = sharded(land_a[1], w_pool_up, m_w_pool_up, v_w_pool_up, "adamw_w_pool_up")
    out_wo = sharded(land_a[2], w_out, m_w_out, v_w_out, "adamw_w_out")
    out_wi = out_pw = None
    if depth > 1:
        out_wi = sharded(land_b[0], w_in, m_w_in, v_w_in, "adamw_w_in", 1)
        out_pw = sharded(land_b[1], pool_w, m_pool_w, v_pool_w, "adamw_pool_w", 1)
    land_b0 = exchange_wait(pend_b, 2, out_wo[0] if out_wi is None else out_wi[0], "exchange_b_wait_0")
    out_wi = sharded(land_b0[0], w_in, m_w_in, v_w_in, "adamw_w_in_0", 0, out_wi)
    out_pw = sharded(land_b0[1], pool_w, m_pool_w, v_pool_w, "adamw_pool_w_0", 0, out_pw)
    out_wi, out_pw, out_au, out_pu, out_wo = [
        [t.reshape(w.shape) for t in res] for res, w in
        zip([out_wi, out_pw, out_au, out_pu, out_wo], [w_in, pool_w, w_attn_up, w_pool_up, w_out])]

    zeros_ps = jnp.zeros((depth, d - pool_width), F32)
    small_rows = [jnp.concatenate(d_norm_gain, axis=0),
                  jnp.concatenate([jnp.concatenate(d_pool_scale, axis=0), zeros_ps], axis=1), d_final_gain]
    n_small = 2 * depth + 1
    small_pad = -(-n_small // 8) * 8
    small = jnp.concatenate(small_rows + [jnp.zeros((small_pad - n_small, d), F32), g[:n_meta]], axis=0)
    small_all = _allgather_small(small, "allgather_small", after=land_b0[0])

    def replicated(rows, width, w, m, v, name):
        parts = lax.slice(small_all, (0, rows[0], 0), (N_DEV, rows[1], width))
        return [t[0] for t in _adamw(parts[None], w[None], m[None], v[None], name)]

    out_ng = replicated((0, depth), d, norm_gain, m_norm_gain, v_norm_gain, "adamw_norm_gain")
    out_ps = replicated((depth, 2 * depth), pool_width, pool_scale, m_pool_scale, v_pool_scale, "adamw_pool_scale")
    out_fg = [t[0] for t in replicated((2 * depth, 2 * depth + 1), d, final_gain[None], m_final_gain[None],
                                       v_final_gain[None], "adamw_final_gain")]
    cols = d // N_DEV
    meta_parts = lax.dynamic_slice(small_all, (0, small_pad, my_index * cols), (N_DEV, n_meta, cols))
    out_meta = [t[0] for t in _adamw(meta_parts[None], meta_tokens[None], m_meta_tokens[None], v_meta_tokens[None],
                                     "adamw_meta")]

    by_weight = [out_meta, out_ng, out_wi, out_pw, out_ps, out_au, out_pu, out_wo, out_fg]
    return (loss, grad_x, *[o[0] for o in by_weight], *[o[1] for o in by_weight],
            *[o[2] for o in by_weight], *[o[3] for o in by_weight])
```

```python
import jax
import jax.numpy as jnp
from jax import lax
from jax.experimental import pallas as pl
from jax.experimental.pallas import tpu as pltpu

F32 = jnp.float32
BF16 = jnp.bfloat16
MESH = pl.DeviceIdType.MESH

N_DEV = 8
HEAD_DIM = 128
ATT_BLOCK = 128
ATT_HEADS = 4
POOL_WINDOWS = (2, 4, 8, 16)
RMS_EPS = 1e-6
ADAM_LR, ADAM_B1, ADAM_B2, ADAM_EPS, ADAM_WD, ADAM_STEP = 0.001, 0.9, 0.999, 1e-08, 0.01, 10
VMEM_LIMIT_BYTES = 56 * 1024 * 1024


def _tile(n, target, mult):
    if n <= target:
        return n
    best = 0
    for t in range(mult, target + 1, mult):
        if n % t == 0:
            best = t
    assert best > 0, (n, target, mult)
    return best


def _params(*sem):
    return pltpu.CompilerParams(dimension_semantics=sem, vmem_limit_bytes=VMEM_LIMIT_BYTES)


def _sigmoid(x):
    return 1.0 / (1.0 + jnp.exp(-x))


def _dot(a, b, ta=False, tb=False):
    dims = (((0 if ta else 1,), (1 if tb else 0,)), ((), ()))
    return lax.dot_general(a.astype(BF16), b.astype(BF16), dims, preferred_element_type=F32)


def _dot_split(a, b):
    hi = a.astype(BF16)
    lo = (a - hi.astype(F32)).astype(BF16)
    return _dot(hi, b) + _dot(lo, b)


def _cast_place(x, layer, axis, me, name, after=None):
    blk = x.shape[1:]
    full = list(blk)
    full[axis] *= N_DEV
    if len(blk) == 2:
        r, c = blk
        tr = _tile(r, max(16, (512 * 1024) // c // 16 * 16), 16)
        steps = r // tr
        in_spec = pl.BlockSpec((None, tr, c), lambda i, me_ref: (layer, i, 0))
        if axis == 1:
            out_spec = pl.BlockSpec((tr, c), lambda i, me_ref: (i, me_ref[0]))
        else:
            out_spec = pl.BlockSpec((tr, c), lambda i, me_ref: (me_ref[0] * steps + i, 0))
    else:
        assert len(blk) == 3 and axis == 1
        steps = 1
        in_spec = pl.BlockSpec((None, *blk), lambda i, me_ref: (layer, 0, 0, 0))
        out_spec = pl.BlockSpec(blk, lambda i, me_ref: (0, me_ref[0], 0))

    def body(me_ref, x_ref, *rest):
        rest[-1][...] = x_ref[...].astype(BF16)

    return pl.pallas_call(
        body, name=name,
        grid_spec=pltpu.PrefetchScalarGridSpec(num_scalar_prefetch=1, grid=(steps,),
                                               in_specs=[in_spec] + _after_specs(after), out_specs=out_spec),
        out_shape=jax.ShapeDtypeStruct(tuple(full), BF16),
        compiler_params=_params("arbitrary"),
    )(me, x, *_after_operands(after))


def _after_operands(after):
    return [] if after is None else [after]


def _after_specs(after):
    return [] if after is None else [pl.BlockSpec(memory_space=pl.ANY)]


def _matmul(a, b, *, ta=False, tb=False, out_dtype=F32, add=None, after=None, name, tm=1088, tn=1024, tk=2048):
    a_list = list(a) if isinstance(a, (list, tuple)) else None
    b_list = list(b) if isinstance(b, (list, tuple)) else None
    assert not (a_list and ta) and not (b_list and tb) and not (a_list and b_list)
    if a_list:
        m, k = a_list[0].shape[0], sum(p.shape[1] for p in a_list)
    else:
        m, k = (a.shape[1], a.shape[0]) if ta else a.shape
    if b_list:
        n = sum(p.shape[1] for p in b_list)
    else:
        n = b.shape[0] if tb else b.shape[1]
    tm = _tile(m, tm, 128 if ta else 16)
    tn = _tile(n, tn, 128)
    tk = _tile(k, tk, 128 if (not ta or tb) else 16)
    if a_list:
        while any(p.shape[1] % tk for p in a_list):
            tk //= 2
    if b_list:
        while any(p.shape[1] % tn for p in b_list):
            tn //= 2
    nk = k // tk

    def ranges(pieces, tile):
        out, start = [], 0
        for p in pieces:
            out.append((start, start + p.shape[1] // tile))
            start = out[-1][1]
        return out

    if a_list:
        a_ranges = ranges(a_list, tk)
        a_specs = [pl.BlockSpec((tm, tk), lambda i, j, kk, s=s, e=e: (i, jnp.clip(kk - s, 0, e - s - 1)))
                   for s, e in a_ranges]
    else:
        a_specs = [pl.BlockSpec((tk, tm), lambda i, j, kk: (kk, i)) if ta
                   else pl.BlockSpec((tm, tk), lambda i, j, kk: (i, kk))]
    if b_list:
        b_ranges = ranges(b_list, tn)
        b_specs = [pl.BlockSpec((tk, tn), lambda i, j, kk, s=s, e=e: (
            jnp.where(jnp.logical_and(j >= s, j < e), kk, 0), jnp.clip(j - s, 0, e - s - 1))) for s, e in b_ranges]
    else:
        b_specs = [pl.BlockSpec((tn, tk), lambda i, j, kk: (j, kk)) if tb
                   else pl.BlockSpec((tk, tn), lambda i, j, kk: (kk, j))]
    o_spec = pl.BlockSpec((tm, tn), lambda i, j, kk: (i, j))
    has_add = add is not None
    na, nb = len(a_specs), len(b_specs)
    simple = nk == 1 and not a_list and not b_list

    def body(*refs):
        a_refs, b_refs = refs[:na], refs[na:na + nb]
        add_ref = refs[na + nb] if has_add else None
        o_ref, acc_ref = refs[-2], refs[-1]
        j, kk = pl.program_id(1), pl.program_id(2)

        def finish(total):
            if has_add:
                total = total + add_ref[...]
            o_ref[...] = total.astype(out_dtype)

        if simple:
            finish(_dot(a_refs[0][...], b_refs[0][...], ta, tb))
            return

        @pl.when(kk == 0)
        def _():
            acc_ref[...] = jnp.zeros_like(acc_ref)

        if a_list:
            for (s, e), a_ref in zip(a_ranges, a_refs):
                @pl.when(jnp.logical_and(kk >= s, kk < e))
                def _(a_ref=a_ref):
                    acc_ref[...] += _dot(a_ref[...], b_refs[0][...], ta, tb)
        elif b_list:
            for (s, e), b_ref in zip(b_ranges, b_refs):
                @pl.when(jnp.logical_and(j >= s, j < e))
                def _(b_ref=b_ref):
                    acc_ref[...] += _dot(a_refs[0][...], b_ref[...], ta, tb)
        else:
            acc_ref[...] += _dot(a_refs[0][...], b_refs[0][...], ta, tb)

        @pl.when(kk == nk - 1)
        def _():
            finish(acc_ref[...])

    operands = (a_list or [a]) + (b_list or [b]) + ([add] if has_add else []) + _after_operands(after)
    in_specs = a_specs + b_specs + ([o_spec] if has_add else []) + _after_specs(after)
    return pl.pallas_call(
        body, name=name, grid=(m // tm, n // tn, nk),
        in_specs=in_specs, out_specs=o_spec,
        out_shape=jax.ShapeDtypeStruct((m, n), out_dtype),
        scratch_shapes=[pltpu.VMEM((8, 128) if simple else (tm, tn), F32)],
        compiler_params=_params("parallel", "parallel", "arbitrary"),
    )(*operands)


def _rms_fwd(x, gain, name, after=None):
    l, d = x.shape
    tr = _tile(l, 272, 16)

    def body(x_ref, g_ref, *rest):
        o_ref = rest[-1]
        xv = x_ref[...]
        r = lax.rsqrt(jnp.mean(xv * xv, axis=-1, keepdims=True) + RMS_EPS)
        o_ref[...] = (xv * r * g_ref[...]).astype(BF16)

    return pl.pallas_call(
        body, name=name, grid=(l // tr,),
        in_specs=[pl.BlockSpec((tr, d), lambda i: (i, 0)), pl.BlockSpec((1, d), lambda i: (0, 0))] + _after_specs(after),
        out_specs=pl.BlockSpec((tr, d), lambda i: (i, 0)),
        out_shape=jax.ShapeDtypeStruct((l, d), BF16),
        compiler_params=_params("parallel"),
    )(x, gain, *_after_operands(after))


def _rms_bwd(dh, x, gain, gout, name):
    l, d = x.shape
    tr = _tile(l, 272, 16)

    def body(dh_ref, x_ref, g_ref, go_ref, gx_ref, gxb_ref, dg_ref):
        xv = x_ref[...]
        r = lax.rsqrt(jnp.mean(xv * xv, axis=-1, keepdims=True) + RMS_EPS)
        nv = xv * r
        dhv = dh_ref[...]
        dn = dhv * g_ref[...]
        dx = r * (dn - nv * jnp.mean(dn * nv, axis=-1, keepdims=True))
        gx = go_ref[...] + dx
        gx_ref[...] = gx
        gxb_ref[...] = gx.astype(BF16)
        part = jnp.sum(dhv * nv, axis=0, keepdims=True)

        @pl.when(pl.program_id(0) == 0)
        def _():
            dg_ref[...] = part

        @pl.when(pl.program_id(0) > 0)
        def _():
            dg_ref[...] += part

    row = pl.BlockSpec((tr, d), lambda i: (i, 0))
    vec = pl.BlockSpec((1, d), lambda i: (0, 0))
    return pl.pallas_call(
        body, name=name, grid=(l // tr,),
        in_specs=[row, row, vec, row], out_specs=[row, row, vec],
        out_shape=[jax.ShapeDtypeStruct((l, d), F32), jax.ShapeDtypeStruct((l, d), BF16),
                   jax.ShapeDtypeStruct((1, d), F32)],
        compiler_params=_params("arbitrary"),
    )(dh, x, gain, gout)


def _final_loss(x, gain, target, row_lo, row_hi, name):
    l, d = x.shape
    tr = _tile(l, 272, 16)

    def body(x_ref, g_ref, t_ref, gx_ref, gxb_ref, dg_ref, loss_ref):
        i = pl.program_id(0)
        xv = x_ref[...]
        r = lax.rsqrt(jnp.mean(xv * xv, axis=-1, keepdims=True) + RMS_EPS)
        nv = xv * r
        gv = g_ref[...]
        rows = i * tr + lax.broadcasted_iota(jnp.int32, (tr, 1), 0)
        valid = jnp.logical_and(rows >= row_lo, rows < row_hi)
        err = jnp.where(valid, nv * gv - t_ref[...], 0.0)
        dy = err * (1.0 / d)
        dn = dy * gv
        gx = r * (dn - nv * jnp.mean(dn * nv, axis=-1, keepdims=True))
        gx_ref[...] = gx
        gxb_ref[...] = gx.astype(BF16)
        part = jnp.sum(dy * nv, axis=0, keepdims=True)
        lpart = jnp.full((1, 128), 0.5 * jnp.sum(jnp.mean(err * err, axis=-1, keepdims=True)), F32)

        @pl.when(i == 0)
        def _():
            dg_ref[...] = part
            loss_ref[...] = lpart

        @pl.when(i > 0)
        def _():
            dg_ref[...] += part
            loss_ref[...] += lpart

    row = pl.BlockSpec((tr, d), lambda i: (i, 0))
    vec = pl.BlockSpec((1, d), lambda i: (0, 0))
    return pl.pallas_call(
        body, name=name, grid=(l // tr,),
        in_specs=[row, vec, row], out_specs=[row, row, vec, pl.BlockSpec((1, 128), lambda i: (0, 0))],
        out_shape=[jax.ShapeDtypeStruct((l, d), F32), jax.ShapeDtypeStruct((l, d), BF16),
                   jax.ShapeDtypeStruct((1, d), F32), jax.ShapeDtypeStruct((1, 128), F32)],
        compiler_params=_params("arbitrary"),
    )(x, gain, target)


def _log1m_beta(z):
    return -(jnp.maximum(z, 0.0) + jnp.log(1.0 + jnp.exp(-jnp.abs(z))))


def _tri(n, relation):
    r = lax.broadcasted_iota(jnp.int32, (n, n), 0)
    c = lax.broadcasted_iota(jnp.int32, (n, n), 1)
    return jnp.where(relation(r, c), 1.0, 0.0).astype(BF16)


def _running_sum(x, tri, backward, split):
    n = tri.shape[0]
    blocks = [x[:, s * n:(s + 1) * n] for s in range(x.shape[1] // n)]
    inner = [(_dot_split if split else _dot)(b, tri) for b in blocks]
    totals = [jnp.sum(b, axis=1, keepdims=True) for b in blocks]
    order = list(reversed(range(len(blocks)))) if backward else list(range(len(blocks)))
    out, offset = [None] * len(blocks), None
    for s in order:
        out[s] = inner[s] if offset is None else inner[s] + offset
        offset = totals[s] if offset is None else offset + totals[s]
    return (out[0] if len(out) == 1 else jnp.concatenate(out, axis=1)), offset


def _head(hh):
    return slice(hh * HEAD_DIM, (hh + 1) * HEAD_DIM)


def _heads(x, hb):
    return jnp.stack([x[:, _head(hh)] for hh in range(hb)], axis=0)


def _bdot(a, b, ta=False, tb=False):
    dims = (((1 if ta else 2,), (2 if tb else 1,)), ((0,), (0,)))
    return lax.dot_general(a.astype(BF16), b.astype(BF16), dims, preferred_element_type=F32)


def _attn_specs(l, hb, n_heads):
    width = hb * HEAD_DIM
    groups = n_heads // hb

    def tile(section):
        return pl.BlockSpec((ATT_BLOCK, width), lambda h, i: (i, section * groups + h))

    def slab(section):
        return pl.BlockSpec((l, width), lambda h, i: (0, section * groups + h))

    return tile, slab


def _attn_fwd(zin, n_heads, name):
    l = zin.shape[0]
    t = ATT_BLOCK
    hb = ATT_HEADS if n_heads % ATT_HEADS == 0 else 1
    scale = HEAD_DIM ** -0.5

    def body(q_ref, k_ref, v_ref, g_ref, o_ref, oa_ref, tot_ref):
        i = pl.program_id(1)
        after = _tri(t, lambda r, c: r > c)
        causal = (lax.broadcasted_iota(jnp.int32, (t, t), 1) < lax.broadcasted_iota(jnp.int32, (t, t), 0))[None]
        q = _heads(q_ref[...], hb).astype(BF16)

        def tile(k0, w, carry, diagonal=False):
            run, acc = carry
            z = _bdot(q, _heads(k_ref[pl.ds(k0, w), :], hb), tb=True) * scale
            lb_all = _log1m_beta(z)
            lb = jnp.where(causal, lb_all, 0.0) if diagonal else lb_all
            between, total = _running_sum(lb.reshape(hb * t, w), after, True, True)
            a = jnp.exp(z + lb_all + between.reshape(hb, t, w) + run)
            if diagonal:
                a = jnp.where(causal, a, 0.0)
            return run + total.reshape(hb, t, 1), acc + _bdot(a, _heads(v_ref[pl.ds(k0, w), :], hb))

        rem = i % 4
        carry = tile(pl.multiple_of(i * t, t), t, (jnp.zeros((hb, t, 1), F32), jnp.zeros((hb, t, HEAD_DIM), F32)), True)
        carry = lax.cond(rem % 2 == 1, lambda c: tile(pl.multiple_of((i - 1) * t, t), t, c), lambda c: c, carry)
        carry = lax.cond(rem >= 2, lambda c: tile(pl.multiple_of((i - rem % 2 - 2) * t, 2 * t), 2 * t, c),
                         lambda c: c, carry)
        run, o = lax.fori_loop(
            0, i // 4, lambda it, c: tile(pl.multiple_of((i - rem - 4 * it - 4) * t, 4 * t), 4 * t, c), carry)
        for hh in range(hb):
            gate = g_ref[:, _head(hh)]
            o_ref[:, _head(hh)] = o[hh]
            oa_ref[:, _head(hh)] = (o[hh] * (gate * _sigmoid(gate))).astype(BF16)
            tot_ref[:, _head(hh)] = jnp.broadcast_to(run[hh], (t, HEAD_DIM))

    tile_spec, slab_spec = _attn_specs(l, hb, n_heads)
    width = n_heads * HEAD_DIM
    return pl.pallas_call(
        body, name=name, grid=(n_heads // hb, l // t),
        in_specs=[tile_spec(0), slab_spec(1), slab_spec(2), tile_spec(3)],
        out_specs=[tile_spec(0), tile_spec(0), tile_spec(0)],
        out_shape=[jax.ShapeDtypeStruct((l, width), F32), jax.ShapeDtypeStruct((l, width), BF16),
                   jax.ShapeDtypeStruct((l, width), F32)],
        compiler_params=_params("parallel", "arbitrary"),
    )(zin, zin, zin, zin)


def _attn_bwd(zin, o, tot, doa, n_heads, name, after=None):
    l = zin.shape[0]
    t = ATT_BLOCK
    nq = l // t
    hb = ATT_HEADS if n_heads % ATT_HEADS == 0 else 1
    scale = HEAD_DIM ** -0.5

    def body(q_ref, k_ref, v_ref, g_ref, o_ref, tot_ref, doa_ref, *rest):
        dq_ref, dk_ref, dv_ref, dg_ref, dk_acc, dv_acc = rest[-6:]
        i = pl.program_id(1)

        @pl.when(i == 0)
        def _():
            dk_acc[...] = jnp.zeros_like(dk_acc)
            dv_acc[...] = jnp.zeros_like(dv_acc)

        upto = _tri(t, lambda r, c: r <= c)
        before = _tri(t, lambda r, c: r < c)
        causal = (lax.broadcasted_iota(jnp.int32, (t, t), 1) < lax.broadcasted_iota(jnp.int32, (t, t), 0))[None]
        gate = g_ref[...]
        sg = _sigmoid(gate)
        doav = doa_ref[...]
        dg_ref[...] = (doav * o_ref[...] * (sg * (1.0 + gate * (1.0 - sg)))).astype(BF16)
        do = _heads(doav * (gate * sg), hb).astype(BF16)
        q = _heads(q_ref[...], hb).astype(BF16)
        total = _heads(tot_ref[...], hb)[:, :, 0:1]

        def tile(k0, w, carry, diagonal=False):
            run, pre, dq = carry
            kb = _heads(k_ref[pl.ds(k0, w), :], hb).astype(BF16)
            z = _bdot(q, kb, tb=True) * scale
            lb_all = _log1m_beta(z)
            lb = jnp.where(causal, lb_all, 0.0) if diagonal else lb_all
            beta = jnp.exp(z + lb_all)
            sofar, lb_total = _running_sum(lb.reshape(hb * t, w), upto, False, True)
            a = beta * jnp.exp(total - run - sofar.reshape(hb, t, w))
            if diagonal:
                a = jnp.where(causal, a, 0.0)
            e = a * _bdot(do, _heads(v_ref[pl.ds(k0, w), :], hb), tb=True)
            earlier, e_total = _running_sum(e.reshape(hb * t, w), before, False, False)
            dz = (e * (1.0 - beta) - beta * (pre + earlier.reshape(hb, t, w))) * scale
            if diagonal:
                dz = jnp.where(causal, dz, 0.0)
            dv = _bdot(a, do, ta=True)
            dk = _bdot(dz, q, ta=True)
            for hh in range(hb):
                dv_acc[pl.ds(k0, w), _head(hh)] += dv[hh]
                dk_acc[pl.ds(k0, w), _head(hh)] += dk[hh]
            return run + lb_total.reshape(hb, t, 1), pre + e_total.reshape(hb, t, 1), dq + _bdot(dz, kb)

        rem = i % 4
        zero = jnp.zeros((hb, t, 1), F32)
        carry = lax.fori_loop(0, i // 4, lambda j, c: tile(pl.multiple_of(j * 4 * t, 4 * t), 4 * t, c),
                              (zero, zero, jnp.zeros((hb, t, HEAD_DIM), F32)))
        carry = lax.cond(rem >= 2, lambda c: tile(pl.multiple_of((i - rem) * t, 2 * t), 2 * t, c), lambda c: c, carry)
        carry = lax.cond(rem % 2 == 1, lambda c: tile(pl.multiple_of((i - 1) * t, t), t, c), lambda c: c, carry)
        _, _, dq = tile(pl.multiple_of(i * t, t), t, carry, True)
        for hh in range(hb):
            dq_ref[:, _head(hh)] = dq[hh].astype(BF16)

        @pl.when(i == nq - 1)
        def _():
            dk_ref[...] = dk_acc[...].astype(BF16)
            dv_ref[...] = dv_acc[...].astype(BF16)

    tile_spec, slab_spec = _attn_specs(l, hb, n_heads)
    out = jax.ShapeDtypeStruct((l, n_heads * HEAD_DIM), BF16)
    return pl.pallas_call(
        body, name=name, grid=(n_heads // hb, nq),
        in_specs=[tile_spec(0), slab_spec(1), slab_spec(2), tile_spec(3), tile_spec(0), tile_spec(0), tile_spec(0)]
        + _after_specs(after),
        out_specs=[tile_spec(0), slab_spec(0), slab_spec(0), tile_spec(0)],
        out_shape=[out, out, out, out],
        scratch_shapes=[pltpu.VMEM((l, hb * HEAD_DIM), F32), pltpu.VMEM((l, hb * HEAD_DIM), F32)],
        compiler_params=_params("parallel", "arbitrary"),
    )(zin, zin, zin, zin, o, tot, doa, *_after_operands(after))


def _shift_rows(x, k, down):
    n = x.shape[0]
    rows = lax.broadcasted_iota(jnp.int32, x.shape, 0)
    if down:
        return jnp.where(rows >= k, pltpu.roll(x, k, 0), 0.0)
    return jnp.where(rows < n - k, pltpu.roll(x, n - k, 0), 0.0)


def _window_sum(x, g, down):
    result = x
    total = x
    for step, k in enumerate((1, 2, 4, 8)):
        total = total + _shift_rows(total, k, down)
        result = jnp.where(g >= step, total, result)
    return result


def _pooled(u, g):
    rows = lax.broadcasted_iota(jnp.int32, (u.shape[0], 1), 0)
    window = jnp.left_shift(2, g)
    cnt = jnp.minimum(rows + 1, window).astype(F32)
    return _window_sum(u, g, True) / cnt - u, cnt


def _pool_fwd(zin, pool_w, pool_scale, u_off, name, after=None):
    l = zin.shape[0]
    n_groups, gd, _ = pool_w.shape

    def body(u_ref, g_ref, w_ref, s_ref, *rest):
        o_ref = rest[-1]
        g = pl.program_id(0)
        pooled, _ = _pooled(u_ref[...], g)
        mixed = _dot(pooled, w_ref[...])
        gate = g_ref[...]
        o_ref[...] = (mixed * s_ref[...] * (gate * _sigmoid(gate))).astype(BF16)

    return pl.pallas_call(
        body, name=name, grid=(n_groups,),
        in_specs=[pl.BlockSpec((l, gd), lambda g: (0, u_off + g)),
                  pl.BlockSpec((l, gd), lambda g: (0, u_off + n_groups + g)),
                  pl.BlockSpec((None, gd, gd), lambda g: (g, 0, 0)),
                  pl.BlockSpec((1, gd), lambda g: (0, g))] + _after_specs(after),
        out_specs=pl.BlockSpec((l, gd), lambda g: (0, g)),
        out_shape=jax.ShapeDtypeStruct((l, n_groups * gd), BF16),
        compiler_params=_params("parallel"),
    )(zin, zin, pool_w, pool_scale, *_after_operands(after))


def _pool_bwd(zin, dop, pool_w, pool_scale, u_off, name):
    l = zin.shape[0]
    n_groups, gd, _ = pool_w.shape

    def body(u_ref, g_ref, w_ref, s_ref, d_ref, du_ref, dg_ref, dw_ref, ds_ref):
        g = pl.program_id(0)
        pooled, cnt = _pooled(u_ref[...], g)
        w = w_ref[...]
        mixed = _dot(pooled, w)
        gate = g_ref[...]
        sg = _sigmoid(gate)
        silu = gate * sg
        dop_v = d_ref[...]
        sc = s_ref[...]
        ds_ref[...] = jnp.sum(dop_v * mixed * silu, axis=0, keepdims=True)
        dg_ref[...] = (dop_v * mixed * sc * (sg * (1.0 + gate * (1.0 - sg)))).astype(BF16)
        dmixed = dop_v * sc * silu
        dw_ref[...] = _dot(pooled, dmixed, ta=True).astype(BF16)
        dpooled = _dot(dmixed, w, tb=True)
        du_ref[...] = (_window_sum(dpooled / cnt, g, False) - dpooled).astype(BF16)

    slab = pl.BlockSpec((l, gd), lambda g: (0, g))
    return pl.pallas_call(
        body, name=name, grid=(n_groups,),
        in_specs=[pl.BlockSpec((l, gd), lambda g: (0, u_off + g)),
                  pl.BlockSpec((l, gd), lambda g: (0, u_off + n_groups + g)),
                  pl.BlockSpec((None, gd, gd), lambda g: (g, 0, 0)),
                  pl.BlockSpec((1, gd), lambda g: (0, g)), slab],
        out_specs=[slab, slab, pl.BlockSpec((None, gd, gd), lambda g: (g, 0, 0)),
                   pl.BlockSpec((1, gd), lambda g: (0, g))],
        out_shape=[jax.ShapeDtypeStruct((l, n_groups * gd), BF16), jax.ShapeDtypeStruct((l, n_groups * gd), BF16),
                   jax.ShapeDtypeStruct(pool_w.shape, BF16), jax.ShapeDtypeStruct((1, n_groups * gd), F32)],
        compiler_params=_params("parallel"),
    )(zin, zin, pool_w, pool_scale, dop)


def _merge_fwd(oa, op, w_au, w_pu, zin, name):
    l, wa = oa.shape
    wp = op.shape[1]
    d = w_au.shape[1]
    tm = _tile(l, 1088, 16)
    tn = _tile(d, 512, 128)
    ma_off = (zin.shape[1] - 2 * d) // tn

    def body(oa_ref, op_ref, wa_ref, wp_ref, ma_ref, mp_ref, ya_ref, yp_ref, mg_ref):
        ya = _dot(oa_ref[...], wa_ref[...])
        yp = _dot(op_ref[...], wp_ref[...])
        ya_ref[...] = ya.astype(BF16)
        yp_ref[...] = yp.astype(BF16)
        mg_ref[...] = (_sigmoid(ma_ref[...]) * ya + _sigmoid(mp_ref[...]) * yp).astype(BF16)

    tile = pl.BlockSpec((tm, tn), lambda i, j: (i, j))
    return pl.pallas_call(
        body, name=name, grid=(l // tm, d // tn),
        in_specs=[pl.BlockSpec((tm, wa), lambda i, j: (i, 0)), pl.BlockSpec((tm, wp), lambda i, j: (i, 0)),
                  pl.BlockSpec((wa, tn), lambda i, j: (0, j)), pl.BlockSpec((wp, tn), lambda i, j: (0, j)),
                  pl.BlockSpec((tm, tn), lambda i, j: (i, ma_off + j)),
                  pl.BlockSpec((tm, tn), lambda i, j: (i, ma_off + d // tn + j))],
        out_specs=[tile, tile, tile],
        out_shape=[jax.ShapeDtypeStruct((l, d), BF16)] * 3,
        compiler_params=_params("parallel", "parallel"),
    )(oa, op, w_au, w_pu, zin, zin)


def _merge_bwd(gout, w_out, zin, ya, yp, name):
    l, d = gout.shape
    tm = _tile(l, 1088, 16)
    tn = _tile(d, 512, 128)
    ma_off = (zin.shape[1] - 2 * d) // tn

    def body(g_ref, w_ref, ma_ref, mp_ref, ya_ref, yp_ref, dya_ref, dyp_ref, dma_ref, dmp_ref):
        dm = _dot(g_ref[...], w_ref[...], tb=True)
        sa = _sigmoid(ma_ref[...])
        sp = _sigmoid(mp_ref[...])
        dya_ref[...] = (dm * sa).astype(BF16)
        dyp_ref[...] = (dm * sp).astype(BF16)
        dma_ref[...] = (dm * ya_ref[...].astype(F32) * (sa * (1.0 - sa))).astype(BF16)
        dmp_ref[...] = (dm * yp_ref[...].astype(F32) * (sp * (1.0 - sp))).astype(BF16)

    tile = pl.BlockSpec((tm, tn), lambda i, j: (i, j))
    out = jax.ShapeDtypeStruct((l, d), BF16)
    return pl.pallas_call(
        body, name=name, grid=(l // tm, d // tn),
        in_specs=[pl.BlockSpec((tm, d), lambda i, j: (i, 0)), pl.BlockSpec((tn, d), lambda i, j: (j, 0)),
                  pl.BlockSpec((tm, tn), lambda i, j: (i, ma_off + j)),
                  pl.BlockSpec((tm, tn), lambda i, j: (i, ma_off + d // tn + j)), tile, tile],
        out_specs=[tile, tile, tile, tile],
        out_shape=[out, out, out, out],
        compiler_params=_params("parallel", "parallel"),
    )(gout, w_out, zin, zin, ya, yp)


def _adamw(parts, w, m, v, name, first=0, into=None):
    n_arrays, n_parts, r, c = parts.shape
    tr = _tile(r, max(8, (128 * 1024) // c // 8 * 8), 8)
    bias1 = 1.0 - ADAM_B1 ** ADAM_STEP
    bias2 = 1.0 - ADAM_B2 ** ADAM_STEP

    def body(p_ref, w_ref, m_ref, v_ref, *rest):
        g_ref, d_ref, nm_ref, nv_ref = rest[-4:]
        g = p_ref[0].astype(F32)
        for j in range(1, n_parts):
            g = g + p_ref[j].astype(F32)
        nm = ADAM_B1 * m_ref[...] + (1.0 - ADAM_B1) * g
        nv = ADAM_B2 * v_ref[...] + (1.0 - ADAM_B2) * (g * g)
        g_ref[...] = g
        nm_ref[...] = nm
        nv_ref[...] = nv
        d_ref[...] = -ADAM_LR * ((nm / bias1) / (jnp.sqrt(nv / bias2) + ADAM_EPS) + ADAM_WD * w_ref[...])

    tile = pl.BlockSpec((None, tr, c), lambda a, i: (a + first, i, 0))
    out = jax.ShapeDtypeStruct(w.shape, F32)
    kept = [] if into is None else list(into)
    return pl.pallas_call(
        body, name=name, grid=(n_arrays, r // tr),
        in_specs=[pl.BlockSpec((None, n_parts, tr, c), lambda a, i: (a, 0, i, 0)), tile, tile, tile]
        + [pl.BlockSpec(memory_space=pl.ANY)] * len(kept),
        out_specs=[tile, tile, tile, tile], out_shape=[out, out, out, out],
        input_output_aliases={4 + k: k for k in range(len(kept))},
        compiler_params=_params("parallel", "parallel"),
    )(parts, w, m, v, *kept)


def _position():
    return lax.axis_index("x"), lax.axis_index("y"), lax.axis_index("c")


def _block_of(ref, axis, size, index):
    idx = [slice(None)] * len(ref.shape)
    idx[axis] = pl.ds(index * size, size)
    return ref.at[tuple(idx)]


HBM_SPEC = pl.BlockSpec(memory_space=pltpu.HBM)
SEM_SPEC = pl.BlockSpec(memory_space=pltpu.SEMAPHORE)
SIDE_EFFECT = pltpu.CompilerParams(has_side_effects=pltpu.SideEffectType.DATAFLOW_SIDE_EFFECTING)


def _split_start(make_copies, n_copies, buffers, name):
    n = len(buffers)

    def body(*refs):
        send_sems, recv_sems = refs[n], refs[n + 1]
        for cp in make_copies(refs[:n], send_sems, recv_sems):
            cp.start()
        refs[-1][...] = jnp.zeros_like(refs[-1])

    sems = pltpu.SemaphoreType.DMA((n_copies,))
    return pl.pallas_call(
        body, name=name, in_specs=[HBM_SPEC] * n,
        out_shape=(sems, sems, *[pltpu.HBM(b.shape, b.dtype) for b in buffers], jax.ShapeDtypeStruct((8, 128), F32)),
        out_specs=(SEM_SPEC, SEM_SPEC, *[HBM_SPEC] * n, pl.BlockSpec(memory_space=pltpu.VMEM)),
        input_output_aliases={i: 2 + i for i in range(n)}, compiler_params=SIDE_EFFECT,
    )(*[pltpu.with_memory_space_constraint(b, pltpu.HBM) for b in buffers])


def _split_wait(make_copies, started, after, name):
    send_sems, recv_sems, *buffers = started[:-1]
    n = len(buffers)

    def body(*refs):
        copies = make_copies(refs[:n], refs[n], refs[n + 1])
        for cp in copies:
            cp.wait_send()
        for cp in copies:
            cp.wait_recv()

    return pl.pallas_call(
        body, name=name, in_specs=[HBM_SPEC] * n + [SEM_SPEC, SEM_SPEC, pl.BlockSpec(memory_space=pl.ANY)],
        out_shape=[pltpu.HBM(b.shape, b.dtype) for b in buffers], out_specs=[HBM_SPEC] * n,
        input_output_aliases={i: i for i in range(n)}, compiler_params=SIDE_EFFECT,
    )(*buffers, send_sems, recv_sems, after)


def _gather_copies(axes, sizes, level):
    def make(fulls, send_sems, recv_sems):
        x, y, c = _position()
        chips = [(1 - x, y), (x, 1 - y), (1 - x, 1 - y)]
        copies = []
        for a, full in enumerate(fulls):
            def copy(k, block, to, full=full, a=a):
                rows = _block_of(full, axes[a], sizes[a], 4 * block[0] + 2 * block[1] + block[2])
                idx = a * (4 if level == 1 else 3) + k
                return pltpu.make_async_remote_copy(src_ref=rows, dst_ref=rows, send_sem=send_sems.at[idx],
                                                    recv_sem=recv_sems.at[idx], device_id=to, device_id_type=MESH)
            if level == 1:
                copies.append(copy(0, (x, y, c), (x, y, 1 - c)))
                copies += [copy(1 + j, (x, y, c), (*chip, c)) for j, chip in enumerate(chips)]
            else:
                copies += [copy(j, (*chip, c), (x, y, 1 - c)) for j, chip in enumerate(chips)]
        return copies
    return make


def _exchange_copies(axes, sizes, layer, n_src):
    flips = [(a, b, d) for a in (0, 1) for b in (0, 1) for d in (0, 1)][1:]

    def make(buffers, send_sems, recv_sems):
        x, y, c = _position()
        my_index = 4 * x + 2 * y + c
        copies = []
        for a in range(n_src):
            for k, flip in enumerate(flips):
                px, py, pc = x ^ flip[0], y ^ flip[1], c ^ flip[2]
                copies.append(pltpu.make_async_remote_copy(
                    src_ref=_block_of(buffers[a], axes[a], sizes[a], 4 * px + 2 * py + pc),
                    dst_ref=buffers[n_src + a].at[layer, my_index],
                    send_sem=send_sems.at[a * 7 + k], recv_sem=recv_sems.at[a * 7 + k],
                    device_id=(px, py, pc), device_id_type=MESH))
        return copies
    return make


def _allgather_small(v, name, after=None):
    r, c = v.shape
    flips = [(a, b, d) for a in (0, 1) for b in (0, 1) for d in (0, 1)][1:]

    def body(v_ref, *rest):
        out_ref, send_sems, recv_sems = rest[-3:]
        x, y, c_ = _position()
        my_index = 4 * x + 2 * y + c_
        out_ref[my_index] = v_ref[...]
        sends = []
        for k, flip in enumerate(flips):
            peer = (x ^ flip[0], y ^ flip[1], c_ ^ flip[2])
            cp = pltpu.make_async_remote_copy(
                src_ref=v_ref, dst_ref=out_ref.at[my_index],
                send_sem=send_sems.at[k], recv_sem=recv_sems.at[k], device_id=peer, device_id_type=MESH)
            cp.start()
            sends.append(cp)
        for k, flip in enumerate(flips):
            px, py, pc = x ^ flip[0], y ^ flip[1], c_ ^ flip[2]
            pltpu.make_async_remote_copy(
                src_ref=v_ref, dst_ref=out_ref.at[4 * px + 2 * py + pc],
                send_sem=send_sems.at[k], recv_sem=recv_sems.at[k],
                device_id=(px, py, pc), device_id_type=MESH).wait_recv()
        for cp in sends:
            cp.wait_send()

    return pl.pallas_call(
        body, name=name,
        in_specs=[pl.BlockSpec(memory_space=pltpu.VMEM)] + _after_specs(after),
        out_specs=pl.BlockSpec(memory_space=pltpu.VMEM),
        out_shape=jax.ShapeDtypeStruct((N_DEV, r, c), v.dtype),
        scratch_shapes=[pltpu.SemaphoreType.DMA((7,)), pltpu.SemaphoreType.DMA((7,))],
    )(v, *_after_operands(after))


def kernel(x, meta_tokens, norm_gain, w_in, pool_w, pool_scale, w_attn_up, w_pool_up, w_out, final_gain, loss_target, m_meta_tokens, m_norm_gain, m_w_in, m_pool_w, m_pool_scale, m_w_attn_up, m_w_pool_up, m_w_out, m_final_gain, v_meta_tokens, v_norm_gain, v_w_in, v_pool_w, v_pool_scale, v_w_attn_up, v_w_pool_up, v_w_out, v_final_gain):
    _, seq, d = x.shape
    n_meta = meta_tokens.shape[0]
    depth = w_in.shape[0]
    sb_width = w_attn_up.shape[1]
    pool_width = w_pool_up.shape[1]
    n_heads = sb_width // HEAD_DIM
    n_groups = pool_w.shape[1]
    gd = pool_w.shape[3]
    assert n_groups == len(POOL_WINDOWS) and gd * n_groups == pool_width
    assert w_in.shape[2] * N_DEV == 4 * sb_width + 2 * pool_width + 2 * d
    l_real = n_meta + seq
    l_pad = -(-l_real // ATT_BLOCK) * ATT_BLOCK
    my_index = 4 * lax.axis_index("x") + 2 * lax.axis_index("y") + lax.axis_index("c")

    me = jnp.reshape(my_index, (1,)).astype(jnp.int32)
    g_named = [("w_in", w_in), ("pool_w", pool_w), ("w_attn_up", w_attn_up), ("w_pool_up", w_pool_up), ("w_out", w_out)]
    g_axes = [1, 1, 1, 1, 0]
    g_sizes = [w.shape[1 + ax] for (_, w), ax in zip(g_named, g_axes)]
    level1 = _gather_copies(g_axes, g_sizes, 1)
    level2 = _gather_copies(g_axes, g_sizes, 2)

    def gather_start(i, after=None):
        fulls = [_cast_place(w, i, ax, me, "cast_" + nm, after) for (nm, w), ax in zip(g_named, g_axes)]
        return _split_start(level1, 4 * len(fulls), fulls, "gather1_start_%d" % i)

    def gather_forward(i, started, after):
        arrived = _split_wait(level1, started, after, "gather1_wait_%d" % i)
        return _split_start(level2, 3 * len(arrived), arrived, "gather2_start_%d" % i)

    meta_all = _allgather_small(meta_tokens, "allgather_meta")
    meta_full = jnp.transpose(meta_all, (1, 0, 2)).reshape(n_meta, d)
    first = gather_start(0, meta_all)

    pad_rows = l_pad - l_real
    hs = jnp.concatenate([meta_full, x[0], jnp.zeros((pad_rows, d), F32)], axis=0)
    target = jnp.concatenate([jnp.zeros((n_meta, d), F32), loss_target[0], jnp.zeros((pad_rows, d), F32)], axis=0)
    u_off = 4 * sb_width // gd
    saved, weights = [], []
    for i in range(depth):
        h = _rms_fwd(hs, norm_gain[i][None], "rms_fwd")
        if i == 0:
            second = gather_forward(0, first, h)
        weights.append(_split_wait(level2, second, target if i == 0 else h, "gather2_wait_%d" % i))
        wi, pw, wau, wpu, wo = weights[i]
        more = i + 1 < depth
        first = gather_start(i + 1, wi) if more else None
        zin = _matmul(h, wi, after=first[-1] if more else None, name="mm_zin")
        o, oa, tot = _attn_fwd(zin, n_heads, "attn_fwd")
        op = _pool_fwd(zin, pw, pool_scale[i][None], u_off, "pool_fwd")
        ya, yp, merged = _merge_fwd(oa, op, wau, wpu, zin, "merge_fwd")
        second = gather_forward(i + 1, first, merged) if more else None
        saved.append((hs, h, zin, o, tot, oa, op, ya, yp, merged))
        hs = _matmul(merged, wo, add=hs, after=second[-1] if more else None, name="mm_out")
    g, gb, d_final_gain, loss_part = _final_loss(hs, final_gain[None], target, n_meta, l_real, "final_loss")
    loss = lax.psum(loss_part[0, 0], ("x", "y", "c"))

    d_norm_gain = [None] * depth
    d_pool_scale = [None] * depth
    axes_a, axes_b = [1, 1, 0], [1, 1]
    blocks_a, blocks_b = [w_attn_up[0], w_pool_up[0], w_out[0]], [w_in[0], pool_w[0]]
    sizes_a = [b.shape[ax] for b, ax in zip(blocks_a, axes_a)]
    sizes_b = [b.shape[ax] for b, ax in zip(blocks_b, axes_b)]
    land_a = [lax.empty((depth, N_DEV, *b.shape), BF16) for b in blocks_a]
    land_b = [lax.empty((depth - 1, N_DEV, *b.shape), BF16) for b in blocks_b]
    land_b0 = [lax.empty((1, N_DEV, *b.shape), BF16) for b in blocks_b]

    def exchange_start(grads, landing, axes, sizes, slot, name):
        landing = [lax.dynamic_update_slice(
            zone, lax.dynamic_slice_in_dim(grad, my_index * size, size, ax)[None, None],
            (slot, my_index) + (0,) * grad.ndim) for zone, grad, ax, size in zip(landing, grads, axes, sizes)]
        copies = _exchange_copies(axes, sizes, slot, len(grads))
        return copies, _split_start(copies, 7 * len(grads), list(grads) + landing, name)

    def exchange_wait(pending, n_src, after, name):
        return _split_wait(pending[0], pending[1], after, name)[n_src:]

    pend_a = pend_b = None
    for i in reversed(range(depth)):
        wi, pw, wau, wpu, wo = weights[i]
        hs_in, h, zin, o, tot, oa, op, ya, yp, merged = saved[i]
        dya, dyp, dma, dmp = _merge_bwd(gb, wo, zin, ya, yp, "merge_bwd")
        dw_out = _matmul(merged, gb, ta=True, out_dtype=BF16, name="mm_dw_out", tm=1024, tk=1088)
        doa = _matmul(dya, wau, tb=True, name="mm_doa")
        dw_au = _matmul(oa, dya, ta=True, out_dtype=BF16, name="mm_dw_au", tm=1024, tk=1088)
        dop = _matmul(dyp, wpu, tb=True, name="mm_dop")
        dw_pu = _matmul(op, dyp, ta=True, out_dtype=BF16, name="mm_dw_pu", tm=1024, tk=1088)
        if pend_a is not None:
            land_a = exchange_wait(pend_a, 3, dw_pu, "exchange_a_wait_%d" % (i + 1))
        pend_a = exchange_start([dw_au, dw_pu, dw_out], land_a, axes_a, sizes_a, i, "exchange_a_start_%d" % i)
        dq, dk, dv, dga = _attn_bwd(zin, o, tot, doa, n_heads, "attn_bwd", after=pend_a[1][-1])
        du, dgp, dpw, dps = _pool_bwd(zin, dop, pw, pool_scale[i][None], u_off, "pool_bwd")
        dzin = [dq, dk, dv, dga, du, dgp, dma, dmp]
        dw_in = _matmul(h, dzin, ta=True, out_dtype=BF16, name="mm_dw_in", tm=1024, tk=1088)
        if pend_b is not None:
            land_b = exchange_wait(pend_b, 2, dw_in, "exchange_b_wait_%d" % (i + 1))
        pend_b = exchange_start([dw_in, dpw], land_b if i > 0 else land_b0, axes_b, sizes_b, max(i - 1, 0),
                                "exchange_b_start_%d" % i)
        dh = _matmul(dzin, wi, tb=True, after=pend_b[1][-1], name="mm_dh")
        g, gb, dng = _rms_bwd(dh, hs_in, norm_gain[i][None], g, "rms_bwd")
        d_norm_gain[i] = dng
        d_pool_scale[i] = dps
    land_a = exchange_wait(pend_a, 3, gb, "exchange_a_wait_0")
    grad_x = g[n_meta:l_real][None]

    def sharded(parts, w, m, v, name, first=0, into=None):
        flat = (depth, -1, w.shape[-1])
        return _adamw(parts.reshape(parts.shape[0], N_DEV, -1, w.shape[-1]), w.reshape(flat), m.reshape(flat),
                      v.reshape(flat), name, first, into)

    out_au = sharded(land_a[0], w_attn_up, m_w_attn_up, v_w_attn_up, "adamw_w_attn_up")
    out_pu = sharded(land_a[1], w_pool_up, m_w_pool_up, v_w_pool_up, "adamw_w_pool_up")
    out_wo = sharded(land_a[2], w_out, m_w_out, v_w_out, "adamw_w_out")
    out_wi = out_pw = None
    if depth > 1:
        out_wi = sharded(land_b[0], w_in, m_w_in, v_w_in, "adamw_w_in", 1)
        out_pw = sharded(land_b[1], pool_w, m_pool_w, v_pool_w, "adamw_pool_w", 1)
    land_b0 = exchange_wait(pend_b, 2, out_wo[0] if out_wi is None else out_wi[0], "exchange_b_wait_0")
    out_wi = sharded(land_b0[0], w_in, m_w_in, v_w_in, "adamw_w_in_0", 0, out_wi)
    out_pw = sharded(land_b0[1], pool_w, m_pool_w, v_pool_w, "adamw_pool_w_0", 0, out_pw)
    out_wi, out_pw, out_au, out_pu, out_wo = [
        [t.reshape(w.shape) for t in res] for res, w in
        zip([out_wi, out_pw, out_au, out_pu, out_wo], [w_in, pool_w, w_attn_up, w_pool_up, w_out])]

    zeros_ps = jnp.zeros((depth, d - pool_width), F32)
    small_rows = [jnp.concatenate(d_norm_gain, axis=0),
                  jnp.concatenate([jnp.concatenate(d_pool_scale, axis=0), zeros_ps], axis=1), d_final_gain]
    n_small = 2 * depth + 1
    small_pad = -(-n_small // 8) * 8
    small = jnp.concatenate(small_rows + [jnp.zeros((small_pad - n_small, d), F32), g[:n_meta]], axis=0)
    small_all = _allgather_small(small, "allgather_small", after=land_b0[0])

    def replicated(rows, width, w, m, v, name):
        parts = lax.slice(small_all, (0, rows[0], 0), (N_DEV, rows[1], width))
        return [t[0] for t in _adamw(parts[None], w[None], m[None], v[None], name)]

    out_ng = replicated((0, depth), d, norm_gain, m_norm_gain, v_norm_gain, "adamw_norm_gain")
    out_ps = replicated((depth, 2 * depth), pool_width, pool_scale, m_pool_scale, v_pool_scale, "adamw_pool_scale")
    out_fg = [t[0] for t in replicated((2 * depth, 2 * depth + 1), d, final_gain[None], m_final_gain[None],
                                       v_final_gain[None], "adamw_final_gain")]
    cols = d // N_DEV
    meta_parts = lax.dynamic_slice(small_all, (0, small_pad, my_index * cols), (N_DEV, n_meta, cols))
    out_meta = [t[0] for t in _adamw(meta_parts[None], meta_tokens[None], m_meta_tokens[None], v_meta_tokens[None],
                                     "adamw_meta")]

    by_weight = [out_meta, out_ng, out_wi, out_pw, out_ps, out_au, out_pu, out_wo, out_fg]
    return (loss, grad_x, *[o[0] for o in by_weight], *[o[1] for o in by_weight],
            *[o[2] for o in by_weight], *[o[3] for o in by_weight])
```

```python
import jax
import jax.numpy as jnp
from jax import lax
from jax.experimental import pallas as pl
from jax.experimental.pallas import tpu as pltpu

F32 = jnp.float32
BF16 = jnp.bfloat16
MESH = pl.DeviceIdType.MESH

N_DEV = 8
HEAD_DIM = 128
ATT_BLOCK = 128
ATT_HEADS = 4
ATT_HEADS_FWD = 8
POOL_WINDOWS = (2, 4, 8, 16)
RMS_EPS = 1e-6
ADAM_LR, ADAM_B1, ADAM_B2, ADAM_EPS, ADAM_WD, ADAM_STEP = 0.001, 0.9, 0.999, 1e-08, 0.01, 10
VMEM_LIMIT_BYTES = 56 * 1024 * 1024


def _tile(n, target, mult):
    if n <= target:
        return n
    best = 0
    for t in range(mult, target + 1, mult):
        if n % t == 0:
            best = t
    assert best > 0, (n, target, mult)
    return best


def _params(*sem):
    return pltpu.CompilerParams(dimension_semantics=sem, vmem_limit_bytes=VMEM_LIMIT_BYTES)


def _sigmoid(x):
    return 1.0 / (1.0 + jnp.exp(-x))


def _dot(a, b, ta=False, tb=False):
    dims = (((0 if ta else 1,), (1 if tb else 0,)), ((), ()))
    return lax.dot_general(a.astype(BF16), b.astype(BF16), dims, preferred_element_type=F32)


def _dot_split(a, b):
    hi = a.astype(BF16)
    lo = (a - hi.astype(F32)).astype(BF16)
    return _dot(hi, b) + _dot(lo, b)


def _cast_place(x, layer, axis, me, name, after=None):
    blk = x.shape[1:]
    full = list(blk)
    full[axis] *= N_DEV
    if len(blk) == 2:
        r, c = blk
        tr = _tile(r, max(16, (512 * 1024) // c // 16 * 16), 16)
        steps = r // tr
        in_spec = pl.BlockSpec((None, tr, c), lambda i, me_ref: (layer, i, 0))
        if axis == 1:
            out_spec = pl.BlockSpec((tr, c), lambda i, me_ref: (i, me_ref[0]))
        else:
            out_spec = pl.BlockSpec((tr, c), lambda i, me_ref: (me_ref[0] * steps + i, 0))
    else:
        assert len(blk) == 3 and axis == 1
        steps = 1
        in_spec = pl.BlockSpec((None, *blk), lambda i, me_ref: (layer, 0, 0, 0))
        out_spec = pl.BlockSpec(blk, lambda i, me_ref: (0, me_ref[0], 0))

    def body(me_ref, x_ref, *rest):
        rest[-1][...] = x_ref[...].astype(BF16)

    return pl.pallas_call(
        body, name=name,
        grid_spec=pltpu.PrefetchScalarGridSpec(num_scalar_prefetch=1, grid=(steps,),
                                               in_specs=[in_spec] + _after_specs(after), out_specs=out_spec),
        out_shape=jax.ShapeDtypeStruct(tuple(full), BF16),
        compiler_params=_params("arbitrary"),
    )(me, x, *_after_operands(after))


def _after_operands(after):
    return [] if after is None else [after]


def _after_specs(after):
    return [] if after is None else [pl.BlockSpec(memory_space=pl.ANY)]


def _matmul(a, b, *, ta=False, tb=False, out_dtype=F32, add=None, after=None, name, tm=1088, tn=1024, tk=2048):
    a_list = list(a) if isinstance(a, (list, tuple)) else None
    b_list = list(b) if isinstance(b, (list, tuple)) else None
    assert not (a_list and ta) and not (b_list and tb) and not (a_list and b_list)
    if a_list:
        m, k = a_list[0].shape[0], sum(p.shape[1] for p in a_list)
    else:
        m, k = (a.shape[1], a.shape[0]) if ta else a.shape
    if b_list:
        n = sum(p.shape[1] for p in b_list)
    else:
        n = b.shape[0] if tb else b.shape[1]
    tm = _tile(m, tm, 128 if ta else 16)
    tn = _tile(n, tn, 128)
    tk = _tile(k, tk, 128 if (not ta or tb) else 16)
    if a_list:
        while any(p.shape[1] % tk for p in a_list):
            tk //= 2
    if b_list:
        while any(p.shape[1] % tn for p in b_list):
            tn //= 2
    nk = k // tk

    def ranges(pieces, tile):
        out, start = [], 0
        for p in pieces:
            out.append((start, start + p.shape[1] // tile))
            start = out[-1][1]
        return out

    if a_list:
        a_ranges = ranges(a_list, tk)
        a_specs = [pl.BlockSpec((tm, tk), lambda i, j, kk, s=s, e=e: (i, jnp.clip(kk - s, 0, e - s - 1)))
                   for s, e in a_ranges]
    else:
        a_specs = [pl.BlockSpec((tk, tm), lambda i, j, kk: (kk, i)) if ta
                   else pl.BlockSpec((tm, tk), lambda i, j, kk: (i, kk))]
    if b_list:
        b_ranges = ranges(b_list, tn)
        b_specs = [pl.BlockSpec((tk, tn), lambda i, j, kk, s=s, e=e: (
            jnp.where(jnp.logical_and(j >= s, j < e), kk, 0), jnp.clip(j - s, 0, e - s - 1))) for s, e in b_ranges]
    else:
        b_specs = [pl.BlockSpec((tn, tk), lambda i, j, kk: (j, kk)) if tb
                   else pl.BlockSpec((tk, tn), lambda i, j, kk: (kk, j))]
    o_spec = pl.BlockSpec((tm, tn), lambda i, j, kk: (i, j))
    has_add = add is not None
    na, nb = len(a_specs), len(b_specs)
    simple = nk == 1 and not a_list and not b_list

    def body(*refs):
        a_refs, b_refs = refs[:na], refs[na:na + nb]
        add_ref = refs[na + nb] if has_add else None
        o_ref, acc_ref = refs[-2], refs[-1]
        j, kk = pl.program_id(1), pl.program_id(2)

        def finish(total):
            if has_add:
                total = total + add_ref[...]
            o_ref[...] = total.astype(out_dtype)

        if simple:
            finish(_dot(a_refs[0][...], b_refs[0][...], ta, tb))
            return

        @pl.when(kk == 0)
        def _():
            acc_ref[...] = jnp.zeros_like(acc_ref)

        if a_list:
            for (s, e), a_ref in zip(a_ranges, a_refs):
                @pl.when(jnp.logical_and(kk >= s, kk < e))
                def _(a_ref=a_ref):
                    acc_ref[...] += _dot(a_ref[...], b_refs[0][...], ta, tb)
        elif b_list:
            for (s, e), b_ref in zip(b_ranges, b_refs):
                @pl.when(jnp.logical_and(j >= s, j < e))
                def _(b_ref=b_ref):
                    acc_ref[...] += _dot(a_refs[0][...], b_ref[...], ta, tb)
        else:
            acc_ref[...] += _dot(a_refs[0][...], b_refs[0][...], ta, tb)

        @pl.when(kk == nk - 1)
        def _():
            finish(acc_ref[...])

    operands = (a_list or [a]) + (b_list or [b]) + ([add] if has_add else []) + _after_operands(after)
    in_specs = a_specs + b_specs + ([o_spec] if has_add else []) + _after_specs(after)
    return pl.pallas_call(
        body, name=name, grid=(m // tm, n // tn, nk),
        in_specs=in_specs, out_specs=o_spec,
        out_shape=jax.ShapeDtypeStruct((m, n), out_dtype),
        scratch_shapes=[pltpu.VMEM((8, 128) if simple else (tm, tn), F32)],
        compiler_params=_params("parallel", "parallel", "arbitrary"),
    )(*operands)


def _rms_fwd(x, gain, name, after=None):
    l, d = x.shape
    tr = _tile(l, 272, 16)

    def body(x_ref, g_ref, *rest):
        o_ref = rest[-1]
        xv = x_ref[...]
        r = lax.rsqrt(jnp.mean(xv * xv, axis=-1, keepdims=True) + RMS_EPS)
        o_ref[...] = (xv * r * g_ref[...]).astype(BF16)

    return pl.pallas_call(
        body, name=name, grid=(l // tr,),
        in_specs=[pl.BlockSpec((tr, d), lambda i: (i, 0)), pl.BlockSpec((1, d), lambda i: (0, 0))] + _after_specs(after),
        out_specs=pl.BlockSpec((tr, d), lambda i: (i, 0)),
        out_shape=jax.ShapeDtypeStruct((l, d), BF16),
        compiler_params=_params("parallel"),
    )(x, gain, *_after_operands(after))


def _rms_bwd(dh, x, gain, gout, name):
    l, d = x.shape
    tr = _tile(l, 272, 16)

    def body(dh_ref, x_ref, g_ref, go_ref, gx_ref, gxb_ref, dg_ref):
        xv = x_ref[...]
        r = lax.rsqrt(jnp.mean(xv * xv, axis=-1, keepdims=True) + RMS_EPS)
        nv = xv * r
        dhv = dh_ref[...]
        dn = dhv * g_ref[...]
        dx = r * (dn - nv * jnp.mean(dn * nv, axis=-1, keepdims=True))
        gx = go_ref[...] + dx
        gx_ref[...] = gx
        gxb_ref[...] = gx.astype(BF16)
        part = jnp.sum(dhv * nv, axis=0, keepdims=True)

        @pl.when(pl.program_id(0) == 0)
        def _():
            dg_ref[...] = part

        @pl.when(pl.program_id(0) > 0)
        def _():
            dg_ref[...] += part

    row = pl.BlockSpec((tr, d), lambda i: (i, 0))
    vec = pl.BlockSpec((1, d), lambda i: (0, 0))
    return pl.pallas_call(
        body, name=name, grid=(l // tr,),
        in_specs=[row, row, vec, row], out_specs=[row, row, vec],
        out_shape=[jax.ShapeDtypeStruct((l, d), F32), jax.ShapeDtypeStruct((l, d), BF16),
                   jax.ShapeDtypeStruct((1, d), F32)],
        compiler_params=_params("arbitrary"),
    )(dh, x, gain, gout)


def _final_loss(x, gain, target, row_lo, row_hi, name):
    l, d = x.shape
    tr = _tile(l, 272, 16)

    def body(x_ref, g_ref, t_ref, gx_ref, gxb_ref, dg_ref, loss_ref):
        i = pl.program_id(0)
        xv = x_ref[...]
        r = lax.rsqrt(jnp.mean(xv * xv, axis=-1, keepdims=True) + RMS_EPS)
        nv = xv * r
        gv = g_ref[...]
        rows = i * tr + lax.broadcasted_iota(jnp.int32, (tr, 1), 0)
        valid = jnp.logical_and(rows >= row_lo, rows < row_hi)
        err = jnp.where(valid, nv * gv - t_ref[...], 0.0)
        dy = err * (1.0 / d)
        dn = dy * gv
        gx = r * (dn - nv * jnp.mean(dn * nv, axis=-1, keepdims=True))
        gx_ref[...] = gx
        gxb_ref[...] = gx.astype(BF16)
        part = jnp.sum(dy * nv, axis=0, keepdims=True)
        lpart = jnp.full((1, 128), 0.5 * jnp.sum(jnp.mean(err * err, axis=-1, keepdims=True)), F32)

        @pl.when(i == 0)
        def _():
            dg_ref[...] = part
            loss_ref[...] = lpart

        @pl.when(i > 0)
        def _():
            dg_ref[...] += part
            loss_ref[...] += lpart

    row = pl.BlockSpec((tr, d), lambda i: (i, 0))
    vec = pl.BlockSpec((1, d), lambda i: (0, 0))
    return pl.pallas_call(
        body, name=name, grid=(l // tr,),
        in_specs=[row, vec, row], out_specs=[row, row, vec, pl.BlockSpec((1, 128), lambda i: (0, 0))],
        out_shape=[jax.ShapeDtypeStruct((l, d), F32), jax.ShapeDtypeStruct((l, d), BF16),
                   jax.ShapeDtypeStruct((1, d), F32), jax.ShapeDtypeStruct((1, 128), F32)],
        compiler_params=_params("arbitrary"),
    )(x, gain, target)


def _log1m_beta(z):
    return -(jnp.maximum(z, 0.0) + jnp.log(1.0 + jnp.exp(-jnp.abs(z))))


def _tri(n, relation):
    r = lax.broadcasted_iota(jnp.int32, (n, n), 0)
    c = lax.broadcasted_iota(jnp.int32, (n, n), 1)
    return jnp.where(relation(r, c), 1.0, 0.0).astype(BF16)


def _running_sum(x, tri, backward, split):
    n = tri.shape[0]
    blocks = [x[:, s * n:(s + 1) * n] for s in range(x.shape[1] // n)]
    inner = [(_dot_split if split else _dot)(b, tri) for b in blocks]
    totals = [jnp.sum(b, axis=1, keepdims=True) for b in blocks]
    order = list(reversed(range(len(blocks)))) if backward else list(range(len(blocks)))
    out, offset = [None] * len(blocks), None
    for s in order:
        out[s] = inner[s] if offset is None else inner[s] + offset
        offset = totals[s] if offset is None else offset + totals[s]
    return (out[0] if len(out) == 1 else jnp.concatenate(out, axis=1)), offset


def _head(hh):
    return slice(hh * HEAD_DIM, (hh + 1) * HEAD_DIM)


def _heads(x, hb):
    return jnp.stack([x[:, _head(hh)] for hh in range(hb)], axis=0)


def _bdot(a, b, ta=False, tb=False):
    dims = (((1 if ta else 2,), (2 if tb else 1,)), ((0,), (0,)))
    return lax.dot_general(a.astype(BF16), b.astype(BF16), dims, preferred_element_type=F32)


def _attn_specs(l, hb, n_heads):
    width = hb * HEAD_DIM
    groups = n_heads // hb

    def tile(section):
        return pl.BlockSpec((ATT_BLOCK, width), lambda h, i: (i, section * groups + h))

    def slab(section):
        return pl.BlockSpec((l, width), lambda h, i: (0, section * groups + h))

    return tile, slab


def _attn_fwd(zin, n_heads, name):
    l = zin.shape[0]
    t = ATT_BLOCK
    hb = next(h for h in (ATT_HEADS_FWD, ATT_HEADS, 1) if n_heads % h == 0)
    scale = HEAD_DIM ** -0.5

    def body(q_ref, k_ref, v_ref, g_ref, o_ref, oa_ref, tot_ref):
        i = pl.program_id(1)
        after = _tri(t, lambda r, c: r > c)
        causal = (lax.broadcasted_iota(jnp.int32, (t, t), 1) < lax.broadcasted_iota(jnp.int32, (t, t), 0))[None]
        q = _heads(q_ref[...], hb).astype(BF16)

        def tile(k0, w, carry, diagonal=False):
            run, acc = carry
            z = _bdot(q, _heads(k_ref[pl.ds(k0, w), :], hb), tb=True) * scale
            lb_all = _log1m_beta(z)
            lb = jnp.where(causal, lb_all, 0.0) if diagonal else lb_all
            between, total = _running_sum(lb.reshape(hb * t, w), after, True, True)
            a = jnp.exp(z + lb_all + between.reshape(hb, t, w) + run)
            if diagonal:
                a = jnp.where(causal, a, 0.0)
            return run + total.reshape(hb, t, 1), acc + _bdot(a, _heads(v_ref[pl.ds(k0, w), :], hb))

        rem = i % 4
        carry = tile(pl.multiple_of(i * t, t), t, (jnp.zeros((hb, t, 1), F32), jnp.zeros((hb, t, HEAD_DIM), F32)), True)
        carry = lax.cond(rem % 2 == 1, lambda c: tile(pl.multiple_of((i - 1) * t, t), t, c), lambda c: c, carry)
        carry = lax.cond(rem >= 2, lambda c: tile(pl.multiple_of((i - rem % 2 - 2) * t, 2 * t), 2 * t, c),
                         lambda c: c, carry)
        run, o = lax.fori_loop(
            0, i // 4, lambda it, c: tile(pl.multiple_of((i - rem - 4 * it - 4) * t, 4 * t), 4 * t, c), carry)
        for hh in range(hb):
            gate = g_ref[:, _head(hh)]
            o_ref[:, _head(hh)] = o[hh]
            oa_ref[:, _head(hh)] = (o[hh] * (gate * _sigmoid(gate))).astype(BF16)
            tot_ref[:, _head(hh)] = jnp.broadcast_to(run[hh], (t, HEAD_DIM))

    tile_spec, slab_spec = _attn_specs(l, hb, n_heads)
    width = n_heads * HEAD_DIM
    return pl.pallas_call(
        body, name=name, grid=(n_heads // hb, l // t),
        in_specs=[tile_spec(0), slab_spec(1), slab_spec(2), tile_spec(3)],
        out_specs=[tile_spec(0), tile_spec(0), tile_spec(0)],
        out_shape=[jax.ShapeDtypeStruct((l, width), F32), jax.ShapeDtypeStruct((l, width), BF16),
                   jax.ShapeDtypeStruct((l, width), F32)],
        compiler_params=_params("parallel", "arbitrary"),
    )(zin, zin, zin, zin)


def _attn_bwd(zin, o, tot, doa, n_heads, name, after=None):
    l = zin.shape[0]
    t = ATT_BLOCK
    nq = l // t
    hb = ATT_HEADS if n_heads % ATT_HEADS == 0 else 1
    scale = HEAD_DIM ** -0.5

    def body(q_ref, k_ref, v_ref, g_ref, o_ref, tot_ref, doa_ref, *rest):
        dq_ref, dk_ref, dv_ref, dg_ref, dk_acc, dv_acc = rest[-6:]
        i = pl.program_id(1)

        @pl.when(i == 0)
        def _():
            dk_acc[...] = jnp.zeros_like(dk_acc)
            dv_acc[...] = jnp.zeros_like(dv_acc)

        upto = _tri(t, lambda r, c: r <= c)
        before = _tri(t, lambda r, c: r < c)
        causal = (lax.broadcasted_iota(jnp.int32, (t, t), 1) < lax.broadcasted_iota(jnp.int32, (t, t), 0))[None]
        gate = g_ref[...]
        sg = _sigmoid(gate)
        doav = doa_ref[...]
        dg_ref[...] = (doav * o_ref[...] * (sg * (1.0 + gate * (1.0 - sg)))).astype(BF16)
        do = _heads(doav * (gate * sg), hb).astype(BF16)
        q = _heads(q_ref[...], hb).astype(BF16)
        total = _heads(tot_ref[...], hb)[:, :, 0:1]

        def tile(k0, w, carry, diagonal=False):
            run, pre, dq = carry
            kb = _heads(k_ref[pl.ds(k0, w), :], hb).astype(BF16)
            z = _bdot(q, kb, tb=True) * scale
            lb_all = _log1m_beta(z)
            lb = jnp.where(causal, lb_all, 0.0) if diagonal else lb_all
            beta = jnp.exp(z + lb_all)
            sofar, lb_total = _running_sum(lb.reshape(hb * t, w), upto, False, True)
            a = beta * jnp.exp(total - run - sofar.reshape(hb, t, w))
            if diagonal:
                a = jnp.where(causal, a, 0.0)
            e = a * _bdot(do, _heads(v_ref[pl.ds(k0, w), :], hb), tb=True)
            earlier, e_total = _running_sum(e.reshape(hb * t, w), before, False, False)
            dz = (e * (1.0 - beta) - beta * (pre + earlier.reshape(hb, t, w))) * scale
            if diagonal:
                dz = jnp.where(causal, dz, 0.0)
            dv = _bdot(a, do, ta=True)
            dk = _bdot(dz, q, ta=True)
            for hh in range(hb):
                dv_acc[pl.ds(k0, w), _head(hh)] += dv[hh]
                dk_acc[pl.ds(k0, w), _head(hh)] += dk[hh]
            return run + lb_total.reshape(hb, t, 1), pre + e_total.reshape(hb, t, 1), dq + _bdot(dz, kb)

        rem = i % 4
        zero = jnp.zeros((hb, t, 1), F32)
        carry = lax.fori_loop(0, i // 4, lambda j, c: tile(pl.multiple_of(j * 4 * t, 4 * t), 4 * t, c),
                              (zero, zero, jnp.zeros((hb, t, HEAD_DIM), F32)))
        carry = lax.cond(rem >= 2, lambda c: tile(pl.multiple_of((i - rem) * t, 2 * t), 2 * t, c), lambda c: c, carry)
        carry = lax.cond(rem % 2 == 1, lambda c: tile(pl.multiple_of((i - 1) * t, t), t, c), lambda c: c, carry)
        _, _, dq = tile(pl.multiple_of(i * t, t), t, carry, True)
        for hh in range(hb):
            dq_ref[:, _head(hh)] = dq[hh].astype(BF16)

        @pl.when(i == nq - 1)
        def _():
            dk_ref[...] = dk_acc[...].astype(BF16)
            dv_ref[...] = dv_acc[...].astype(BF16)

    tile_spec, slab_spec = _attn_specs(l, hb, n_heads)
    out = jax.ShapeDtypeStruct((l, n_heads * HEAD_DIM), BF16)
    return pl.pallas_call(
        body, name=name, grid=(n_heads // hb, nq),
        in_specs=[tile_spec(0), slab_spec(1), slab_spec(2), tile_spec(3), tile_spec(0), tile_spec(0), tile_spec(0)]
        + _after_specs(after),
        out_specs=[tile_spec(0), slab_spec(0), slab_spec(0), tile_spec(0)],
        out_shape=[out, out, out, out],
        scratch_shapes=[pltpu.VMEM((l, hb * HEAD_DIM), F32), pltpu.VMEM((l, hb * HEAD_DIM), F32)],
        compiler_params=_params("parallel", "arbitrary"),
    )(zin, zin, zin, zin, o, tot, doa, *_after_operands(after))


def _shift_rows(x, k, down):
    n = x.shape[0]
    rows = lax.broadcasted_iota(jnp.int32, x.shape, 0)
    if down:
        return jnp.where(rows >= k, pltpu.roll(x, k, 0), 0.0)
    return jnp.where(rows < n - k, pltpu.roll(x, n - k, 0), 0.0)


def _window_sum(x, g, down):
    result = x
    total = x
    for step, k in enumerate((1, 2, 4, 8)):
        total = total + _shift_rows(total, k, down)
        result = jnp.where(g >= step, total, result)
    return result


def _pooled(u, g):
    rows = lax.broadcasted_iota(jnp.int32, (u.shape[0], 1), 0)
    window = jnp.left_shift(2, g)
    cnt = jnp.minimum(rows + 1, window).astype(F32)
    return _window_sum(u, g, True) / cnt - u, cnt


def _pool_fwd(zin, pool_w, pool_scale, u_off, name, after=None):
    l = zin.shape[0]
    n_groups, gd, _ = pool_w.shape

    def body(u_ref, g_ref, w_ref, s_ref, *rest):
        o_ref = rest[-1]
        g = pl.program_id(0)
        pooled, _ = _pooled(u_ref[...], g)
        mixed = _dot(pooled, w_ref[...])
        gate = g_ref[...]
        o_ref[...] = (mixed * s_ref[...] * (gate * _sigmoid(gate))).astype(BF16)

    return pl.pallas_call(
        body, name=name, grid=(n_groups,),
        in_specs=[pl.BlockSpec((l, gd), lambda g: (0, u_off + g)),
                  pl.BlockSpec((l, gd), lambda g: (0, u_off + n_groups + g)),
                  pl.BlockSpec((None, gd, gd), lambda g: (g, 0, 0)),
                  pl.BlockSpec((1, gd), lambda g: (0, g))] + _after_specs(after),
        out_specs=pl.BlockSpec((l, gd), lambda g: (0, g)),
        out_shape=jax.ShapeDtypeStruct((l, n_groups * gd), BF16),
        compiler_params=_params("parallel"),
    )(zin, zin, pool_w, pool_scale, *_after_operands(after))


def _pool_bwd(zin, dop, pool_w, pool_scale, u_off, name):
    l = zin.shape[0]
    n_groups, gd, _ = pool_w.shape

    def body(u_ref, g_ref, w_ref, s_ref, d_ref, du_ref, dg_ref, dw_ref, ds_ref):
        g = pl.program_id(0)
        pooled, cnt = _pooled(u_ref[...], g)
        w = w_ref[...]
        mixed = _dot(pooled, w)
        gate = g_ref[...]
        sg = _sigmoid(gate)
        silu = gate * sg
        dop_v = d_ref[...]
        sc = s_ref[...]
        ds_ref[...] = jnp.sum(dop_v * mixed * silu, axis=0, keepdims=True)
        dg_ref[...] = (dop_v * mixed * sc * (sg * (1.0 + gate * (1.0 - sg)))).astype(BF16)
        dmixed = dop_v * sc * silu
        dw_ref[...] = _dot(pooled, dmixed, ta=True).astype(BF16)
        dpooled = _dot(dmixed, w, tb=True)
        du_ref[...] = (_window_sum(dpooled / cnt, g, False) - dpooled).astype(BF16)

    slab = pl.BlockSpec((l, gd), lambda g: (0, g))
    return pl.pallas_call(
        body, name=name, grid=(n_groups,),
        in_specs=[pl.BlockSpec((l, gd), lambda g: (0, u_off + g)),
                  pl.BlockSpec((l, gd), lambda g: (0, u_off + n_groups + g)),
                  pl.BlockSpec((None, gd, gd), lambda g: (g, 0, 0)),
                  pl.BlockSpec((1, gd), lambda g: (0, g)), slab],
        out_specs=[slab, slab, pl.BlockSpec((None, gd, gd), lambda g: (g, 0, 0)),
                   pl.BlockSpec((1, gd), lambda g: (0, g))],
        out_shape=[jax.ShapeDtypeStruct((l, n_groups * gd), BF16), jax.ShapeDtypeStruct((l, n_groups * gd), BF16),
                   jax.ShapeDtypeStruct(pool_w.shape, BF16), jax.ShapeDtypeStruct((1, n_groups * gd), F32)],
        compiler_params=_params("parallel"),
    )(zin, zin, pool_w, pool_scale, dop)


def _merge_fwd(oa, op, w_au, w_pu, zin, name):
    l, wa = oa.shape
    wp = op.shape[1]
    d = w_au.shape[1]
    tm = _tile(l, 1088, 16)
    tn = _tile(d, 512, 128)
    ma_off = (zin.shape[1] - 2 * d) // tn

    def body(oa_ref, op_ref, wa_ref, wp_ref, ma_ref, mp_ref, ya_ref, yp_ref, mg_ref):
        ya = _dot(oa_ref[...], wa_ref[...])
        yp = _dot(op_ref[...], wp_ref[...])
        ya_ref[...] = ya.astype(BF16)
        yp_ref[...] = yp.astype(BF16)
        mg_ref[...] = (_sigmoid(ma_ref[...]) * ya + _sigmoid(mp_ref[...]) * yp).astype(BF16)

    tile = pl.BlockSpec((tm, tn), lambda i, j: (i, j))
    return pl.pallas_call(
        body, name=name, grid=(l // tm, d // tn),
        in_specs=[pl.BlockSpec((tm, wa), lambda i, j: (i, 0)), pl.BlockSpec((tm, wp), lambda i, j: (i, 0)),
                  pl.BlockSpec((wa, tn), lambda i, j: (0, j)), pl.BlockSpec((wp, tn), lambda i, j: (0, j)),
                  pl.BlockSpec((tm, tn), lambda i, j: (i, ma_off + j)),
                  pl.BlockSpec((tm, tn), lambda i, j: (i, ma_off + d // tn + j))],
        out_specs=[tile, tile, tile],
        out_shape=[jax.ShapeDtypeStruct((l, d), BF16)] * 3,
        compiler_params=_params("parallel", "parallel"),
    )(oa, op, w_au, w_pu, zin, zin)


def _merge_bwd(gout, w_out, zin, ya, yp, name):
    l, d = gout.shape
    tm = _tile(l, 1088, 16)
    tn = _tile(d, 512, 128)
    ma_off = (zin.shape[1] - 2 * d) // tn

    def body(g_ref, w_ref, ma_ref, mp_ref, ya_ref, yp_ref, dya_ref, dyp_ref, dma_ref, dmp_ref):
        dm = _dot(g_ref[...], w_ref[...], tb=True)
        sa = _sigmoid(ma_ref[...])
        sp = _sigmoid(mp_ref[...])
        dya_ref[...] = (dm * sa).astype(BF16)
        dyp_ref[...] = (dm * sp).astype(BF16)
        dma_ref[...] = (dm * ya_ref[...].astype(F32) * (sa * (1.0 - sa))).astype(BF16)
        dmp_ref[...] = (dm * yp_ref[...].astype(F32) * (sp * (1.0 - sp))).astype(BF16)

    tile = pl.BlockSpec((tm, tn), lambda i, j: (i, j))
    out = jax.ShapeDtypeStruct((l, d), BF16)
    return pl.pallas_call(
        body, name=name, grid=(l // tm, d // tn),
        in_specs=[pl.BlockSpec((tm, d), lambda i, j: (i, 0)), pl.BlockSpec((tn, d), lambda i, j: (j, 0)),
                  pl.BlockSpec((tm, tn), lambda i, j: (i, ma_off + j)),
                  pl.BlockSpec((tm, tn), lambda i, j: (i, ma_off + d // tn + j)), tile, tile],
        out_specs=[tile, tile, tile, tile],
        out_shape=[out, out, out, out],
        compiler_params=_params("parallel", "parallel"),
    )(gout, w_out, zin, zin, ya, yp)


def _adamw(parts, w, m, v, name, first=0, into=None):
    n_arrays, n_parts, r, c = parts.shape
    tr = _tile(r, max(8, (128 * 1024) // c // 8 * 8), 8)
    bias1 = 1.0 - ADAM_B1 ** ADAM_STEP
    bias2 = 1.0 - ADAM_B2 ** ADAM_STEP

    def body(p_ref, w_ref, m_ref, v_ref, *rest):
        g_ref, d_ref, nm_ref, nv_ref = rest[-4:]
        g = p_ref[0].astype(F32)
        for j in range(1, n_parts):
            g = g + p_ref[j].astype(F32)
        nm = ADAM_B1 * m_ref[...] + (1.0 - ADAM_B1) * g
        nv = ADAM_B2 * v_ref[...] + (1.0 - ADAM_B2) * (g * g)
        g_ref[...] = g
        nm_ref[...] = nm
        nv_ref[...] = nv
        d_ref[...] = -ADAM_LR * ((nm / bias1) / (jnp.sqrt(nv / bias2) + ADAM_EPS) + ADAM_WD * w_ref[...])

    tile = pl.BlockSpec((None, tr, c), lambda a, i: (a + first, i, 0))
    out = jax.ShapeDtypeStruct(w.shape, F32)
    kept = [] if into is None else list(into)
    return pl.pallas_call(
        body, name=name, grid=(n_arrays, r // tr),
        in_specs=[pl.BlockSpec((None, n_parts, tr, c), lambda a, i: (a, 0, i, 0)), tile, tile, tile]
        + [pl.BlockSpec(memory_space=pl.ANY)] * len(kept),
        out_specs=[tile, tile, tile, tile], out_shape=[out, out, out, out],
        input_output_aliases={4 + k: k for k in range(len(kept))},
        compiler_params=_params("parallel", "parallel"),
    )(parts, w, m, v, *kept)


def _position():
    return lax.axis_index("x"), lax.axis_index("y"), lax.axis_index("c")


def _block_of(ref, axis, size, index):
    idx = [slice(None)] * len(ref.shape)
    idx[axis] = pl.ds(index * size, size)
    return ref.at[tuple(idx)]


HBM_SPEC = pl.BlockSpec(memory_space=pltpu.HBM)
SEM_SPEC = pl.BlockSpec(memory_space=pltpu.SEMAPHORE)
SIDE_EFFECT = pltpu.CompilerParams(has_side_effects=pltpu.SideEffectType.DATAFLOW_SIDE_EFFECTING)


def _split_start(make_copies, n_copies, buffers, name):
    n = len(buffers)

    def body(*refs):
        send_sems, recv_sems = refs[n], refs[n + 1]
        for cp in make_copies(refs[:n], send_sems, recv_sems):
            cp.start()
        refs[-1][...] = jnp.zeros_like(refs[-1])

    sems = pltpu.SemaphoreType.DMA((n_copies,))
    return pl.pallas_call(
        body, name=name, in_specs=[HBM_SPEC] * n,
        out_shape=(sems, sems, *[pltpu.HBM(b.shape, b.dtype) for b in buffers], jax.ShapeDtypeStruct((8, 128), F32)),
        out_specs=(SEM_SPEC, SEM_SPEC, *[HBM_SPEC] * n, pl.BlockSpec(memory_space=pltpu.VMEM)),
        input_output_aliases={i: 2 + i for i in range(n)}, compiler_params=SIDE_EFFECT,
    )(*[pltpu.with_memory_space_constraint(b, pltpu.HBM) for b in buffers])


def _split_wait(make_copies, started, after, name):
    send_sems, recv_sems, *buffers = started[:-1]
    n = len(buffers)

    def body(*refs):
        copies = make_copies(refs[:n], refs[n], refs[n + 1])
        for cp in copies:
            cp.wait_send()
        for cp in copies:
            cp.wait_recv()

    return pl.pallas_call(
        body, name=name, in_specs=[HBM_SPEC] * n + [SEM_SPEC, SEM_SPEC, pl.BlockSpec(memory_space=pl.ANY)],
        out_shape=[pltpu.HBM(b.shape, b.dtype) for b in buffers], out_specs=[HBM_SPEC] * n,
        input_output_aliases={i: i for i in range(n)}, compiler_params=SIDE_EFFECT,
    )(*buffers, send_sems, recv_sems, after)


def _gather_copies(axes, sizes, level):
    def make(fulls, send_sems, recv_sems):
        x, y, c = _position()
        chips = [(1 - x, y), (x, 1 - y), (1 - x, 1 - y)]
        copies = []
        for a, full in enumerate(fulls):
            def copy(k, block, to, full=full, a=a):
                rows = _block_of(full, axes[a], sizes[a], 4 * block[0] + 2 * block[1] + block[2])
                idx = a * (4 if level == 1 else 3) + k
                return pltpu.make_async_remote_copy(src_ref=rows, dst_ref=rows, send_sem=send_sems.at[idx],
                                                    recv_sem=recv_sems.at[idx], device_id=to, device_id_type=MESH)
            if level == 1:
                copies.append(copy(0, (x, y, c), (x, y, 1 - c)))
                copies += [copy(1 + j, (x, y, c), (*chip, c)) for j, chip in enumerate(chips)]
            else:
                copies += [copy(j, (*chip, c), (x, y, 1 - c)) for j, chip in enumerate(chips)]
        return copies
    return make


def _exchange_copies(axes, sizes, layer, n_src):
    flips = [(a, b, d) for a in (0, 1) for b in (0, 1) for d in (0, 1)][1:]

    def make(buffers, send_sems, recv_sems):
        x, y, c = _position()
        my_index = 4 * x + 2 * y + c
        copies = []
        for a in range(n_src):
            for k, flip in enumerate(flips):
                px, py, pc = x ^ flip[0], y ^ flip[1], c ^ flip[2]
                copies.append(pltpu.make_async_remote_copy(
                    src_ref=_block_of(buffers[a], axes[a], sizes[a], 4 * px + 2 * py + pc),
                    dst_ref=buffers[n_src + a].at[layer, my_index],
                    send_sem=send_sems.at[a * 7 + k], recv_sem=recv_sems.at[a * 7 + k],
                    device_id=(px, py, pc), device_id_type=MESH))
        return copies
    return make


def _allgather_small(v, name, after=None):
    r, c = v.shape
    flips = [(a, b, d) for a in (0, 1) for b in (0, 1) for d in (0, 1)][1:]

    def body(v_ref, *rest):
        out_ref, send_sems, recv_sems = rest[-3:]
        x, y, c_ = _position()
        my_index = 4 * x + 2 * y + c_
        out_ref[my_index] = v_ref[...]
        sends = []
        for k, flip in enumerate(flips):
            peer = (x ^ flip[0], y ^ flip[1], c_ ^ flip[2])
            cp = pltpu.make_async_remote_copy(
                src_ref=v_ref, dst_ref=out_ref.at[my_index],
                send_sem=send_sems.at[k], recv_sem=recv_sems.at[k], device_id=peer, device_id_type=MESH)
            cp.start()
            sends.append(cp)
        for k, flip in enumerate(flips):
            px, py, pc = x ^ flip[0], y ^ flip[1], c_ ^ flip[2]
            pltpu.make_async_remote_copy(
                src_ref=v_ref, dst_ref=out_ref.at[4 * px + 2 * py + pc],
                send_sem=send_sems.at[k], recv_sem=recv_sems.at[k],
                device_id=(px, py, pc), device_id_type=MESH).wait_recv()
        for cp in sends:
            cp.wait_send()

    return pl.pallas_call(
        body, name=name,
        in_specs=[pl.BlockSpec(memory_space=pltpu.VMEM)] + _after_specs(after),
        out_specs=pl.BlockSpec(memory_space=pltpu.VMEM),
        out_shape=jax.ShapeDtypeStruct((N_DEV, r, c), v.dtype),
        scratch_shapes=[pltpu.SemaphoreType.DMA((7,)), pltpu.SemaphoreType.DMA((7,))],
    )(v, *_after_operands(after))


def kernel(x, meta_tokens, norm_gain, w_in, pool_w, pool_scale, w_attn_up, w_pool_up, w_out, final_gain, loss_target, m_meta_tokens, m_norm_gain, m_w_in, m_pool_w, m_pool_scale, m_w_attn_up, m_w_pool_up, m_w_out, m_final_gain, v_meta_tokens, v_norm_gain, v_w_in, v_pool_w, v_pool_scale, v_w_attn_up, v_w_pool_up, v_w_out, v_final_gain):
    _, seq, d = x.shape
    n_meta = meta_tokens.shape[0]
    depth = w_in.shape[0]
    sb_width = w_attn_up.shape[1]
    pool_width = w_pool_up.shape[1]
    n_heads = sb_width // HEAD_DIM
    n_groups = pool_w.shape[1]
    gd = pool_w.shape[3]
    assert n_groups == len(POOL_WINDOWS) and gd * n_groups == pool_width
    assert w_in.shape[2] * N_DEV == 4 * sb_width + 2 * pool_width + 2 * d
    l_real = n_meta + seq
    l_pad = -(-l_real // ATT_BLOCK) * ATT_BLOCK
    my_index = 4 * lax.axis_index("x") + 2 * lax.axis_index("y") + lax.axis_index("c")

    me = jnp.reshape(my_index, (1,)).astype(jnp.int32)
    g_named = [("w_in", w_in), ("pool_w", pool_w), ("w_attn_up", w_attn_up), ("w_pool_up", w_pool_up), ("w_out", w_out)]
    g_axes = [1, 1, 1, 1, 0]
    g_sizes = [w.shape[1 + ax] for (_, w), ax in zip(g_named, g_axes)]
    level1 = _gather_copies(g_axes, g_sizes, 1)
    level2 = _gather_copies(g_axes, g_sizes, 2)

    def gather_start(i, after=None):
        fulls = [_cast_place(w, i, ax, me, "cast_" + nm, after) for (nm, w), ax in zip(g_named, g_axes)]
        return _split_start(level1, 4 * len(fulls), fulls, "gather1_start_%d" % i)

    def gather_forward(i, started, after):
        arrived = _split_wait(level1, started, after, "gather1_wait_%d" % i)
        return _split_start(level2, 3 * len(arrived), arrived, "gather2_start_%d" % i)

    meta_all = _allgather_small(meta_tokens, "allgather_meta")
    meta_full = jnp.transpose(meta_all, (1, 0, 2)).reshape(n_meta, d)
    first = gather_start(0)
    second = gather_forward(0, first, first[-1])

    pad_rows = l_pad - l_real
    hs = jnp.concatenate([meta_full, x[0], jnp.zeros((pad_rows, d), F32)], axis=0)
    target = jnp.concatenate([jnp.zeros((n_meta, d), F32), loss_target[0], jnp.zeros((pad_rows, d), F32)], axis=0)
    u_off = 4 * sb_width // gd
    saved, weights = [], []
    for i in range(depth):
        h = _rms_fwd(hs, norm_gain[i][None], "rms_fwd")
        weights.append(_split_wait(level2, second, h, "gather2_wait_%d" % i))
        wi, pw, wau, wpu, wo = weights[i]
        more = i + 1 < depth
        first = gather_start(i + 1, wi) if more else None
        zin = _matmul(h, wi, after=first[-1] if more else None, name="mm_zin")
        o, oa, tot = _attn_fwd(zin, n_heads, "attn_fwd")
        op = _pool_fwd(zin, pw, pool_scale[i][None], u_off, "pool_fwd")
        ya, yp, merged = _merge_fwd(oa, op, wau, wpu, zin, "merge_fwd")
        second = gather_forward(i + 1, first, merged) if more else None
        saved.append((hs, h, zin, o, tot, oa, op, ya, yp, merged))
        hs = _matmul(merged, wo, add=hs, after=second[-1] if more else None, name="mm_out")
    g, gb, d_final_gain, loss_part = _final_loss(hs, final_gain[None], target, n_meta, l_real, "final_loss")
    loss = lax.psum(loss_part[0, 0], ("x", "y", "c"))

    d_norm_gain = [None] * depth
    d_pool_scale = [None] * depth
    axes_a, axes_b = [1, 1, 0], [1, 1]
    blocks_a, blocks_b = [w_attn_up[0], w_pool_up[0], w_out[0]], [w_in[0], pool_w[0]]
    sizes_a = [b.shape[ax] for b, ax in zip(blocks_a, axes_a)]
    sizes_b = [b.shape[ax] for b, ax in zip(blocks_b, axes_b)]
    land_a = [lax.empty((depth, N_DEV, *b.shape), BF16) for b in blocks_a]
    land_b = [lax.empty((depth - 1, N_DEV, *b.shape), BF16) for b in blocks_b]
    land_b0 = [lax.empty((1, N_DEV, *b.shape), BF16) for b in blocks_b]

    def exchange_start(grads, landing, axes, sizes, slot, name):
        landing = [lax.dynamic_update_slice(
            zone, lax.dynamic_slice_in_dim(grad, my_index * size, size, ax)[None, None],
            (slot, my_index) + (0,) * grad.ndim) for zone, grad, ax, size in zip(landing, grads, axes, sizes)]
        copies = _exchange_copies(axes, sizes, slot, len(grads))
        return copies, _split_start(copies, 7 * len(grads), list(grads) + landing, name)

    def exchange_wait(pending, n_src, after, name):
        return _split_wait(pending[0], pending[1], after, name)[n_src:]

    pend_a = pend_b = None
    for i in reversed(range(depth)):
        wi, pw, wau, wpu, wo = weights[i]
        hs_in, h, zin, o, tot, oa, op, ya, yp, merged = saved[i]
        dya, dyp, dma, dmp = _merge_bwd(gb, wo, zin, ya, yp, "merge_bwd")
        dw_out = _matmul(merged, gb, ta=True, out_dtype=BF16, name="mm_dw_out", tm=1024, tk=1088)
        doa = _matmul(dya, wau, tb=True, name="mm_doa")
        dw_au = _matmul(oa, dya, ta=True, out_dtype=BF16, name="mm_dw_au", tm=1024, tk=1088)
        dop = _matmul(dyp, wpu, tb=True, name="mm_dop")
        dw_pu = _matmul(op, dyp, ta=True, out_dtype=BF16, name="mm_dw_pu", tm=1024, tk=1088)
        if pend_a is not None:
            land_a = exchange_wait(pend_a, 3, dw_pu, "exchange_a_wait_%d" % (i + 1))
        pend_a = exchange_start([dw_au, dw_pu, dw_out], land_a, axes_a, sizes_a, i, "exchange_a_start_%d" % i)
        dq, dk, dv, dga = _attn_bwd(zin, o, tot, doa, n_heads, "attn_bwd", after=pend_a[1][-1])
        du, dgp, dpw, dps = _pool_bwd(zin, dop, pw, pool_scale[i][None], u_off, "pool_bwd")
        dzin = [dq, dk, dv, dga, du, dgp, dma, dmp]
        dw_in = _matmul(h, dzin, ta=True, out_dtype=BF16, name="mm_dw_in", tm=1024, tk=1088)
        if pend_b is not None:
            land_b = exchange_wait(pend_b, 2, dw_in, "exchange_b_wait_%d" % (i + 1))
        pend_b = exchange_start([dw_in, dpw], land_b if i > 0 else land_b0, axes_b, sizes_b, max(i - 1, 0),
                                "exchange_b_start_%d" % i)
        dh = _matmul(dzin, wi, tb=True, after=pend_b[1][-1], name="mm_dh")
        g, gb, dng = _rms_bwd(dh, hs_in, norm_gain[i][None], g, "rms_bwd")
        d_norm_gain[i] = dng
        d_pool_scale[i] = dps
    land_a = exchange_wait(pend_a, 3, gb, "exchange_a_wait_0")
    grad_x = g[n_meta:l_real][None]

    def sharded(parts, w, m, v, name, first=0, into=None):
        flat = (depth, -1, w.shape[-1])
        return _adamw(parts.reshape(parts.shape[0], N_DEV, -1, w.shape[-1]), w.reshape(flat), m.reshape(flat),
                      v.reshape(flat), name, first, into)

    out_au = sharded(land_a[0], w_attn_up, m_w_attn_up, v_w_attn_up, "adamw_w_attn_up")
    out_pu = sharded(land_a[1], w_pool_up, m_w_pool_up, v_w_pool_up, "adamw_w_pool_up")
    out_wo = sharded(land_a[2], w_out, m_w_out, v_w_out, "adamw_w_out")
    out_wi = out_pw = None
    if depth > 1:
        out_wi = sharded(land_b[0], w_in, m_w_in, v_w_in, "adamw_w_in", 1)
        out_pw = sharded(land_b[1], pool_w, m_pool_w, v_pool_w, "adamw_pool_w", 1)
    land_b0 = exchange_wait(pend_b, 2, out_wo[0] if out_wi is None else out_wi[0], "exchange_b_wait_0")
    out_wi = sharded(land_b0[0], w_in, m_w_in, v_w_in, "adamw_w_in_0", 0, out_wi)
    out_pw = sharded(land_b0[1], pool_w, m_pool_w, v_pool_w, "adamw_pool_w_0", 0, out_pw)
    out_wi, out_pw, out_au, out_pu, out_wo = [
        [t.reshape(w.shape) for t in res] for res, w in
        zip([out_wi, out_pw, out_au, out_pu, out_wo], [w_in, pool_w, w_attn_up, w_pool_up, w_out])]

    zeros_ps = jnp.zeros((depth, d - pool_width), F32)
    small_rows = [jnp.concatenate(d_norm_gain, axis=0),
                  jnp.concatenate([jnp.concatenate(d_pool_scale, axis=0), zeros_ps], axis=1), d_final_gain]
    n_small = 2 * depth + 1
    small_pad = -(-n_small // 8) * 8
    small = jnp.concatenate(small_rows + [jnp.zeros((small_pad - n_small, d), F32), g[:n_meta]], axis=0)
    small_all = _allgather_small(small, "allgather_small", after=land_b0[0])

    def replicated(rows, width, w, m, v, name):
        parts = lax.slice(small_all, (0, rows[0], 0), (N_DEV, rows[1], width))
        return [t[0] for t in _adamw(parts[None], w[None], m[None], v[None], name)]

    out_ng = replicated((0, depth), d, norm_gain, m_norm_gain, v_norm_gain, "adamw_norm_gain")
    out_ps = replicated((depth, 2 * depth), pool_width, pool_scale, m_pool_scale, v_pool_scale, "adamw_pool_scale")
    out_fg = [t[0] for t in replicated((2 * depth, 2 * depth + 1), d, final_gain[None], m_final_gain[None],
                                       v_final_gain[None], "adamw_final_gain")]
    cols = d // N_DEV
    meta_parts = lax.dynamic_slice(small_all, (0, small_pad, my_index * cols), (N_DEV, n_meta, cols))
    out_meta = [t[0] for t in _adamw(meta_parts[None], meta_tokens[None], m_meta_tokens[None], v_meta_tokens[None],
                                     "adamw_meta")]

    by_weight = [out_meta, out_ng, out_wi, out_pw, out_ps, out_au, out_pu, out_wo, out_fg]
    return (loss, grad_x, *[o[0] for o in by_weight], *[o[1] for o in by_weight],
            *[o[2] for o in by_weight], *[o[3] for o in by_weight])
```

```python
import jax
import jax.numpy as jnp
from jax import lax
from jax.experimental import pallas as pl
from jax.experimental.pallas import tpu as pltpu

F32 = jnp.float32
BF16 = jnp.bfloat16
MESH = pl.DeviceIdType.MESH

N_DEV = 8
HEAD_DIM = 128
ATT_BLOCK = 128
ATT_HEADS = 4
ATT_HEADS_FWD = 8
POOL_WINDOWS = (2, 4, 8, 16)
RMS_EPS = 1e-6
ADAM_LR, ADAM_B1, ADAM_B2, ADAM_EPS, ADAM_WD, ADAM_STEP = 0.001, 0.9, 0.999, 1e-08, 0.01, 10
VMEM_LIMIT_BYTES = 56 * 1024 * 1024


def _tile(n, target, mult):
    if n <= target:
        return n
    best = 0
    for t in range(mult, target + 1, mult):
        if n % t == 0:
            best = t
    assert best > 0, (n, target, mult)
    return best


def _params(*sem):
    return pltpu.CompilerParams(dimension_semantics=sem, vmem_limit_bytes=VMEM_LIMIT_BYTES)


def _sigmoid(x):
    return 1.0 / (1.0 + jnp.exp(-x))


def _dot(a, b, ta=False, tb=False):
    dims = (((0 if ta else 1,), (1 if tb else 0,)), ((), ()))
    return lax.dot_general(a.astype(BF16), b.astype(BF16), dims, preferred_element_type=F32)


def _dot_split(a, b):
    hi = a.astype(BF16)
    lo = (a - hi.astype(F32)).astype(BF16)
    return _dot(hi, b) + _dot(lo, b)


def _cast_place(x, layer, axis, me, name, after=None):
    blk = x.shape[1:]
    full = list(blk)
    full[axis] *= N_DEV
    if len(blk) == 2:
        r, c = blk
        tr = _tile(r, max(16, (512 * 1024) // c // 16 * 16), 16)
        steps = r // tr
        in_spec = pl.BlockSpec((None, tr, c), lambda i, me_ref: (layer, i, 0))
        if axis == 1:
            out_spec = pl.BlockSpec((tr, c), lambda i, me_ref: (i, me_ref[0]))
        else:
            out_spec = pl.BlockSpec((tr, c), lambda i, me_ref: (me_ref[0] * steps + i, 0))
    else:
        assert len(blk) == 3 and axis == 1
        steps = 1
        in_spec = pl.BlockSpec((None, *blk), lambda i, me_ref: (layer, 0, 0, 0))
        out_spec = pl.BlockSpec(blk, lambda i, me_ref: (0, me_ref[0], 0))

    def body(me_ref, x_ref, *rest):
        rest[-1][...] = x_ref[...].astype(BF16)

    return pl.pallas_call(
        body, name=name,
        grid_spec=pltpu.PrefetchScalarGridSpec(num_scalar_prefetch=1, grid=(steps,),
                                               in_specs=[in_spec] + _after_specs(after), out_specs=out_spec),
        out_shape=jax.ShapeDtypeStruct(tuple(full), BF16),
        compiler_params=_params("arbitrary"),
    )(me, x, *_after_operands(after))


def _after_operands(after):
    return [] if after is None else [after]


def _after_specs(after):
    return [] if after is None else [pl.BlockSpec(memory_space=pl.ANY)]


def _matmul(a, b, *, ta=False, tb=False, out_dtype=F32, add=None, after=None, name, tm=1088, tn=1024, tk=2048):
    a_list = list(a) if isinstance(a, (list, tuple)) else None
    b_list = list(b) if isinstance(b, (list, tuple)) else None
    assert not (a_list and ta) and not (b_list and tb) and not (a_list and b_list)
    if a_list:
        m, k = a_list[0].shape[0], sum(p.shape[1] for p in a_list)
    else:
        m, k = (a.shape[1], a.shape[0]) if ta else a.shape
    if b_list:
        n = sum(p.shape[1] for p in b_list)
    else:
        n = b.shape[0] if tb else b.shape[1]
    tm = _tile(m, tm, 128 if ta else 16)
    tn = _tile(n, tn, 128)
    tk = _tile(k, tk, 128 if (not ta or tb) else 16)
    if a_list:
        while any(p.shape[1] % tk for p in a_list):
            tk //= 2
    if b_list:
        while any(p.shape[1] % tn for p in b_list):
            tn //= 2
    nk = k // tk

    def ranges(pieces, tile):
        out, start = [], 0
        for p in pieces:
            out.append((start, start + p.shape[1] // tile))
            start = out[-1][1]
        return out

    if a_list:
        a_ranges = ranges(a_list, tk)
        a_specs = [pl.BlockSpec((tm, tk), lambda i, j, kk, s=s, e=e: (i, jnp.clip(kk - s, 0, e - s - 1)))
                   for s, e in a_ranges]
    else:
        a_specs = [pl.BlockSpec((tk, tm), lambda i, j, kk: (kk, i)) if ta
                   else pl.BlockSpec((tm, tk), lambda i, j, kk: (i, kk))]
    if b_list:
        b_ranges = ranges(b_list, tn)
        b_specs = [pl.BlockSpec((tk, tn), lambda i, j, kk, s=s, e=e: (
            jnp.where(jnp.logical_and(j >= s, j < e), kk, 0), jnp.clip(j - s, 0, e - s - 1))) for s, e in b_ranges]
    else:
        b_specs = [pl.BlockSpec((tn, tk), lambda i, j, kk: (j, kk)) if tb
                   else pl.BlockSpec((tk, tn), lambda i, j, kk: (kk, j))]
    o_spec = pl.BlockSpec((tm, tn), lambda i, j, kk: (i, j))
    has_add = add is not None
    na, nb = len(a_specs), len(b_specs)
    simple = nk == 1 and not a_list and not b_list

    def body(*refs):
        a_refs, b_refs = refs[:na], refs[na:na + nb]
        add_ref = refs[na + nb] if has_add else None
        o_ref, acc_ref = refs[-2], refs[-1]
        j, kk = pl.program_id(1), pl.program_id(2)

        def finish(total):
            if has_add:
                total = total + add_ref[...]
            o_ref[...] = total.astype(out_dtype)

        if simple:
            finish(_dot(a_refs[0][...], b_refs[0][...], ta, tb))
            return

        @pl.when(kk == 0)
        def _():
            acc_ref[...] = jnp.zeros_like(acc_ref)

        if a_list:
            for (s, e), a_ref in zip(a_ranges, a_refs):
                @pl.when(jnp.logical_and(kk >= s, kk < e))
                def _(a_ref=a_ref):
                    acc_ref[...] += _dot(a_ref[...], b_refs[0][...], ta, tb)
        elif b_list:
            for (s, e), b_ref in zip(b_ranges, b_refs):
                @pl.when(jnp.logical_and(j >= s, j < e))
                def _(b_ref=b_ref):
                    acc_ref[...] += _dot(a_refs[0][...], b_ref[...], ta, tb)
        else:
            acc_ref[...] += _dot(a_refs[0][...], b_refs[0][...], ta, tb)

        @pl.when(kk == nk - 1)
        def _():
            finish(acc_ref[...])

    operands = (a_list or [a]) + (b_list or [b]) + ([add] if has_add else []) + _after_operands(after)
    in_specs = a_specs + b_specs + ([o_spec] if has_add else []) + _after_specs(after)
    return pl.pallas_call(
        body, name=name, grid=(m // tm, n // tn, nk),
        in_specs=in_specs, out_specs=o_spec,
        out_shape=jax.ShapeDtypeStruct((m, n), out_dtype),
        scratch_shapes=[pltpu.VMEM((8, 128) if simple else (tm, tn), F32)],
        compiler_params=_params("parallel", "parallel", "arbitrary"),
    )(*operands)


def _rms_fwd(x, gain, name, after=None):
    l, d = x.shape
    tr = _tile(l, 272, 16)

    def body(x_ref, g_ref, *rest):
        o_ref = rest[-1]
        xv = x_ref[...]
        r = lax.rsqrt(jnp.mean(xv * xv, axis=-1, keepdims=True) + RMS_EPS)
        o_ref[...] = (xv * r * g_ref[...]).astype(BF16)

    return pl.pallas_call(
        body, name=name, grid=(l // tr,),
        in_specs=[pl.BlockSpec((tr, d), lambda i: (i, 0)), pl.BlockSpec((1, d), lambda i: (0, 0))] + _after_specs(after),
        out_specs=pl.BlockSpec((tr, d), lambda i: (i, 0)),
        out_shape=jax.ShapeDtypeStruct((l, d), BF16),
        compiler_params=_params("parallel"),
    )(x, gain, *_after_operands(after))


def _rms_bwd(dh, x, gain, gout, name):
    l, d = x.shape
    tr = _tile(l, 272, 16)

    def body(dh_ref, x_ref, g_ref, go_ref, gx_ref, gxb_ref, dg_ref):
        xv = x_ref[...]
        r = lax.rsqrt(jnp.mean(xv * xv, axis=-1, keepdims=True) + RMS_EPS)
        nv = xv * r
        dhv = dh_ref[...]
        dn = dhv * g_ref[...]
        dx = r * (dn - nv * jnp.mean(dn * nv, axis=-1, keepdims=True))
        gx = go_ref[...] + dx
        gx_ref[...] = gx
        gxb_ref[...] = gx.astype(BF16)
        part = jnp.sum(dhv * nv, axis=0, keepdims=True)

        @pl.when(pl.program_id(0) == 0)
        def _():
            dg_ref[...] = part

        @pl.when(pl.program_id(0) > 0)
        def _():
            dg_ref[...] += part

    row = pl.BlockSpec((tr, d), lambda i: (i, 0))
    vec = pl.BlockSpec((1, d), lambda i: (0, 0))
    return pl.pallas_call(
        body, name=name, grid=(l // tr,),
        in_specs=[row, row, vec, row], out_specs=[row, row, vec],
        out_shape=[jax.ShapeDtypeStruct((l, d), F32), jax.ShapeDtypeStruct((l, d), BF16),
                   jax.ShapeDtypeStruct((1, d), F32)],
        compiler_params=_params("arbitrary"),
    )(dh, x, gain, gout)


def _final_loss(x, gain, target, row_lo, row_hi, name):
    l, d = x.shape
    tr = _tile(l, 272, 16)

    def body(x_ref, g_ref, t_ref, gx_ref, gxb_ref, dg_ref, loss_ref):
        i = pl.program_id(0)
        xv = x_ref[...]
        r = lax.rsqrt(jnp.mean(xv * xv, axis=-1, keepdims=True) + RMS_EPS)
        nv = xv * r
        gv = g_ref[...]
        rows = i * tr + lax.broadcasted_iota(jnp.int32, (tr, 1), 0)
        valid = jnp.logical_and(rows >= row_lo, rows < row_hi)
        err = jnp.where(valid, nv * gv - t_ref[...], 0.0)
        dy = err * (1.0 / d)
        dn = dy * gv
        gx = r * (dn - nv * jnp.mean(dn * nv, axis=-1, keepdims=True))
        gx_ref[...] = gx
        gxb_ref[...] = gx.astype(BF16)
        part = jnp.sum(dy * nv, axis=0, keepdims=True)
        lpart = jnp.full((1, 128), 0.5 * jnp.sum(jnp.mean(err * err, axis=-1, keepdims=True)), F32)

        @pl.when(i == 0)
        def _():
            dg_ref[...] = part
            loss_ref[...] = lpart

        @pl.when(i > 0)
        def _():
            dg_ref[...] += part
            loss_ref[...] += lpart

    row = pl.BlockSpec((tr, d), lambda i: (i, 0))
    vec = pl.BlockSpec((1, d), lambda i: (0, 0))
    return pl.pallas_call(
        body, name=name, grid=(l // tr,),
        in_specs=[row, vec, row], out_specs=[row, row, vec, pl.BlockSpec((1, 128), lambda i: (0, 0))],
        out_shape=[jax.ShapeDtypeStruct((l, d), F32), jax.ShapeDtypeStruct((l, d), BF16),
                   jax.ShapeDtypeStruct((1, d), F32), jax.ShapeDtypeStruct((1, 128), F32)],
        compiler_params=_params("arbitrary"),
    )(x, gain, target)


def _log1m_beta(z):
    return -(jnp.maximum(z, 0.0) + jnp.log(1.0 + jnp.exp(-jnp.abs(z))))


def _tri(n, relation):
    r = lax.broadcasted_iota(jnp.int32, (n, n), 0)
    c = lax.broadcasted_iota(jnp.int32, (n, n), 1)
    return jnp.where(relation(r, c), 1.0, 0.0).astype(BF16)


def _running_sum(x, tri, backward, split):
    n = tri.shape[0]
    blocks = [x[:, s * n:(s + 1) * n] for s in range(x.shape[1] // n)]
    inner = [(_dot_split if split else _dot)(b, tri) for b in blocks]
    totals = [jnp.sum(b, axis=1, keepdims=True) for b in blocks]
    order = list(reversed(range(len(blocks)))) if backward else list(range(len(blocks)))
    out, offset = [None] * len(blocks), None
    for s in order:
        out[s] = inner[s] if offset is None else inner[s] + offset
        offset = totals[s] if offset is None else offset + totals[s]
    return (out[0] if len(out) == 1 else jnp.concatenate(out, axis=1)), offset


def _head(hh):
    return slice(hh * HEAD_DIM, (hh + 1) * HEAD_DIM)


def _heads(x, hb):
    return jnp.stack([x[:, _head(hh)] for hh in range(hb)], axis=0)


def _bdot(a, b, ta=False, tb=False):
    dims = (((1 if ta else 2,), (2 if tb else 1,)), ((0,), (0,)))
    return lax.dot_general(a.astype(BF16), b.astype(BF16), dims, preferred_element_type=F32)


def _attn_specs(l, hb, n_heads):
    width = hb * HEAD_DIM
    groups = n_heads // hb

    def tile(section):
        return pl.BlockSpec((ATT_BLOCK, width), lambda h, i: (i, section * groups + h))

    def slab(section):
        return pl.BlockSpec((l, width), lambda h, i: (0, section * groups + h))

    return tile, slab


def _attn_fwd(zin, n_heads, name):
    l = zin.shape[0]
    t = ATT_BLOCK
    hb = next(h for h in (ATT_HEADS_FWD, ATT_HEADS, 1) if n_heads % h == 0)
    scale = HEAD_DIM ** -0.5

    def body(q_ref, k_ref, v_ref, g_ref, o_ref, oa_ref, tot_ref):
        i = pl.program_id(1)
        after = _tri(t, lambda r, c: r > c)
        causal = (lax.broadcasted_iota(jnp.int32, (t, t), 1) < lax.broadcasted_iota(jnp.int32, (t, t), 0))[None]
        q = _heads(q_ref[...], hb).astype(BF16)

        def tile(k0, w, carry, diagonal=False):
            run, acc = carry
            z = _bdot(q, _heads(k_ref[pl.ds(k0, w), :], hb), tb=True) * scale
            lb_all = _log1m_beta(z)
            lb = jnp.where(causal, lb_all, 0.0) if diagonal else lb_all
            between, total = _running_sum(lb.reshape(hb * t, w), after, True, True)
            a = jnp.exp(z + lb_all + between.reshape(hb, t, w) + run)
            if diagonal:
                a = jnp.where(causal, a, 0.0)
            return run + total.reshape(hb, t, 1), acc + _bdot(a, _heads(v_ref[pl.ds(k0, w), :], hb))

        rem = i % 4
        carry = tile(pl.multiple_of(i * t, t), t, (jnp.zeros((hb, t, 1), F32), jnp.zeros((hb, t, HEAD_DIM), F32)), True)
        carry = lax.cond(rem % 2 == 1, lambda c: tile(pl.multiple_of((i - 1) * t, t), t, c), lambda c: c, carry)
        carry = lax.cond(rem >= 2, lambda c: tile(pl.multiple_of((i - rem % 2 - 2) * t, 2 * t), 2 * t, c),
                         lambda c: c, carry)
        run, o = lax.fori_loop(
            0, i // 4, lambda it, c: tile(pl.multiple_of((i - rem - 4 * it - 4) * t, 4 * t), 4 * t, c), carry)
        for hh in range(hb):
            gate = g_ref[:, _head(hh)]
            o_ref[:, _head(hh)] = o[hh]
            oa_ref[:, _head(hh)] = (o[hh] * (gate * _sigmoid(gate))).astype(BF16)
            tot_ref[:, _head(hh)] = jnp.broadcast_to(run[hh], (t, HEAD_DIM))

    tile_spec, slab_spec = _attn_specs(l, hb, n_heads)
    width = n_heads * HEAD_DIM
    return pl.pallas_call(
        body, name=name, grid=(n_heads // hb, l // t),
        in_specs=[tile_spec(0), slab_spec(1), slab_spec(2), tile_spec(3)],
        out_specs=[tile_spec(0), tile_spec(0), tile_spec(0)],
        out_shape=[jax.ShapeDtypeStruct((l, width), F32), jax.ShapeDtypeStruct((l, width), BF16),
                   jax.ShapeDtypeStruct((l, width), F32)],
        compiler_params=_params("parallel", "arbitrary"),
    )(zin, zin, zin, zin)


def _attn_bwd(zin, o, tot, doa, n_heads, name, after=None):
    l = zin.shape[0]
    t = ATT_BLOCK
    nq = l // t
    hb = ATT_HEADS if n_heads % ATT_HEADS == 0 else 1
    scale = HEAD_DIM ** -0.5

    def body(q_ref, k_ref, v_ref, g_ref, o_ref, tot_ref, doa_ref, *rest):
        dq_ref, dk_ref, dv_ref, dg_ref, dk_acc, dv_acc = rest[-6:]
        i = pl.program_id(1)

        @pl.when(i == 0)
        def _():
            dk_acc[...] = jnp.zeros_like(dk_acc)
            dv_acc[...] = jnp.zeros_like(dv_acc)

        upto = _tri(t, lambda r, c: r <= c)
        before = _tri(t, lambda r, c: r < c)
        causal = (lax.broadcasted_iota(jnp.int32, (t, t), 1) < lax.broadcasted_iota(jnp.int32, (t, t), 0))[None]
        gate = g_ref[...]
        sg = _sigmoid(gate)
        doav = doa_ref[...]
        dg_ref[...] = (doav * o_ref[...] * (sg * (1.0 + gate * (1.0 - sg)))).astype(BF16)
        do = _heads(doav * (gate * sg), hb).astype(BF16)
        q = _heads(q_ref[...], hb).astype(BF16)
        total = _heads(tot_ref[...], hb)[:, :, 0:1]

        def tile(k0, w, carry, diagonal=False):
            run, pre, dq = carry
            kb = _heads(k_ref[pl.ds(k0, w), :], hb).astype(BF16)
            z = _bdot(q, kb, tb=True) * scale
            lb_all = _log1m_beta(z)
            lb = jnp.where(causal, lb_all, 0.0) if diagonal else lb_all
            beta = jnp.exp(z + lb_all)
            sofar, lb_total = _running_sum(lb.reshape(hb * t, w), upto, False, True)
            a = beta * jnp.exp(total - run - sofar.reshape(hb, t, w))
            if diagonal:
                a = jnp.where(causal, a, 0.0)
            e = a * _bdot(do, _heads(v_ref[pl.ds(k0, w), :], hb), tb=True)
            earlier, e_total = _running_sum(e.reshape(hb * t, w), before, False, False)
            dz = (e * (1.0 - beta) - beta * (pre + earlier.reshape(hb, t, w))) * scale
            if diagonal:
                dz = jnp.where(causal, dz, 0.0)
            dv = _bdot(a, do, ta=True)
            dk = _bdot(dz, q, ta=True)
            for hh in range(hb):
                dv_acc[pl.ds(k0, w), _head(hh)] += dv[hh]
                dk_acc[pl.ds(k0, w), _head(hh)] += dk[hh]
            return run + lb_total.reshape(hb, t, 1), pre + e_total.reshape(hb, t, 1), dq + _bdot(dz, kb)

        rem = i % 4
        zero = jnp.zeros((hb, t, 1), F32)
        carry = lax.fori_loop(0, i // 4, lambda j, c: tile(pl.multiple_of(j * 4 * t, 4 * t), 4 * t, c),
                              (zero, zero, jnp.zeros((hb, t, HEAD_DIM), F32)))
        carry = lax.cond(rem >= 2, lambda c: tile(pl.multiple_of((i - rem) * t, 2 * t), 2 * t, c), lambda c: c, carry)
        carry = lax.cond(rem % 2 == 1, lambda c: tile(pl.multiple_of((i - 1) * t, t), t, c), lambda c: c, carry)
        _, _, dq = tile(pl.multiple_of(i * t, t), t, carry, True)
        for hh in range(hb):
            dq_ref[:, _head(hh)] = dq[hh].astype(BF16)

        @pl.when(i == nq - 1)
        def _():
            dk_ref[...] = dk_acc[...].astype(BF16)
            dv_ref[...] = dv_acc[...].astype(BF16)

    tile_spec, slab_spec = _attn_specs(l, hb, n_heads)
    out = jax.ShapeDtypeStruct((l, n_heads * HEAD_DIM), BF16)
    return pl.pallas_call(
        body, name=name, grid=(n_heads // hb, nq),
        in_specs=[tile_spec(0), slab_spec(1), slab_spec(2), tile_spec(3), tile_spec(0), tile_spec(0), tile_spec(0)]
        + _after_specs(after),
        out_specs=[tile_spec(0), slab_spec(0), slab_spec(0), tile_spec(0)],
        out_shape=[out, out, out, out],
        scratch_shapes=[pltpu.VMEM((l, hb * HEAD_DIM), F32), pltpu.VMEM((l, hb * HEAD_DIM), F32)],
        compiler_params=_params("parallel", "arbitrary"),
    )(zin, zin, zin, zin, o, tot, doa, *_after_operands(after))


def _shift_rows(x, k, down):
    n = x.shape[0]
    rows = lax.broadcasted_iota(jnp.int32, x.shape, 0)
    if down:
        return jnp.where(rows >= k, pltpu.roll(x, k, 0), 0.0)
    return jnp.where(rows < n - k, pltpu.roll(x, n - k, 0), 0.0)


def _window_sum(x, g, down):
    result = x
    total = x
    for step, k in enumerate((1, 2, 4, 8)):
        total = total + _shift_rows(total, k, down)
        result = jnp.where(g >= step, total, result)
    return result


def _pooled(u, g):
    rows = lax.broadcasted_iota(jnp.int32, (u.shape[0], 1), 0)
    window = jnp.left_shift(2, g)
    cnt = jnp.minimum(rows + 1, window).astype(F32)
    return _window_sum(u, g, True) / cnt - u, cnt


def _pool_fwd(zin, pool_w, pool_scale, u_off, name, after=None):
    l = zin.shape[0]
    n_groups, gd, _ = pool_w.shape

    def body(u_ref, g_ref, w_ref, s_ref, *rest):
        o_ref = rest[-1]
        g = pl.program_id(0)
        pooled, _ = _pooled(u_ref[...], g)
        mixed = _dot(pooled, w_ref[...])
        gate = g_ref[...]
        o_ref[...] = (mixed * s_ref[...] * (gate * _sigmoid(gate))).astype(BF16)

    return pl.pallas_call(
        body, name=name, grid=(n_groups,),
        in_specs=[pl.BlockSpec((l, gd), lambda g: (0, u_off + g)),
                  pl.BlockSpec((l, gd), lambda g: (0, u_off + n_groups + g)),
                  pl.BlockSpec((None, gd, gd), lambda g: (g, 0, 0)),
                  pl.BlockSpec((1, gd), lambda g: (0, g))] + _after_specs(after),
        out_specs=pl.BlockSpec((l, gd), lambda g: (0, g)),
        out_shape=jax.ShapeDtypeStruct((l, n_groups * gd), BF16),
        compiler_params=_params("parallel"),
    )(zin, zin, pool_w, pool_scale, *_after_operands(after))


def _pool_bwd(zin, dop, pool_w, pool_scale, u_off, name):
    l = zin.shape[0]
    n_groups, gd, _ = pool_w.shape

    def body(u_ref, g_ref, w_ref, s_ref, d_ref, du_ref, dg_ref, dw_ref, ds_ref):
        g = pl.program_id(0)
        pooled, cnt = _pooled(u_ref[...], g)
        w = w_ref[...]
        mixed = _dot(pooled, w)
        gate = g_ref[...]
        sg = _sigmoid(gate)
        silu = gate * sg
        dop_v = d_ref[...]
        sc = s_ref[...]
        ds_ref[...] = jnp.sum(dop_v * mixed * silu, axis=0, keepdims=True)
        dg_ref[...] = (dop_v * mixed * sc * (sg * (1.0 + gate * (1.0 - sg)))).astype(BF16)
        dmixed = dop_v * sc * silu
        dw_ref[...] = _dot(pooled, dmixed, ta=True).astype(BF16)
        dpooled = _dot(dmixed, w, tb=True)
        du_ref[...] = (_window_sum(dpooled / cnt, g, False) - dpooled).astype(BF16)

    slab = pl.BlockSpec((l, gd), lambda g: (0, g))
    return pl.pallas_call(
        body, name=name, grid=(n_groups,),
        in_specs=[pl.BlockSpec((l, gd), lambda g: (0, u_off + g)),
                  pl.BlockSpec((l, gd), lambda g: (0, u_off + n_groups + g)),
                  pl.BlockSpec((None, gd, gd), lambda g: (g, 0, 0)),
                  pl.BlockSpec((1, gd), lambda g: (0, g)), slab],
        out_specs=[slab, slab, pl.BlockSpec((None, gd, gd), lambda g: (g, 0, 0)),
                   pl.BlockSpec((1, gd), lambda g: (0, g))],
        out_shape=[jax.ShapeDtypeStruct((l, n_groups * gd), BF16), jax.ShapeDtypeStruct((l, n_groups * gd), BF16),
                   jax.ShapeDtypeStruct(pool_w.shape, BF16), jax.ShapeDtypeStruct((1, n_groups * gd), F32)],
        compiler_params=_params("parallel"),
    )(zin, zin, pool_w, pool_scale, dop)


def _merge_fwd(oa, op, w_au, w_pu, zin, name):
    l, wa = oa.shape
    wp = op.shape[1]
    d = w_au.shape[1]
    tm = _tile(l, 1088, 16)
    tn = _tile(d, 512, 128)
    ma_off = (zin.shape[1] - 2 * d) // tn

    def body(oa_ref, op_ref, wa_ref, wp_ref, ma_ref, mp_ref, ya_ref, yp_ref, mg_ref):
        ya = _dot(oa_ref[...], wa_ref[...])
        yp = _dot(op_ref[...], wp_ref[...])
        ya_ref[...] = ya.astype(BF16)
        yp_ref[...] = yp.astype(BF16)
        mg_ref[...] = (_sigmoid(ma_ref[...]) * ya + _sigmoid(mp_ref[...]) * yp).astype(BF16)

    tile = pl.BlockSpec((tm, tn), lambda i, j: (i, j))
    return pl.pallas_call(
        body, name=name, grid=(l // tm, d // tn),
        in_specs=[pl.BlockSpec((tm, wa), lambda i, j: (i, 0)), pl.BlockSpec((tm, wp), lambda i, j: (i, 0)),
                  pl.BlockSpec((wa, tn), lambda i, j: (0, j)), pl.BlockSpec((wp, tn), lambda i, j: (0, j)),
                  pl.BlockSpec((tm, tn), lambda i, j: (i, ma_off + j)),
                  pl.BlockSpec((tm, tn), lambda i, j: (i, ma_off + d // tn + j))],
        out_specs=[tile, tile, tile],
        out_shape=[jax.ShapeDtypeStruct((l, d), BF16)] * 3,
        compiler_params=_params("parallel", "parallel"),
    )(oa, op, w_au, w_pu, zin, zin)


def _merge_bwd(gout, w_out, zin, ya, yp, name):
    l, d = gout.shape
    tm = _tile(l, 1088, 16)
    tn = _tile(d, 512, 128)
    ma_off = (zin.shape[1] - 2 * d) // tn

    def body(g_ref, w_ref, ma_ref, mp_ref, ya_ref, yp_ref, dya_ref, dyp_ref, dma_ref, dmp_ref):
        dm = _dot(g_ref[...], w_ref[...], tb=True)
        sa = _sigmoid(ma_ref[...])
        sp = _sigmoid(mp_ref[...])
        dya_ref[...] = (dm * sa).astype(BF16)
        dyp_ref[...] = (dm * sp).astype(BF16)
        dma_ref[...] = (dm * ya_ref[...].astype(F32) * (sa * (1.0 - sa))).astype(BF16)
        dmp_ref[...] = (dm * yp_ref[...].astype(F32) * (sp * (1.0 - sp))).astype(BF16)

    tile = pl.BlockSpec((tm, tn), lambda i, j: (i, j))
    out = jax.ShapeDtypeStruct((l, d), BF16)
    return pl.pallas_call(
        body, name=name, grid=(l // tm, d // tn),
        in_specs=[pl.BlockSpec((tm, d), lambda i, j: (i, 0)), pl.BlockSpec((tn, d), lambda i, j: (j, 0)),
                  pl.BlockSpec((tm, tn), lambda i, j: (i, ma_off + j)),
                  pl.BlockSpec((tm, tn), lambda i, j: (i, ma_off + d // tn + j)), tile, tile],
        out_specs=[tile, tile, tile, tile],
        out_shape=[out, out, out, out],
        compiler_params=_params("parallel", "parallel"),
    )(gout, w_out, zin, zin, ya, yp)


def _adamw(parts, w, m, v, name, first=0, into=None):
    n_arrays, n_parts, r, c = parts.shape
    tr = _tile(r, max(8, (256 * 1024) // c // 8 * 8), 8)
    bias1 = 1.0 - ADAM_B1 ** ADAM_STEP
    bias2 = 1.0 - ADAM_B2 ** ADAM_STEP

    def body(p_ref, w_ref, m_ref, v_ref, *rest):
        g_ref, d_ref, nm_ref, nv_ref = rest[-4:]
        g = p_ref[0].astype(F32)
        for j in range(1, n_parts):
            g = g + p_ref[j].astype(F32)
        nm = ADAM_B1 * m_ref[...] + (1.0 - ADAM_B1) * g
        nv = ADAM_B2 * v_ref[...] + (1.0 - ADAM_B2) * (g * g)
        g_ref[...] = g
        nm_ref[...] = nm
        nv_ref[...] = nv
        d_ref[...] = -ADAM_LR * ((nm / bias1) / (jnp.sqrt(nv / bias2) + ADAM_EPS) + ADAM_WD * w_ref[...])

    tile = pl.BlockSpec((None, tr, c), lambda a, i: (a + first, i, 0))
    out = jax.ShapeDtypeStruct(w.shape, F32)
    kept = [] if into is None else list(into)
    return pl.pallas_call(
        body, name=name, grid=(n_arrays, r // tr),
        in_specs=[pl.BlockSpec((None, n_parts, tr, c), lambda a, i: (a, 0, i, 0)), tile, tile, tile]
        + [pl.BlockSpec(memory_space=pl.ANY)] * len(kept),
        out_specs=[tile, tile, tile, tile], out_shape=[out, out, out, out],
        input_output_aliases={4 + k: k for k in range(len(kept))},
        compiler_params=_params("parallel", "parallel"),
    )(parts, w, m, v, *kept)


def _position():
    return lax.axis_index("x"), lax.axis_index("y"), lax.axis_index("c")


def _block_of(ref, axis, size, index):
    idx = [slice(None)] * len(ref.shape)
    idx[axis] = pl.ds(index * size, size)
    return ref.at[tuple(idx)]


HBM_SPEC = pl.BlockSpec(memory_space=pltpu.HBM)
SEM_SPEC = pl.BlockSpec(memory_space=pltpu.SEMAPHORE)
SIDE_EFFECT = pltpu.CompilerParams(has_side_effects=pltpu.SideEffectType.DATAFLOW_SIDE_EFFECTING)


def _split_start(make_copies, n_copies, buffers, name):
    n = len(buffers)

    def body(*refs):
        send_sems, recv_sems = refs[n], refs[n + 1]
        for cp in make_copies(refs[:n], send_sems, recv_sems):
            cp.start()
        refs[-1][...] = jnp.zeros_like(refs[-1])

    sems = pltpu.SemaphoreType.DMA((n_copies,))
    return pl.pallas_call(
        body, name=name, in_specs=[HBM_SPEC] * n,
        out_shape=(sems, sems, *[pltpu.HBM(b.shape, b.dtype) for b in buffers], jax.ShapeDtypeStruct((8, 128), F32)),
        out_specs=(SEM_SPEC, SEM_SPEC, *[HBM_SPEC] * n, pl.BlockSpec(memory_space=pltpu.VMEM)),
        input_output_aliases={i: 2 + i for i in range(n)}, compiler_params=SIDE_EFFECT,
    )(*[pltpu.with_memory_space_constraint(b, pltpu.HBM) for b in buffers])


def _split_wait(make_copies, started, after, name):
    send_sems, recv_sems, *buffers = started[:-1]
    n = len(buffers)

    def body(*refs):
        copies = make_copies(refs[:n], refs[n], refs[n + 1])
        for cp in copies:
            cp.wait_send()
        for cp in copies:
            cp.wait_recv()

    return pl.pallas_call(
        body, name=name, in_specs=[HBM_SPEC] * n + [SEM_SPEC, SEM_SPEC, pl.BlockSpec(memory_space=pl.ANY)],
        out_shape=[pltpu.HBM(b.shape, b.dtype) for b in buffers], out_specs=[HBM_SPEC] * n,
        input_output_aliases={i: i for i in range(n)}, compiler_params=SIDE_EFFECT,
    )(*buffers, send_sems, recv_sems, after)


def _gather_copies(axes, sizes, level):
    def make(fulls, send_sems, recv_sems):
        x, y, c = _position()
        chips = [(1 - x, y), (x, 1 - y), (1 - x, 1 - y)]
        copies = []
        for a, full in enumerate(fulls):
            def copy(k, block, to, full=full, a=a):
                rows = _block_of(full, axes[a], sizes[a], 4 * block[0] + 2 * block[1] + block[2])
                idx = a * (4 if level == 1 else 3) + k
                return pltpu.make_async_remote_copy(src_ref=rows, dst_ref=rows, send_sem=send_sems.at[idx],
                                                    recv_sem=recv_sems.at[idx], device_id=to, device_id_type=MESH)
            if level == 1:
                copies.append(copy(0, (x, y, c), (x, y, 1 - c)))
                copies += [copy(1 + j, (x, y, c), (*chip, c)) for j, chip in enumerate(chips)]
            else:
                copies += [copy(j, (*chip, c), (x, y, 1 - c)) for j, chip in enumerate(chips)]
        return copies
    return make


def _exchange_copies(axes, sizes, layer, n_src):
    flips = [(a, b, d) for a in (0, 1) for b in (0, 1) for d in (0, 1)][1:]

    def make(buffers, send_sems, recv_sems):
        x, y, c = _position()
        my_index = 4 * x + 2 * y + c
        copies = []
        for a in range(n_src):
            for k, flip in enumerate(flips):
                px, py, pc = x ^ flip[0], y ^ flip[1], c ^ flip[2]
                copies.append(pltpu.make_async_remote_copy(
                    src_ref=_block_of(buffers[a], axes[a], sizes[a], 4 * px + 2 * py + pc),
                    dst_ref=buffers[n_src + a].at[layer, my_index],
                    send_sem=send_sems.at[a * 7 + k], recv_sem=recv_sems.at[a * 7 + k],
                    device_id=(px, py, pc), device_id_type=MESH))
        return copies
    return make


def _allgather_small(v, name, after=None):
    r, c = v.shape
    flips = [(a, b, d) for a in (0, 1) for b in (0, 1) for d in (0, 1)][1:]

    def body(v_ref, *rest):
        out_ref, send_sems, recv_sems = rest[-3:]
        x, y, c_ = _position()
        my_index = 4 * x + 2 * y + c_
        out_ref[my_index] = v_ref[...]
        sends = []
        for k, flip in enumerate(flips):
            peer = (x ^ flip[0], y ^ flip[1], c_ ^ flip[2])
            cp = pltpu.make_async_remote_copy(
                src_ref=v_ref, dst_ref=out_ref.at[my_index],
                send_sem=send_sems.at[k], recv_sem=recv_sems.at[k], device_id=peer, device_id_type=MESH)
            cp.start()
            sends.append(cp)
        for k, flip in enumerate(flips):
            px, py, pc = x ^ flip[0], y ^ flip[1], c_ ^ flip[2]
            pltpu.make_async_remote_copy(
                src_ref=v_ref, dst_ref=out_ref.at[4 * px + 2 * py + pc],
                send_sem=send_sems.at[k], recv_sem=recv_sems.at[k],
                device_id=(px, py, pc), device_id_type=MESH).wait_recv()
        for cp in sends:
            cp.wait_send()

    return pl.pallas_call(
        body, name=name,
        in_specs=[pl.BlockSpec(memory_space=pltpu.VMEM)] + _after_specs(after),
        out_specs=pl.BlockSpec(memory_space=pltpu.VMEM),
        out_shape=jax.ShapeDtypeStruct((N_DEV, r, c), v.dtype),
        scratch_shapes=[pltpu.SemaphoreType.DMA((7,)), pltpu.SemaphoreType.DMA((7,))],
    )(v, *_after_operands(after))


def kernel(x, meta_tokens, norm_gain, w_in, pool_w, pool_scale, w_attn_up, w_pool_up, w_out, final_gain, loss_target, m_meta_tokens, m_norm_gain, m_w_in, m_pool_w, m_pool_scale, m_w_attn_up, m_w_pool_up, m_w_out, m_final_gain, v_meta_tokens, v_norm_gain, v_w_in, v_pool_w, v_pool_scale, v_w_attn_up, v_w_pool_up, v_w_out, v_final_gain):
    _, seq, d = x.shape
    n_meta = meta_tokens.shape[0]
    depth = w_in.shape[0]
    sb_width = w_attn_up.shape[1]
    pool_width = w_pool_up.shape[1]
    n_heads = sb_width // HEAD_DIM
    n_groups = pool_w.shape[1]
    gd = pool_w.shape[3]
    assert n_groups == len(POOL_WINDOWS) and gd * n_groups == pool_width
    assert w_in.shape[2] * N_DEV == 4 * sb_width + 2 * pool_width + 2 * d
    l_real = n_meta + seq
    l_pad = -(-l_real // ATT_BLOCK) * ATT_BLOCK
    my_index = 4 * lax.axis_index("x") + 2 * lax.axis_index("y") + lax.axis_index("c")

    me = jnp.reshape(my_index, (1,)).astype(jnp.int32)
    g_named = [("w_in", w_in), ("pool_w", pool_w), ("w_attn_up", w_attn_up), ("w_pool_up", w_pool_up), ("w_out", w_out)]
    g_axes = [1, 1, 1, 1, 0]
    g_sizes = [w.shape[1 + ax] for (_, w), ax in zip(g_named, g_axes)]
    level1 = _gather_copies(g_axes, g_sizes, 1)
    level2 = _gather_copies(g_axes, g_sizes, 2)

    def gather_start(i, after=None):
        fulls = [_cast_place(w, i, ax, me, "cast_" + nm, after) for (nm, w), ax in zip(g_named, g_axes)]
        return _split_start(level1, 4 * len(fulls), fulls, "gather1_start_%d" % i)

    def gather_forward(i, started, after):
        arrived = _split_wait(level1, started, after, "gather1_wait_%d" % i)
        return _split_start(level2, 3 * len(arrived), arrived, "gather2_start_%d" % i)

    meta_all = _allgather_small(meta_tokens, "allgather_meta")
    meta_full = jnp.transpose(meta_all, (1, 0, 2)).reshape(n_meta, d)
    first = gather_start(0)
    second = gather_forward(0, first, first[-1])

    pad_rows = l_pad - l_real
    hs = jnp.concatenate([meta_full, x[0], jnp.zeros((pad_rows, d), F32)], axis=0)
    target = jnp.concatenate([jnp.zeros((n_meta, d), F32), loss_target[0], jnp.zeros((pad_rows, d), F32)], axis=0)
    u_off = 4 * sb_width // gd
    saved, weights = [], []
    for i in range(depth):
        h = _rms_fwd(hs, norm_gain[i][None], "rms_fwd")
        weights.append(_split_wait(level2, second, h, "gather2_wait_%d" % i))
        wi, pw, wau, wpu, wo = weights[i]
        more = i + 1 < depth
        first = gather_start(i + 1, wi) if more else None
        zin = _matmul(h, wi, after=first[-1] if more else None, name="mm_zin")
        o, oa, tot = _attn_fwd(zin, n_heads, "attn_fwd")
        op = _pool_fwd(zin, pw, pool_scale[i][None], u_off, "pool_fwd")
        ya, yp, merged = _merge_fwd(oa, op, wau, wpu, zin, "merge_fwd")
        second = gather_forward(i + 1, first, merged) if more else None
        saved.append((hs, h, zin, o, tot, oa, op, ya, yp, merged))
        hs = _matmul(merged, wo, add=hs, after=second[-1] if more else None, name="mm_out")
    g, gb, d_final_gain, loss_part = _final_loss(hs, final_gain[None], target, n_meta, l_real, "final_loss")
    loss = lax.psum(loss_part[0, 0], ("x", "y", "c"))

    d_norm_gain = [None] * depth
    d_pool_scale = [None] * depth
    axes_a, axes_b = [1, 1, 0], [1, 1]
    blocks_a, blocks_b = [w_attn_up[0], w_pool_up[0], w_out[0]], [w_in[0], pool_w[0]]
    sizes_a = [b.shape[ax] for b, ax in zip(blocks_a, axes_a)]
    sizes_b = [b.shape[ax] for b, ax in zip(blocks_b, axes_b)]
    land_a = [lax.empty((depth, N_DEV, *b.shape), BF16) for b in blocks_a]
    land_b = [lax.empty((depth - 1, N_DEV, *b.shape), BF16) for b in blocks_b]
    land_b0 = [lax.empty((1, N_DEV, *b.shape), BF16) for b in blocks_b]

    def exchange_start(grads, landing, axes, sizes, slot, name):
        landing = [lax.dynamic_update_slice(
            zone, lax.dynamic_slice_in_dim(grad, my_index * size, size, ax)[None, None],
            (slot, my_index) + (0,) * grad.ndim) for zone, grad, ax, size in zip(landing, grads, axes, sizes)]
        copies = _exchange_copies(axes, sizes, slot, len(grads))
        return copies, _split_start(copies, 7 * len(grads), list(grads) + landing, name)

    def exchange_wait(pending, n_src, after, name):
        return _split_wait(pending[0], pending[1], after, name)[n_src:]

    pend_a = pend_b = None
    for i in reversed(range(depth)):
        wi, pw, wau, wpu, wo = weights[i]
        hs_in, h, zin, o, tot, oa, op, ya, yp, merged = saved[i]
        dya, dyp, dma, dmp = _merge_bwd(gb, wo, zin, ya, yp, "merge_bwd")
        dw_out = _matmul(merged, gb, ta=True, out_dtype=BF16, name="mm_dw_out", tm=1024, tk=2176)
        doa = _matmul(dya, wau, tb=True, name="mm_doa")
        dw_au = _matmul(oa, dya, ta=True, out_dtype=BF16, name="mm_dw_au", tm=1024, tk=2176)
        dop = _matmul(dyp, wpu, tb=True, name="mm_dop")
        dw_pu = _matmul(op, dyp, ta=True, out_dtype=BF16, name="mm_dw_pu", tm=1024, tk=2176)
        if pend_a is not None:
            land_a = exchange_wait(pend_a, 3, dw_pu, "exchange_a_wait_%d" % (i + 1))
        pend_a = exchange_start([dw_au, dw_pu, dw_out], land_a, axes_a, sizes_a, i, "exchange_a_start_%d" % i)
        dq, dk, dv, dga = _attn_bwd(zin, o, tot, doa, n_heads, "attn_bwd", after=pend_a[1][-1])
        du, dgp, dpw, dps = _pool_bwd(zin, dop, pw, pool_scale[i][None], u_off, "pool_bwd")
        dzin = [dq, dk, dv, dga, du, dgp, dma, dmp]
        dw_in = _matmul(h, dzin, ta=True, out_dtype=BF16, name="mm_dw_in", tm=1024, tk=1088)
        if pend_b is not None:
            land_b = exchange_wait(pend_b, 2, dw_in, "exchange_b_wait_%d" % (i + 1))
        pend_b = exchange_start([dw_in, dpw], land_b if i > 0 else land_b0, axes_b, sizes_b, max(i - 1, 0),
                                "exchange_b_start_%d" % i)
        dh = _matmul(dzin, wi, tb=True, after=pend_b[1][-1], name="mm_dh")
        g, gb, dng = _rms_bwd(dh, hs_in, norm_gain[i][None], g, "rms_bwd")
        d_norm_gain[i] = dng
        d_pool_scale[i] = dps
    land_a = exchange_wait(pend_a, 3, gb, "exchange_a_wait_0")
    grad_x = g[n_meta:l_real][None]

    def sharded(parts, w, m, v, name, first=0, into=None):
        flat = (depth, -1, w.shape[-1])
        return _adamw(parts.reshape(parts.shape[0], N_DEV, -1, w.shape[-1]), w.reshape(flat), m.reshape(flat),
                      v.reshape(flat), name, first, into)

    out_au = sharded(land_a[0], w_attn_up, m_w_attn_up, v_w_attn_up, "adamw_w_attn_up")
    out_pu = sharded(land_a[1], w_pool_up, m_w_pool_up, v_w_pool_up, "adamw_w_pool_up")
    out_wo = sharded(land_a[2], w_out, m_w_out, v_w_out, "adamw_w_out")
    out_wi = out_pw = None
    if depth > 1:
        out_wi = sharded(land_b[0], w_in, m_w_in, v_w_in, "adamw_w_in", 1)
        out_pw = sharded(land_b[1], pool_w, m_pool_w, v_pool_w, "adamw_pool_w", 1)
    land_b0 = exchange_wait(pend_b, 2, out_wo[0] if out_wi is None else out_wi[0], "exchange_b_wait_0")
    out_wi = sharded(land_b0[0], w_in, m_w_in, v_w_in, "adamw_w_in_0", 0, out_wi)
    out_pw = sharded(land_b0[1], pool_w, m_pool_w, v_pool_w, "adamw_pool_w_0", 0, out_pw)
    out_wi, out_pw, out_au, out_pu, out_wo = [
        [t.reshape(w.shape) for t in res] for res, w in
        zip([out_wi, out_pw, out_au, out_pu, out_wo], [w_in, pool_w, w_attn_up, w_pool_up, w_out])]

    zeros_ps = jnp.zeros((depth, d - pool_width), F32)
    small_rows = [jnp.concatenate(d_norm_gain, axis=0),
                  jnp.concatenate([jnp.concatenate(d_pool_scale, axis=0), zeros_ps], axis=1), d_final_gain]
    n_small = 2 * depth + 1
    small_pad = -(-n_small // 8) * 8
    small = jnp.concatenate(small_rows + [jnp.zeros((small_pad - n_small, d), F32), g[:n_meta]], axis=0)
    small_all = _allgather_small(small, "allgather_small", after=land_b0[0])

    def replicated(rows, width, w, m, v, name):
        parts = lax.slice(small_all, (0, rows[0], 0), (N_DEV, rows[1], width))
        return [t[0] for t in _adamw(parts[None], w[None], m[None], v[None], name)]

    out_ng = replicated((0, depth), d, norm_gain, m_norm_gain, v_norm_gain, "adamw_norm_gain")
    out_ps = replicated((depth, 2 * depth), pool_width, pool_scale, m_pool_scale, v_pool_scale, "adamw_pool_scale")
    out_fg = [t[0] for t in replicated((2 * depth, 2 * depth + 1), d, final_gain[None], m_final_gain[None],
                                       v_final_gain[None], "adamw_final_gain")]
    cols = d // N_DEV
    meta_parts = lax.dynamic_slice(small_all, (0, small_pad, my_index * cols), (N_DEV, n_meta, cols))
    out_meta = [t[0] for t in _adamw(meta_parts[None], meta_tokens[None], m_meta_tokens[None], v_meta_tokens[None],
                                     "adamw_meta")]

    by_weight = [out_meta, out_ng, out_wi, out_pw, out_ps, out_au, out_pu, out_wo, out_fg]
    return (loss, grad_x, *[o[0] for o in by_weight], *[o[1] for o in by_weight],
            *[o[2] for o in by_weight], *[o[3] for o in by_weight])
```

```python
import jax
import jax.numpy as jnp
from jax import lax
from jax.experimental import pallas as pl
from jax.experimental.pallas import tpu as pltpu

F32 = jnp.float32
BF16 = jnp.bfloat16
MESH = pl.DeviceIdType.MESH

N_DEV = 8
HEAD_DIM = 128
ATT_BLOCK = 128
ATT_HEADS = 4
ATT_HEADS_FWD = 8
POOL_WINDOWS = (2, 4, 8, 16)
RMS_EPS = 1e-6
ADAM_LR, ADAM_B1, ADAM_B2, ADAM_EPS, ADAM_WD, ADAM_STEP = 0.001, 0.9, 0.999, 1e-08, 0.01, 10
VMEM_LIMIT_BYTES = 56 * 1024 * 1024


def _tile(n, target, mult):
    if n <= target:
        return n
    best = 0
    for t in range(mult, target + 1, mult):
        if n % t == 0:
            best = t
    assert best > 0, (n, target, mult)
    return best


def _params(*sem):
    return pltpu.CompilerParams(dimension_semantics=sem, vmem_limit_bytes=VMEM_LIMIT_BYTES)


def _sigmoid(x):
    return 1.0 / (1.0 + jnp.exp(-x))


def _dot(a, b, ta=False, tb=False):
    dims = (((0 if ta else 1,), (1 if tb else 0,)), ((), ()))
    return lax.dot_general(a.astype(BF16), b.astype(BF16), dims, preferred_element_type=F32)


def _dot_split(a, b):
    hi = a.astype(BF16)
    lo = (a - hi.astype(F32)).astype(BF16)
    return _dot(hi, b) + _dot(lo, b)


def _cast_place(x, layer, axis, me, name, after=None):
    blk = x.shape[1:]
    full = list(blk)
    full[axis] *= N_DEV
    if len(blk) == 2:
        r, c = blk
        tr = _tile(r, max(16, (512 * 1024) // c // 16 * 16), 16)
        steps = r // tr
        in_spec = pl.BlockSpec((None, tr, c), lambda i, me_ref: (layer, i, 0))
        if axis == 1:
            out_spec = pl.BlockSpec((tr, c), lambda i, me_ref: (i, me_ref[0]))
        else:
            out_spec = pl.BlockSpec((tr, c), lambda i, me_ref: (me_ref[0] * steps + i, 0))
    else:
        assert len(blk) == 3 and axis == 1
        steps = 1
        in_spec = pl.BlockSpec((None, *blk), lambda i, me_ref: (layer, 0, 0, 0))
        out_spec = pl.BlockSpec(blk, lambda i, me_ref: (0, me_ref[0], 0))

    def body(me_ref, x_ref, *rest):
        rest[-1][...] = x_ref[...].astype(BF16)

    return pl.pallas_call(
        body, name=name,
        grid_spec=pltpu.PrefetchScalarGridSpec(num_scalar_prefetch=1, grid=(steps,),
                                               in_specs=[in_spec] + _after_specs(after), out_specs=out_spec),
        out_shape=jax.ShapeDtypeStruct(tuple(full), BF16),
        compiler_params=_params("arbitrary"),
    )(me, x, *_after_operands(after))


def _after_operands(after):
    return [] if after is None else [after]


def _after_specs(after):
    return [] if after is None else [pl.BlockSpec(memory_space=pl.ANY)]


def _matmul(a, b, *, ta=False, tb=False, out_dtype=F32, add=None, after=None, name, tm=1088, tn=1024, tk=2048):
    a_list = list(a) if isinstance(a, (list, tuple)) else None
    b_list = list(b) if isinstance(b, (list, tuple)) else None
    assert not (a_list and ta) and not (b_list and tb) and not (a_list and b_list)
    if a_list:
        m, k = a_list[0].shape[0], sum(p.shape[1] for p in a_list)
    else:
        m, k = (a.shape[1], a.shape[0]) if ta else a.shape
    if b_list:
        n = sum(p.shape[1] for p in b_list)
    else:
        n = b.shape[0] if tb else b.shape[1]
    tm = _tile(m, tm, 128 if ta else 16)
    tn = _tile(n, tn, 128)
    tk = _tile(k, tk, 128 if (not ta or tb) else 16)
    if a_list:
        while any(p.shape[1] % tk for p in a_list):
            tk //= 2
    if b_list:
        while any(p.shape[1] % tn for p in b_list):
            tn //= 2
    nk = k // tk

    def ranges(pieces, tile):
        out, start = [], 0
        for p in pieces:
            out.append((start, start + p.shape[1] // tile))
            start = out[-1][1]
        return out

    if a_list:
        a_ranges = ranges(a_list, tk)
        a_specs = [pl.BlockSpec((tm, tk), lambda i, j, kk, s=s, e=e: (i, jnp.clip(kk - s, 0, e - s - 1)))
                   for s, e in a_ranges]
    else:
        a_specs = [pl.BlockSpec((tk, tm), lambda i, j, kk: (kk, i)) if ta
                   else pl.BlockSpec((tm, tk), lambda i, j, kk: (i, kk))]
    if b_list:
        b_ranges = ranges(b_list, tn)
        b_specs = [pl.BlockSpec((tk, tn), lambda i, j, kk, s=s, e=e: (
            jnp.where(jnp.logical_and(j >= s, j < e), kk, 0), jnp.clip(j - s, 0, e - s - 1))) for s, e in b_ranges]
    else:
        b_specs = [pl.BlockSpec((tn, tk), lambda i, j, kk: (j, kk)) if tb
                   else pl.BlockSpec((tk, tn), lambda i, j, kk: (kk, j))]
    o_spec = pl.BlockSpec((tm, tn), lambda i, j, kk: (i, j))
    has_add = add is not None
    na, nb = len(a_specs), len(b_specs)
    simple = nk == 1 and not a_list and not b_list

    def body(*refs):
        a_refs, b_refs = refs[:na], refs[na:na + nb]
        add_ref = refs[na + nb] if has_add else None
        o_ref, acc_ref = refs[-2], refs[-1]
        j, kk = pl.program_id(1), pl.program_id(2)

        def finish(total):
            if has_add:
                total = total + add_ref[...]
            o_ref[...] = total.astype(out_dtype)

        if simple:
            finish(_dot(a_refs[0][...], b_refs[0][...], ta, tb))
            return

        @pl.when(kk == 0)
        def _():
            acc_ref[...] = jnp.zeros_like(acc_ref)

        if a_list:
            for (s, e), a_ref in zip(a_ranges, a_refs):
                @pl.when(jnp.logical_and(kk >= s, kk < e))
                def _(a_ref=a_ref):
                    acc_ref[...] += _dot(a_ref[...], b_refs[0][...], ta, tb)
        elif b_list:
            for (s, e), b_ref in zip(b_ranges, b_refs):
                @pl.when(jnp.logical_and(j >= s, j < e))
                def _(b_ref=b_ref):
                    acc_ref[...] += _dot(a_refs[0][...], b_ref[...], ta, tb)
        else:
            acc_ref[...] += _dot(a_refs[0][...], b_refs[0][...], ta, tb)

        @pl.when(kk == nk - 1)
        def _():
            finish(acc_ref[...])

    operands = (a_list or [a]) + (b_list or [b]) + ([add] if has_add else []) + _after_operands(after)
    in_specs = a_specs + b_specs + ([o_spec] if has_add else []) + _after_specs(after)
    return pl.pallas_call(
        body, name=name, grid=(m // tm, n // tn, nk),
        in_specs=in_specs, out_specs=o_spec,
        out_shape=jax.ShapeDtypeStruct((m, n), out_dtype),
        scratch_shapes=[pltpu.VMEM((8, 128) if simple else (tm, tn), F32)],
        compiler_params=_params("parallel", "parallel", "arbitrary"),
    )(*operands)


def _rms_fwd(x, gain, name, after=None):
    l, d = x.shape
    tr = _tile(l, 272, 16)

    def body(x_ref, g_ref, *rest):
        o_ref = rest[-1]
        xv = x_ref[...]
        r = lax.rsqrt(jnp.mean(xv * xv, axis=-1, keepdims=True) + RMS_EPS)
        o_ref[...] = (xv * r * g_ref[...]).astype(BF16)

    return pl.pallas_call(
        body, name=name, grid=(l // tr,),
        in_specs=[pl.BlockSpec((tr, d), lambda i: (i, 0)), pl.BlockSpec((1, d), lambda i: (0, 0))] + _after_specs(after),
        out_specs=pl.BlockSpec((tr, d), lambda i: (i, 0)),
        out_shape=jax.ShapeDtypeStruct((l, d), BF16),
        compiler_params=_params("parallel"),
    )(x, gain, *_after_operands(after))


def _rms_bwd(dh, x, gain, gout, name):
    l, d = x.shape
    tr = _tile(l, 272, 16)

    def body(dh_ref, x_ref, g_ref, go_ref, gx_ref, gxb_ref, dg_ref):
        xv = x_ref[...]
        r = lax.rsqrt(jnp.mean(xv * xv, axis=-1, keepdims=True) + RMS_EPS)
        nv = xv * r
        dhv = dh_ref[...]
        dn = dhv * g_ref[...]
        dx = r * (dn - nv * jnp.mean(dn * nv, axis=-1, keepdims=True))
        gx = go_ref[...] + dx
        gx_ref[...] = gx
        gxb_ref[...] = gx.astype(BF16)
        part = jnp.sum(dhv * nv, axis=0, keepdims=True)

        @pl.when(pl.program_id(0) == 0)
        def _():
            dg_ref[...] = part

        @pl.when(pl.program_id(0) > 0)
        def _():
            dg_ref[...] += part

    row = pl.BlockSpec((tr, d), lambda i: (i, 0))
    vec = pl.BlockSpec((1, d), lambda i: (0, 0))
    return pl.pallas_call(
        body, name=name, grid=(l // tr,),
        in_specs=[row, row, vec, row], out_specs=[row, row, vec],
        out_shape=[jax.ShapeDtypeStruct((l, d), F32), jax.ShapeDtypeStruct((l, d), BF16),
                   jax.ShapeDtypeStruct((1, d), F32)],
        compiler_params=_params("arbitrary"),
    )(dh, x, gain, gout)


def _final_loss(x, gain, target, row_lo, row_hi, name):
    l, d = x.shape
    tr = _tile(l, 272, 16)

    def body(x_ref, g_ref, t_ref, gx_ref, gxb_ref, dg_ref, loss_ref):
        i = pl.program_id(0)
        xv = x_ref[...]
        r = lax.rsqrt(jnp.mean(xv * xv, axis=-1, keepdims=True) + RMS_EPS)
        nv = xv * r
        gv = g_ref[...]
        rows = i * tr + lax.broadcasted_iota(jnp.int32, (tr, 1), 0)
        valid = jnp.logical_and(rows >= row_lo, rows < row_hi)
        err = jnp.where(valid, nv * gv - t_ref[...], 0.0)
        dy = err * (1.0 / d)
        dn = dy * gv
        gx = r * (dn - nv * jnp.mean(dn * nv, axis=-1, keepdims=True))
        gx_ref[...] = gx
        gxb_ref[...] = gx.astype(BF16)
        part = jnp.sum(dy * nv, axis=0, keepdims=True)
        lpart = jnp.full((1, 128), 0.5 * jnp.sum(jnp.mean(err * err, axis=-1, keepdims=True)), F32)

        @pl.when(i == 0)
        def _():
            dg_ref[...] = part
            loss_ref[...] = lpart

        @pl.when(i > 0)
        def _():
            dg_ref[...] += part
            loss_ref[...] += lpart

    row = pl.BlockSpec((tr, d), lambda i: (i, 0))
    vec = pl.BlockSpec((1, d), lambda i: (0, 0))
    return pl.pallas_call(
        body, name=name, grid=(l // tr,),
        in_specs=[row, vec, row], out_specs=[row, row, vec, pl.BlockSpec((1, 128), lambda i: (0, 0))],
        out_shape=[jax.ShapeDtypeStruct((l, d), F32), jax.ShapeDtypeStruct((l, d), BF16),
                   jax.ShapeDtypeStruct((1, d), F32), jax.ShapeDtypeStruct((1, 128), F32)],
        compiler_params=_params("arbitrary"),
    )(x, gain, target)


def _log1m_beta(z):
    return -(jnp.maximum(z, 0.0) + jnp.log(1.0 + jnp.exp(-jnp.abs(z))))


def _tri(n, relation):
    r = lax.broadcasted_iota(jnp.int32, (n, n), 0)
    c = lax.broadcasted_iota(jnp.int32, (n, n), 1)
    return jnp.where(relation(r, c), 1.0, 0.0).astype(BF16)


def _running_sum(x, tri, backward, split):
    n = tri.shape[0]
    blocks = [x[:, s * n:(s + 1) * n] for s in range(x.shape[1] // n)]
    inner = [(_dot_split if split else _dot)(b, tri) for b in blocks]
    totals = [jnp.sum(b, axis=1, keepdims=True) for b in blocks]
    order = list(reversed(range(len(blocks)))) if backward else list(range(len(blocks)))
    out, offset = [None] * len(blocks), None
    for s in order:
        out[s] = inner[s] if offset is None else inner[s] + offset
        offset = totals[s] if offset is None else offset + totals[s]
    return (out[0] if len(out) == 1 else jnp.concatenate(out, axis=1)), offset


def _head(hh):
    return slice(hh * HEAD_DIM, (hh + 1) * HEAD_DIM)


def _heads(x, hb):
    return jnp.stack([x[:, _head(hh)] for hh in range(hb)], axis=0)


def _bdot(a, b, ta=False, tb=False):
    dims = (((1 if ta else 2,), (2 if tb else 1,)), ((0,), (0,)))
    return lax.dot_general(a.astype(BF16), b.astype(BF16), dims, preferred_element_type=F32)


def _attn_specs(l, hb, n_heads):
    width = hb * HEAD_DIM
    groups = n_heads // hb

    def tile(section):
        return pl.BlockSpec((ATT_BLOCK, width), lambda h, i: (i, section * groups + h))

    def slab(section):
        return pl.BlockSpec((l, width), lambda h, i: (0, section * groups + h))

    return tile, slab


def _attn_fwd(zin, n_heads, name):
    l = zin.shape[0]
    t = ATT_BLOCK
    hb = next(h for h in (ATT_HEADS_FWD, ATT_HEADS, 1) if n_heads % h == 0)
    scale = HEAD_DIM ** -0.5

    def body(q_ref, k_ref, v_ref, g_ref, o_ref, oa_ref, tot_ref):
        i = pl.program_id(1)
        after = _tri(t, lambda r, c: r > c)
        causal = (lax.broadcasted_iota(jnp.int32, (t, t), 1) < lax.broadcasted_iota(jnp.int32, (t, t), 0))[None]
        q = _heads(q_ref[...] * scale, hb).astype(BF16)

        def tile(k0, w, carry, diagonal=False):
            run, acc = carry
            z = _bdot(q, _heads(k_ref[pl.ds(k0, w), :], hb), tb=True)
            lb_all = _log1m_beta(z)
            lb = jnp.where(causal, lb_all, 0.0) if diagonal else lb_all
            between, total = _running_sum(lb.reshape(hb * t, w), after, True, True)
            a = jnp.exp(z + lb_all + between.reshape(hb, t, w) + run)
            if diagonal:
                a = jnp.where(causal, a, 0.0)
            return run + total.reshape(hb, t, 1), acc + _bdot(a, _heads(v_ref[pl.ds(k0, w), :], hb))

        rem = i % 4
        carry = tile(pl.multiple_of(i * t, t), t, (jnp.zeros((hb, t, 1), F32), jnp.zeros((hb, t, HEAD_DIM), F32)), True)
        carry = lax.cond(rem % 2 == 1, lambda c: tile(pl.multiple_of((i - 1) * t, t), t, c), lambda c: c, carry)
        carry = lax.cond(rem >= 2, lambda c: tile(pl.multiple_of((i - rem % 2 - 2) * t, 2 * t), 2 * t, c),
                         lambda c: c, carry)
        run, o = lax.fori_loop(
            0, i // 4, lambda it, c: tile(pl.multiple_of((i - rem - 4 * it - 4) * t, 4 * t), 4 * t, c), carry)
        for hh in range(hb):
            gate = g_ref[:, _head(hh)]
            o_ref[:, _head(hh)] = o[hh]
            oa_ref[:, _head(hh)] = (o[hh] * (gate * _sigmoid(gate))).astype(BF16)
            tot_ref[:, _head(hh)] = jnp.broadcast_to(run[hh], (t, HEAD_DIM))

    tile_spec, slab_spec = _attn_specs(l, hb, n_heads)
    width = n_heads * HEAD_DIM
    return pl.pallas_call(
        body, name=name, grid=(n_heads // hb, l // t),
        in_specs=[tile_spec(0), slab_spec(1), slab_spec(2), tile_spec(3)],
        out_specs=[tile_spec(0), tile_spec(0), tile_spec(0)],
        out_shape=[jax.ShapeDtypeStruct((l, width), F32), jax.ShapeDtypeStruct((l, width), BF16),
                   jax.ShapeDtypeStruct((l, width), F32)],
        compiler_params=_params("parallel", "arbitrary"),
    )(zin, zin, zin, zin)


def _attn_bwd(zin, o, tot, doa, n_heads, name, after=None):
    l = zin.shape[0]
    t = ATT_BLOCK
    nq = l // t
    hb = ATT_HEADS if n_heads % ATT_HEADS == 0 else 1
    scale = HEAD_DIM ** -0.5

    def body(q_ref, k_ref, v_ref, g_ref, o_ref, tot_ref, doa_ref, *rest):
        dq_ref, dk_ref, dv_ref, dg_ref, dk_acc, dv_acc = rest[-6:]
        i = pl.program_id(1)

        @pl.when(i == 0)
        def _():
            dk_acc[...] = jnp.zeros_like(dk_acc)
            dv_acc[...] = jnp.zeros_like(dv_acc)

        upto = _tri(t, lambda r, c: r <= c)
        before = _tri(t, lambda r, c: r < c)
        causal = (lax.broadcasted_iota(jnp.int32, (t, t), 1) < lax.broadcasted_iota(jnp.int32, (t, t), 0))[None]
        gate = g_ref[...]
        sg = _sigmoid(gate)
        doav = doa_ref[...]
        dg_ref[...] = (doav * o_ref[...] * (sg * (1.0 + gate * (1.0 - sg)))).astype(BF16)
        do = _heads(doav * (gate * sg), hb).astype(BF16)
        q = _heads(q_ref[...] * scale, hb).astype(BF16)
        total = _heads(tot_ref[...], hb)[:, :, 0:1]

        def tile(k0, w, carry, diagonal=False):
            run, pre, dq = carry
            kb = _heads(k_ref[pl.ds(k0, w), :], hb).astype(BF16)
            z = _bdot(q, kb, tb=True)
            lb_all = _log1m_beta(z)
            lb = jnp.where(causal, lb_all, 0.0) if diagonal else lb_all
            beta = jnp.exp(z + lb_all)
            sofar, lb_total = _running_sum(lb.reshape(hb * t, w), upto, False, True)
            a = beta * jnp.exp(total - run - sofar.reshape(hb, t, w))
            if diagonal:
                a = jnp.where(causal, a, 0.0)
            e = a * _bdot(do, _heads(v_ref[pl.ds(k0, w), :], hb), tb=True)
            earlier, e_total = _running_sum(e.reshape(hb * t, w), before, False, False)
            dz = e * (1.0 - beta) - beta * (pre + earlier.reshape(hb, t, w))
            if diagonal:
                dz = jnp.where(causal, dz, 0.0)
            dv = _bdot(a, do, ta=True)
            dk = _bdot(dz, q, ta=True)
            for hh in range(hb):
                dv_acc[pl.ds(k0, w), _head(hh)] += dv[hh]
                dk_acc[pl.ds(k0, w), _head(hh)] += dk[hh]
            return run + lb_total.reshape(hb, t, 1), pre + e_total.reshape(hb, t, 1), dq + _bdot(dz, kb)

        rem = i % 4
        zero = jnp.zeros((hb, t, 1), F32)
        carry = lax.fori_loop(0, i // 4, lambda j, c: tile(pl.multiple_of(j * 4 * t, 4 * t), 4 * t, c),
                              (zero, zero, jnp.zeros((hb, t, HEAD_DIM), F32)))
        carry = lax.cond(rem >= 2, lambda c: tile(pl.multiple_of((i - rem) * t, 2 * t), 2 * t, c), lambda c: c, carry)
        carry = lax.cond(rem % 2 == 1, lambda c: tile(pl.multiple_of((i - 1) * t, t), t, c), lambda c: c, carry)
        _, _, dq = tile(pl.multiple_of(i * t, t), t, carry, True)
        for hh in range(hb):
            dq_ref[:, _head(hh)] = (dq[hh] * scale).astype(BF16)

        @pl.when(i == nq - 1)
        def _():
            dk_ref[...] = dk_acc[...].astype(BF16)
            dv_ref[...] = dv_acc[...].astype(BF16)

    tile_spec, slab_spec = _attn_specs(l, hb, n_heads)
    out = jax.ShapeDtypeStruct((l, n_heads * HEAD_DIM), BF16)
    return pl.pallas_call(
        body, name=name, grid=(n_heads // hb, nq),
        in_specs=[tile_spec(0), slab_spec(1), slab_spec(2), tile_spec(3), tile_spec(0), tile_spec(0), tile_spec(0)]
        + _after_specs(after),
        out_specs=[tile_spec(0), slab_spec(0), slab_spec(0), tile_spec(0)],
        out_shape=[out, out, out, out],
        scratch_shapes=[pltpu.VMEM((l, hb * HEAD_DIM), F32), pltpu.VMEM((l, hb * HEAD_DIM), F32)],
        compiler_params=_params("parallel", "arbitrary"),
    )(zin, zin, zin, zin, o, tot, doa, *_after_operands(after))


def _shift_rows(x, k, down):
    n = x.shape[0]
    rows = lax.broadcasted_iota(jnp.int32, x.shape, 0)
    if down:
        return jnp.where(rows >= k, pltpu.roll(x, k, 0), 0.0)
    return jnp.where(rows < n - k, pltpu.roll(x, n - k, 0), 0.0)


def _window_sum(x, g, down):
    result = x
    total = x
    for step, k in enumerate((1, 2, 4, 8)):
        total = total + _shift_rows(total, k, down)
        result = jnp.where(g >= step, total, result)
    return result


def _pooled(u, g):
    rows = lax.broadcasted_iota(jnp.int32, (u.shape[0], 1), 0)
    window = jnp.left_shift(2, g)
    cnt = jnp.minimum(rows + 1, window).astype(F32)
    return _window_sum(u, g, True) / cnt - u, cnt


def _pool_fwd(zin, pool_w, pool_scale, u_off, name, after=None):
    l = zin.shape[0]
    n_groups, gd, _ = pool_w.shape

    def body(u_ref, g_ref, w_ref, s_ref, *rest):
        o_ref = rest[-1]
        g = pl.program_id(0)
        pooled, _ = _pooled(u_ref[...], g)
        mixed = _dot(pooled, w_ref[...])
        gate = g_ref[...]
        o_ref[...] = (mixed * s_ref[...] * (gate * _sigmoid(gate))).astype(BF16)

    return pl.pallas_call(
        body, name=name, grid=(n_groups,),
        in_specs=[pl.BlockSpec((l, gd), lambda g: (0, u_off + g)),
                  pl.BlockSpec((l, gd), lambda g: (0, u_off + n_groups + g)),
                  pl.BlockSpec((None, gd, gd), lambda g: (g, 0, 0)),
                  pl.BlockSpec((1, gd), lambda g: (0, g))] + _after_specs(after),
        out_specs=pl.BlockSpec((l, gd), lambda g: (0, g)),
        out_shape=jax.ShapeDtypeStruct((l, n_groups * gd), BF16),
        compiler_params=_params("parallel"),
    )(zin, zin, pool_w, pool_scale, *_after_operands(after))


def _pool_bwd(zin, dop, pool_w, pool_scale, u_off, name):
    l = zin.shape[0]
    n_groups, gd, _ = pool_w.shape

    def body(u_ref, g_ref, w_ref, s_ref, d_ref, du_ref, dg_ref, dw_ref, ds_ref):
        g = pl.program_id(0)
        pooled, cnt = _pooled(u_ref[...], g)
        w = w_ref[...]
        mixed = _dot(pooled, w)
        gate = g_ref[...]
        sg = _sigmoid(gate)
        silu = gate * sg
        dop_v = d_ref[...]
        sc = s_ref[...]
        ds_ref[...] = jnp.sum(dop_v * mixed * silu, axis=0, keepdims=True)
        dg_ref[...] = (dop_v * mixed * sc * (sg * (1.0 + gate * (1.0 - sg)))).astype(BF16)
        dmixed = dop_v * sc * silu
        dw_ref[...] = _dot(pooled, dmixed, ta=True).astype(BF16)
        dpooled = _dot(dmixed, w, tb=True)
        du_ref[...] = (_window_sum(dpooled / cnt, g, False) - dpooled).astype(BF16)

    slab = pl.BlockSpec((l, gd), lambda g: (0, g))
    return pl.pallas_call(
        body, name=name, grid=(n_groups,),
        in_specs=[pl.BlockSpec((l, gd), lambda g: (0, u_off + g)),
                  pl.BlockSpec((l, gd), lambda g: (0, u_off + n_groups + g)),
                  pl.BlockSpec((None, gd, gd), lambda g: (g, 0, 0)),
                  pl.BlockSpec((1, gd), lambda g: (0, g)), slab],
        out_specs=[slab, slab, pl.BlockSpec((None, gd, gd), lambda g: (g, 0, 0)),
                   pl.BlockSpec((1, gd), lambda g: (0, g))],
        out_shape=[jax.ShapeDtypeStruct((l, n_groups * gd), BF16), jax.ShapeDtypeStruct((l, n_groups * gd), BF16),
                   jax.ShapeDtypeStruct(pool_w.shape, BF16), jax.ShapeDtypeStruct((1, n_groups * gd), F32)],
        compiler_params=_params("parallel"),
    )(zin, zin, pool_w, pool_scale, dop)


def _merge_fwd(oa, op, w_au, w_pu, zin, name):
    l, wa = oa.shape
    wp = op.shape[1]
    d = w_au.shape[1]
    tm = _tile(l, 1088, 16)
    tn = _tile(d, 512, 128)
    ma_off = (zin.shape[1] - 2 * d) // tn

    def body(oa_ref, op_ref, wa_ref, wp_ref, ma_ref, mp_ref, ya_ref, yp_ref, mg_ref):
        ya = _dot(oa_ref[...], wa_ref[...])
        yp = _dot(op_ref[...], wp_ref[...])
        ya_ref[...] = ya.astype(BF16)
        yp_ref[...] = yp.astype(BF16)
        mg_ref[...] = (_sigmoid(ma_ref[...]) * ya + _sigmoid(mp_ref[...]) * yp).astype(BF16)

    tile = pl.BlockSpec((tm, tn), lambda i, j: (i, j))
    return pl.pallas_call(
        body, name=name, grid=(l // tm, d // tn),
        in_specs=[pl.BlockSpec((tm, wa), lambda i, j: (i, 0)), pl.BlockSpec((tm, wp), lambda i, j: (i, 0)),
                  pl.BlockSpec((wa, tn), lambda i, j: (0, j)), pl.BlockSpec((wp, tn), lambda i, j: (0, j)),
                  pl.BlockSpec((tm, tn), lambda i, j: (i, ma_off + j)),
                  pl.BlockSpec((tm, tn), lambda i, j: (i, ma_off + d // tn + j))],
        out_specs=[tile, tile, tile],
        out_shape=[jax.ShapeDtypeStruct((l, d), BF16)] * 3,
        compiler_params=_params("parallel", "parallel"),
    )(oa, op, w_au, w_pu, zin, zin)


def _merge_bwd(gout, w_out, zin, ya, yp, name):
    l, d = gout.shape
    tm = _tile(l, 1088, 16)
    tn = _tile(d, 512, 128)
    ma_off = (zin.shape[1] - 2 * d) // tn

    def body(g_ref, w_ref, ma_ref, mp_ref, ya_ref, yp_ref, dya_ref, dyp_ref, dma_ref, dmp_ref):
        dm = _dot(g_ref[...], w_ref[...], tb=True)
        sa = _sigmoid(ma_ref[...])
        sp = _sigmoid(mp_ref[...])
        dya_ref[...] = (dm * sa).astype(BF16)
        dyp_ref[...] = (dm * sp).astype(BF16)
        dma_ref[...] = (dm * ya_ref[...].astype(F32) * (sa * (1.0 - sa))).astype(BF16)
        dmp_ref[...] = (dm * yp_ref[...].astype(F32) * (sp * (1.0 - sp))).astype(BF16)

    tile = pl.BlockSpec((tm, tn), lambda i, j: (i, j))
    out = jax.ShapeDtypeStruct((l, d), BF16)
    return pl.pallas_call(
        body, name=name, grid=(l // tm, d // tn),
        in_specs=[pl.BlockSpec((tm, d), lambda i, j: (i, 0)), pl.BlockSpec((tn, d), lambda i, j: (j, 0)),
                  pl.BlockSpec((tm, tn), lambda i, j: (i, ma_off + j)),
                  pl.BlockSpec((tm, tn), lambda i, j: (i, ma_off + d // tn + j)), tile, tile],
        out_specs=[tile, tile, tile, tile],
        out_shape=[out, out, out, out],
        compiler_params=_params("parallel", "parallel"),
    )(gout, w_out, zin, zin, ya, yp)


def _adamw(parts, w, m, v, name, first=0, into=None):
    n_arrays, n_parts, r, c = parts.shape
    tr = _tile(r, max(8, (256 * 1024) // c // 8 * 8), 8)
    bias1 = 1.0 - ADAM_B1 ** ADAM_STEP
    bias2 = 1.0 - ADAM_B2 ** ADAM_STEP

    def body(p_ref, w_ref, m_ref, v_ref, *rest):
        g_ref, d_ref, nm_ref, nv_ref = rest[-4:]
        g = p_ref[0].astype(F32)
        for j in range(1, n_parts):
            g = g + p_ref[j].astype(F32)
        nm = ADAM_B1 * m_ref[...] + (1.0 - ADAM_B1) * g
        nv = ADAM_B2 * v_ref[...] + (1.0 - ADAM_B2) * (g * g)
        g_ref[...] = g
        nm_ref[...] = nm
        nv_ref[...] = nv
        d_ref[...] = -ADAM_LR * ((nm / bias1) / (jnp.sqrt(nv / bias2) + ADAM_EPS) + ADAM_WD * w_ref[...])

    tile = pl.BlockSpec((None, tr, c), lambda a, i: (a + first, i, 0))
    out = jax.ShapeDtypeStruct(w.shape, F32)
    kept = [] if into is None else list(into)
    return pl.pallas_call(
        body, name=name, grid=(n_arrays, r // tr),
        in_specs=[pl.BlockSpec((None, n_parts, tr, c), lambda a, i: (a, 0, i, 0)), tile, tile, tile]
        + [pl.BlockSpec(memory_space=pl.ANY)] * len(kept),
        out_specs=[tile, tile, tile, tile], out_shape=[out, out, out, out],
        input_output_aliases={4 + k: k for k in range(len(kept))},
        compiler_params=_params("parallel", "parallel"),
    )(parts, w, m, v, *kept)


def _position():
    return lax.axis_index("x"), lax.axis_index("y"), lax.axis_index("c")


def _block_of(ref, axis, size, index):
    idx = [slice(None)] * len(ref.shape)
    idx[axis] = pl.ds(index * size, size)
    return ref.at[tuple(idx)]


HBM_SPEC = pl.BlockSpec(memory_space=pltpu.HBM)
SEM_SPEC = pl.BlockSpec(memory_space=pltpu.SEMAPHORE)
SIDE_EFFECT = pltpu.CompilerParams(has_side_effects=pltpu.SideEffectType.DATAFLOW_SIDE_EFFECTING)


def _split_start(make_copies, n_copies, buffers, name):
    n = len(buffers)

    def body(*refs):
        send_sems, recv_sems = refs[n], refs[n + 1]
        for cp in make_copies(refs[:n], send_sems, recv_sems):
            cp.start()
        refs[-1][...] = jnp.zeros_like(refs[-1])

    sems = pltpu.SemaphoreType.DMA((n_copies,))
    return pl.pallas_call(
        body, name=name, in_specs=[HBM_SPEC] * n,
        out_shape=(sems, sems, *[pltpu.HBM(b.shape, b.dtype) for b in buffers], jax.ShapeDtypeStruct((8, 128), F32)),
        out_specs=(SEM_SPEC, SEM_SPEC, *[HBM_SPEC] * n, pl.BlockSpec(memory_space=pltpu.VMEM)),
        input_output_aliases={i: 2 + i for i in range(n)}, compiler_params=SIDE_EFFECT,
    )(*[pltpu.with_memory_space_constraint(b, pltpu.HBM) for b in buffers])


def _split_wait(make_copies, started, after, name):
    send_sems, recv_sems, *buffers = started[:-1]
    n = len(buffers)

    def body(*refs):
        copies = make_copies(refs[:n], refs[n], refs[n + 1])
        for cp in copies:
            cp.wait_send()
        for cp in copies:
            cp.wait_recv()

    return pl.pallas_call(
        body, name=name, in_specs=[HBM_SPEC] * n + [SEM_SPEC, SEM_SPEC, pl.BlockSpec(memory_space=pl.ANY)],
        out_shape=[pltpu.HBM(b.shape, b.dtype) for b in buffers], out_specs=[HBM_SPEC] * n,
        input_output_aliases={i: i for i in range(n)}, compiler_params=SIDE_EFFECT,
    )(*buffers, send_sems, recv_sems, after)


def _gather_copies(axes, sizes, level):
    def make(fulls, send_sems, recv_sems):
        x, y, c = _position()
        chips = [(1 - x, y), (x, 1 - y), (1 - x, 1 - y)]
        copies = []
        for a, full in enumerate(fulls):
            def copy(k, block, to, full=full, a=a):
                rows = _block_of(full, axes[a], sizes[a], 4 * block[0] + 2 * block[1] + block[2])
                idx = a * (4 if level == 1 else 3) + k
                return pltpu.make_async_remote_copy(src_ref=rows, dst_ref=rows, send_sem=send_sems.at[idx],
                                                    recv_sem=recv_sems.at[idx], device_id=to, device_id_type=MESH)
            if level == 1:
                copies.append(copy(0, (x, y, c), (x, y, 1 - c)))
                copies += [copy(1 + j, (x, y, c), (*chip, c)) for j, chip in enumerate(chips)]
            else:
                copies += [copy(j, (*chip, c), (x, y, 1 - c)) for j, chip in enumerate(chips)]
        return copies
    return make


def _exchange_copies(axes, sizes, layer, n_src):
    flips = [(a, b, d) for a in (0, 1) for b in (0, 1) for d in (0, 1)][1:]

    def make(buffers, send_sems, recv_sems):
        x, y, c = _position()
        my_index = 4 * x + 2 * y + c
        copies = []
        for a in range(n_src):
            for k, flip in enumerate(flips):
                px, py, pc = x ^ flip[0], y ^ flip[1], c ^ flip[2]
                copies.append(pltpu.make_async_remote_copy(
                    src_ref=_block_of(buffers[a], axes[a], sizes[a], 4 * px + 2 * py + pc),
                    dst_ref=buffers[n_src + a].at[layer, my_index],
                    send_sem=send_sems.at[a * 7 + k], recv_sem=recv_sems.at[a * 7 + k],
                    device_id=(px, py, pc), device_id_type=MESH))
        return copies
    return make


def _allgather_small(v, name, after=None):
    r, c = v.shape
    flips = [(a, b, d) for a in (0, 1) for b in (0, 1) for d in (0, 1)][1:]

    def body(v_ref, *rest):
        out_ref, send_sems, recv_sems = rest[-3:]
        x, y, c_ = _position()
        my_index = 4 * x + 2 * y + c_
        out_ref[my_index] = v_ref[...]
        sends = []
        for k, flip in enumerate(flips):
            peer = (x ^ flip[0], y ^ flip[1], c_ ^ flip[2])
            cp = pltpu.make_async_remote_copy(
                src_ref=v_ref, dst_ref=out_ref.at[my_index],
                send_sem=send_sems.at[k], recv_sem=recv_sems.at[k], device_id=peer, device_id_type=MESH)
            cp.start()
            sends.append(cp)
        for k, flip in enumerate(flips):
            px, py, pc = x ^ flip[0], y ^ flip[1], c_ ^ flip[2]
            pltpu.make_async_remote_copy(
                src_ref=v_ref, dst_ref=out_ref.at[4 * px + 2 * py + pc],
                send_sem=send_sems.at[k], recv_sem=recv_sems.at[k],
                device_id=(px, py, pc), device_id_type=MESH).wait_recv()
        for cp in sends:
            cp.wait_send()

    return pl.pallas_call(
        body, name=name,
        in_specs=[pl.BlockSpec(memory_space=pltpu.VMEM)] + _after_specs(after),
        out_specs=pl.BlockSpec(memory_space=pltpu.VMEM),
        out_shape=jax.ShapeDtypeStruct((N_DEV, r, c), v.dtype),
        scratch_shapes=[pltpu.SemaphoreType.DMA((7,)), pltpu.SemaphoreType.DMA((7,))],
    )(v, *_after_operands(after))


def kernel(x, meta_tokens, norm_gain, w_in, pool_w, pool_scale, w_attn_up, w_pool_up, w_out, final_gain, loss_target, m_meta_tokens, m_norm_gain, m_w_in, m_pool_w, m_pool_scale, m_w_attn_up, m_w_pool_up, m_w_out, m_final_gain, v_meta_tokens, v_norm_gain, v_w_in, v_pool_w, v_pool_scale, v_w_attn_up, v_w_pool_up, v_w_out, v_final_gain):
    _, seq, d = x.shape
    n_meta = meta_tokens.shape[0]
    depth = w_in.shape[0]
    sb_width = w_attn_up.shape[1]
    pool_width = w_pool_up.shape[1]
    n_heads = sb_width // HEAD_DIM
    n_groups = pool_w.shape[1]
    gd = pool_w.shape[3]
    assert n_groups == len(POOL_WINDOWS) and gd * n_groups == pool_width
    assert w_in.shape[2] * N_DEV == 4 * sb_width + 2 * pool_width + 2 * d
    l_real = n_meta + seq
    l_pad = -(-l_real // ATT_BLOCK) * ATT_BLOCK
    my_index = 4 * lax.axis_index("x") + 2 * lax.axis_index("y") + lax.axis_index("c")

    me = jnp.reshape(my_index, (1,)).astype(jnp.int32)
    g_named = [("w_in", w_in), ("pool_w", pool_w), ("w_attn_up", w_attn_up), ("w_pool_up", w_pool_up), ("w_out", w_out)]
    g_axes = [1, 1, 1, 1, 0]
    g_sizes = [w.shape[1 + ax] for (_, w), ax in zip(g_named, g_axes)]
    level1 = _gather_copies(g_axes, g_sizes, 1)
    level2 = _gather_copies(g_axes, g_sizes, 2)

    def gather_start(i, after=None):
        fulls = [_cast_place(w, i, ax, me, "cast_" + nm, after) for (nm, w), ax in zip(g_named, g_axes)]
        return _split_start(level1, 4 * len(fulls), fulls, "gather1_start_%d" % i)

    def gather_forward(i, started, after):
        arrived = _split_wait(level1, started, after, "gather1_wait_%d" % i)
        return _split_start(level2, 3 * len(arrived), arrived, "gather2_start_%d" % i)

    meta_all = _allgather_small(meta_tokens, "allgather_meta")
    meta_full = jnp.transpose(meta_all, (1, 0, 2)).reshape(n_meta, d)
    first = gather_start(0)
    second = gather_forward(0, first, first[-1])

    pad_rows = l_pad - l_real
    hs = jnp.concatenate([meta_full, x[0], jnp.zeros((pad_rows, d), F32)], axis=0)
    target = jnp.concatenate([jnp.zeros((n_meta, d), F32), loss_target[0], jnp.zeros((pad_rows, d), F32)], axis=0)
    u_off = 4 * sb_width // gd
    saved, weights = [], []
    for i in range(depth):
        h = _rms_fwd(hs, norm_gain[i][None], "rms_fwd")
        weights.append(_split_wait(level2, second, h, "gather2_wait_%d" % i))
        wi, pw, wau, wpu, wo = weights[i]
        more = i + 1 < depth
        first = gather_start(i + 1, wi) if more else None
        zin = _matmul(h, wi, after=first[-1] if more else None, name="mm_zin")
        o, oa, tot = _attn_fwd(zin, n_heads, "attn_fwd")
        op = _pool_fwd(zin, pw, pool_scale[i][None], u_off, "pool_fwd")
        ya, yp, merged = _merge_fwd(oa, op, wau, wpu, zin, "merge_fwd")
        second = gather_forward(i + 1, first, merged) if more else None
        saved.append((hs, h, zin, o, tot, oa, op, ya, yp, merged))
        hs = _matmul(merged, wo, add=hs, after=second[-1] if more else None, name="mm_out")
    g, gb, d_final_gain, loss_part = _final_loss(hs, final_gain[None], target, n_meta, l_real, "final_loss")
    loss = lax.psum(loss_part[0, 0], ("x", "y", "c"))

    d_norm_gain = [None] * depth
    d_pool_scale = [None] * depth
    axes_a, axes_b = [1, 1, 0], [1, 1]
    blocks_a, blocks_b = [w_attn_up[0], w_pool_up[0], w_out[0]], [w_in[0], pool_w[0]]
    sizes_a = [b.shape[ax] for b, ax in zip(blocks_a, axes_a)]
    sizes_b = [b.shape[ax] for b, ax in zip(blocks_b, axes_b)]
    land_a = [lax.empty((depth, N_DEV, *b.shape), BF16) for b in blocks_a]
    land_b = [lax.empty((depth - 1, N_DEV, *b.shape), BF16) for b in blocks_b]
    land_b0 = [lax.empty((1, N_DEV, *b.shape), BF16) for b in blocks_b]

    def exchange_start(grads, landing, axes, sizes, slot, name):
        landing = [lax.dynamic_update_slice(
            zone, lax.dynamic_slice_in_dim(grad, my_index * size, size, ax)[None, None],
            (slot, my_index) + (0,) * grad.ndim) for zone, grad, ax, size in zip(landing, grads, axes, sizes)]
        copies = _exchange_copies(axes, sizes, slot, len(grads))
        return copies, _split_start(copies, 7 * len(grads), list(grads) + landing, name)

    def exchange_wait(pending, n_src, after, name):
        return _split_wait(pending[0], pending[1], after, name)[n_src:]

    pend_a = pend_b = None
    for i in reversed(range(depth)):
        wi, pw, wau, wpu, wo = weights[i]
        hs_in, h, zin, o, tot, oa, op, ya, yp, merged = saved[i]
        dya, dyp, dma, dmp = _merge_bwd(gb, wo, zin, ya, yp, "merge_bwd")
        dw_out = _matmul(merged, gb, ta=True, out_dtype=BF16, name="mm_dw_out", tm=1024, tk=2176)
        doa = _matmul(dya, wau, tb=True, name="mm_doa")
        dw_au = _matmul(oa, dya, ta=True, out_dtype=BF16, name="mm_dw_au", tm=1024, tk=2176)
        dop = _matmul(dyp, wpu, tb=True, name="mm_dop")
        dw_pu = _matmul(op, dyp, ta=True, out_dtype=BF16, name="mm_dw_pu", tm=1024, tk=2176)
        if pend_a is not None:
            land_a = exchange_wait(pend_a, 3, dw_pu, "exchange_a_wait_%d" % (i + 1))
        pend_a = exchange_start([dw_au, dw_pu, dw_out], land_a, axes_a, sizes_a, i, "exchange_a_start_%d" % i)
        dq, dk, dv, dga = _attn_bwd(zin, o, tot, doa, n_heads, "attn_bwd", after=pend_a[1][-1])
        du, dgp, dpw, dps = _pool_bwd(zin, dop, pw, pool_scale[i][None], u_off, "pool_bwd")
        dzin = [dq, dk, dv, dga, du, dgp, dma, dmp]
        dw_in = _matmul(h, dzin, ta=True, out_dtype=BF16, name="mm_dw_in", tm=1024, tk=1088)
        if pend_b is not None:
            land_b = exchange_wait(pend_b, 2, dw_in, "exchange_b_wait_%d" % (i + 1))
        pend_b = exchange_start([dw_in, dpw], land_b if i > 0 else land_b0, axes_b, sizes_b, max(i - 1, 0),
                                "exchange_b_start_%d" % i)
        dh = _matmul(dzin, wi, tb=True, after=pend_b[1][-1], name="mm_dh")
        g, gb, dng = _rms_bwd(dh, hs_in, norm_gain[i][None], g, "rms_bwd")
        d_norm_gain[i] = dng
        d_pool_scale[i] = dps
    land_a = exchange_wait(pend_a, 3, gb, "exchange_a_wait_0")
    grad_x = g[n_meta:l_real][None]

    def sharded(parts, w, m, v, name, first=0, into=None):
        flat = (depth, -1, w.shape[-1])
        return _adamw(parts.reshape(parts.shape[0], N_DEV, -1, w.shape[-1]), w.reshape(flat), m.reshape(flat),
                      v.reshape(flat), name, first, into)

    out_au = sharded(land_a[0], w_attn_up, m_w_attn_up, v_w_attn_up, "adamw_w_attn_up")
    out_pu = sharded(land_a[1], w_pool_up, m_w_pool_up, v_w_pool_up, "adamw_w_pool_up")
    out_wo = sharded(land_a[2], w_out, m_w_out, v_w_out, "adamw_w_out")
    out_wi = out_pw = None
    if depth > 1:
        out_wi = sharded(land_b[0], w_in, m_w_in, v_w_in, "adamw_w_in", 1)
        out_pw = sharded(land_b[1], pool_w, m_pool_w, v_pool_w, "adamw_pool_w", 1)
    land_b0 = exchange_wait(pend_b, 2, out_wo[0] if out_wi is None else out_wi[0], "exchange_b_wait_0")
    out_wi = sharded(land_b0[0], w_in, m_w_in, v_w_in, "adamw_w_in_0", 0, out_wi)
    out_pw = sharded(land_b0[1], pool_w, m_pool_w, v_pool_w, "adamw_pool_w_0", 0, out_pw)
    out_wi, out_pw, out_au, out_pu, out_wo = [
        [t.reshape(w.shape) for t in res] for res, w in
        zip([out_wi, out_pw, out_au, out_pu, out_wo], [w_in, pool_w, w_attn_up, w_pool_up, w_out])]

    zeros_ps = jnp.zeros((depth, d - pool_width), F32)
    small_rows = [jnp.concatenate(d_norm_gain, axis=0),
                  jnp.concatenate([jnp.concatenate(d_pool_scale, axis=0), zeros_ps], axis=1), d_final_gain]
    n_small = 2 * depth + 1
    small_pad = -(-n_small // 8) * 8
    small = jnp.concatenate(small_rows + [jnp.zeros((small_pad - n_small, d), F32), g[:n_meta]], axis=0)
    small_all = _allgather_small(small, "allgather_small", after=land_b0[0])

    def replicated(rows, width, w, m, v, name):
        parts = lax.slice(small_all, (0, rows[0], 0), (N_DEV, rows[1], width))
        return [t[0] for t in _adamw(parts[None], w[None], m[None], v[None], name)]

    out_ng = replicated((0, depth), d, norm_gain, m_norm_gain, v_norm_gain, "adamw_norm_gain")
    out_ps = replicated((depth, 2 * depth), pool_width, pool_scale, m_pool_scale, v_pool_scale, "adamw_pool_scale")
    out_fg = [t[0] for t in replicated((2 * depth, 2 * depth + 1), d, final_gain[None], m_final_gain[None],
                                       v_final_gain[None], "adamw_final_gain")]
    cols = d // N_DEV
    meta_parts = lax.dynamic_slice(small_all, (0, small_pad, my_index * cols), (N_DEV, n_meta, cols))
    out_meta = [t[0] for t in _adamw(meta_parts[None], meta_tokens[None], m_meta_tokens[None], v_meta_tokens[None],
                                     "adamw_meta")]

    by_weight = [out_meta, out_ng, out_wi, out_pw, out_ps, out_au, out_pu, out_wo, out_fg]
    return (loss, grad_x, *[o[0] for o in by_weight], *[o[1] for o in by_weight],
            *[o[2] for o in by_weight], *[o[3] for o in by_weight])
```
